```python
import math
import jax
import jax.numpy as jnp
from jax import lax
import numpy as np

D_MODEL = 4096
BATCH = 2
SEQ = 4096
DEPTH = 2

CTX_LEN = 256
GRID_W = 64
N_SUB = 3
N_MOD = 3 * N_SUB
D_FF = 2 * D_MODEL
EPS = 1e-6
ROPE_THETA = 10000.0
Q_BLOCK = 128
MLA_HEADS = D_MODEL // 256
Q_LORA = D_MODEL // 4
KV_LORA = D_MODEL // 8
NOPE_DIM = 128
ROPE_DIM = 64
V_DIM = 128
MLA_SCALE = (NOPE_DIM + ROPE_DIM) ** -0.5
HY_WIDTH = D_MODEL // 2
HY_EMB = 33
HY_HIDDEN = 64
HY_TARGET = 1e-2
HY_SHORT_DECAY_PCT = 0.3
HY_LONG_DECAY_PCT = 1.5
DIFF_HEADS = D_MODEL // 256
DIFF_DIM = 128
DIFF_SCALE = DIFF_DIM ** -0.5
Z_Q_END = Q_LORA
Z_KV_END = Q_LORA + KV_LORA + ROPE_DIM
AB_IN = Z_KV_END + 3 * HY_WIDTH
AB_MIX = MLA_HEADS * V_DIM + HY_WIDTH
DIFF_HD = DIFF_HEADS * 2 * DIFF_DIM
C_IN = 3 * DIFF_HD
N_EVEN = (DEPTH + 1) // 2
N_ODD = DEPTH // 2

kernel_name = "hybrid_mla_hyena_diffattn_macaron_dit"


def rms_norm(x, g):
    xf = x.astype(jnp.float32)
    y = xf * lax.rsqrt(jnp.mean(xf * xf, axis=-1, keepdims=True) + EPS)
    return (y * g.astype(jnp.float32)).astype(x.dtype)


def modulate(h, m):
    return h * (1.0 + m[:, 1][:, None]) + m[:, 0][:, None]


def swiglu(h, w_gate, w_up, w_down):
    return (jax.nn.silu(h @ w_gate) * (h @ w_up)) @ w_down


def half_ffn(x, m, g, w_gate, w_up, w_down):
    h = modulate(rms_norm(x, g), m)
    return x + 0.5 * m[:, 2][:, None] * swiglu(h, w_gate, w_up, w_down)


def axial_rope(x, row_pos, col_pos):
    r = x.shape[-1]
    n = r // 4
    inv = ROPE_THETA ** (-jnp.arange(n, dtype=jnp.float32) / n)
    ang = jnp.concatenate([row_pos[:, None] * inv, col_pos[:, None] * inv], axis=-1)
    cos = jnp.cos(ang)[:, None, :]
    sin = jnp.sin(ang)[:, None, :]
    xf = x.astype(jnp.float32).reshape(*x.shape[:-1], r // 2, 2)
    x1, x2 = xf[..., 0], xf[..., 1]
    out = jnp.stack([x1 * cos - x2 * sin, x1 * sin + x2 * cos], axis=-1)
    return out.reshape(x.shape).astype(x.dtype)


def attend_blocked(q, k, v, scale):
    b, sq, h, dk = q.shape
    dv = v.shape[-1]
    qb = jnp.moveaxis(q.reshape(b, sq // Q_BLOCK, Q_BLOCK, h, dk), 1, 0)

    def block(qi):
        s = jnp.einsum("bqhd,bkhd->bhqk", qi, k, preferred_element_type=jnp.float32) * scale
        p = jax.nn.softmax(s, axis=-1).astype(v.dtype)
        return jnp.einsum("bhqk,bkhd->bqhd", p, v)

    o = lax.map(block, qb)
    return jnp.moveaxis(o, 0, 1).reshape(b, sq, h, dv)


def diff_attend_blocked(q, k, v, lam, scale):
    b, sq, h, _, d = q.shape
    dv = v.shape[-1]
    qb = jnp.moveaxis(q.reshape(b, sq // Q_BLOCK, Q_BLOCK, h, 2, d), 1, 0)

    def block(qi):
        s = jnp.einsum("bqhcd,bkhcd->bhcqk", qi, k, preferred_element_type=jnp.float32) * scale
        p = jax.nn.softmax(s, axis=-1)
        a = (p[:, :, 0] - lam * p[:, :, 1]).astype(v.dtype)
        return jnp.einsum("bhqk,bkhe->bqhe", a, v)

    o = lax.map(block, qb)
    return jnp.moveaxis(o, 0, 1).reshape(b, sq, h, dv)


def mla_keys_values(zkv, kv_norm_g, w_ukv, pos):
    b, L, _ = zkv.shape
    ckv = rms_norm(zkv[..., :KV_LORA], kv_norm_g)
    k_rope = zkv[..., KV_LORA:][:, :, None, :]
    if pos is not None:
        k_rope = axial_rope(k_rope, *pos)
    kv = (ckv @ w_ukv).reshape(b, L, MLA_HEADS, NOPE_DIM + V_DIM)
    k = jnp.concatenate([kv[..., :NOPE_DIM], jnp.broadcast_to(k_rope, (b, L, MLA_HEADS, ROPE_DIM))], axis=-1)
    return k, kv[..., NOPE_DIM:]


def mla_queries(zq, q_norm_g, w_uq, pos):
    b, L, _ = zq.shape
    q = (rms_norm(zq, q_norm_g) @ w_uq).reshape(b, L, MLA_HEADS, NOPE_DIM + ROPE_DIM)
    q_rope = q[..., NOPE_DIM:]
    if pos is not None:
        q_rope = axial_rope(q_rope, *pos)
    return jnp.concatenate([q[..., :NOPE_DIM], q_rope], axis=-1)


def short_conv(u, w, bias):
    up = jnp.pad(u, ((0, 0), (1, 1), (0, 0)))
    return up[:, :-2] * w[0] + up[:, 1:-1] * w[1] + up[:, 2:] * w[2] + bias


def hyena_filters(L, w1, b1, w2, b2, w3, b3, freq, w_out):
    t = jnp.linspace(0.0, 1.0, L, dtype=jnp.float32)[:, None]
    bands = (HY_EMB - 1) // 2
    ang = (2.0 * math.pi / L) * jnp.arange(L, dtype=jnp.float32)[:, None] * jnp.linspace(1e-4, bands - 1, bands, dtype=jnp.float32)[None, :]
    z = jnp.concatenate([t, jnp.cos(ang), -jnp.sin(ang)], axis=-1)
    a = jnp.sin(freq * (z @ w1 + b1))
    a = jnp.sin(freq * (a @ w2 + b2))
    a = jnp.sin(freq * (a @ w3 + b3))
    h = (a @ w_out).astype(jnp.float32).reshape(L, 2, HY_WIDTH)
    deltas = jnp.abs(jnp.linspace(math.log(HY_TARGET) / HY_LONG_DECAY_PCT, math.log(HY_TARGET) / HY_SHORT_DECAY_PCT, HY_WIDTH, dtype=jnp.float32))
    window = jnp.exp(-t * deltas[None, :])
    h = h * window[:, None, :]
    return h[:, 0], h[:, 1]


def bidir_long_conv(u, h_fwd, h_bwd, bias):
    L = u.shape[1]
    g = jnp.concatenate([h_fwd, jnp.zeros_like(h_fwd[:1]), h_bwd[:0:-1]], axis=0)
    uf = jnp.fft.rfft(u.astype(jnp.float32), n=2 * L, axis=1)
    gf = jnp.fft.rfft(g, n=2 * L, axis=0)
    y = jnp.fft.irfft(uf * gf[None], n=2 * L, axis=1)[:, :L]
    return (y + u.astype(jnp.float32) * bias.astype(jnp.float32)).astype(u.dtype)


def hyena_branch(zh, conv_w, conv_b, filt, hy_bias):
    L = zh.shape[1]
    u = short_conv(zh, conv_w, conv_b)
    x0, x1, v = jnp.split(u, 3, axis=-1)
    h_fwd, h_bwd = hyena_filters(L, *filt)
    return x0 * bidir_long_conv(v * x1, h_fwd, h_bwd, hy_bias)


def ab_mixer(h, hc, pos, need_ctx, w_in, q_norm_g, kv_norm_g, w_uq, w_ukv, conv_w, conv_b, filt, hy_bias, w_out):
    b, L, _ = h.shape
    z = h @ w_in
    k, v = mla_keys_values(z[..., Z_Q_END:Z_KV_END], kv_norm_g, w_ukv, pos)
    if need_ctx:
        zc = hc @ w_in
        zc_kv = zc[..., Z_Q_END:Z_KV_END]
    else:
        zc_kv = hc @ w_in[:, Z_Q_END:Z_KV_END]
    kc, vc = mla_keys_values(zc_kv, kv_norm_g, w_ukv, None)
    q = mla_queries(z[..., :Z_Q_END], q_norm_g, w_uq, pos)
    att = attend_blocked(q, jnp.concatenate([kc, k], axis=1), jnp.concatenate([vc, v], axis=1), MLA_SCALE)
    hy = hyena_branch(z[..., Z_KV_END:], conv_w, conv_b, filt, hy_bias)
    y = jnp.concatenate([att.reshape(b, L, MLA_HEADS * V_DIM), hy], axis=-1) @ w_out
    if not need_ctx:
        return y, None
    lc = hc.shape[1]
    qc = mla_queries(zc[..., :Z_Q_END], q_norm_g, w_uq, None)
    attc = attend_blocked(qc, kc, vc, MLA_SCALE)
    hyc = hyena_branch(zc[..., Z_KV_END:], conv_w, conv_b, filt, hy_bias)
    yc = jnp.concatenate([attc.reshape(b, lc, MLA_HEADS * V_DIM), hyc], axis=-1) @ w_out
    return y, yc


def diff_mixer(h, hc, pos, need_ctx, w_in, lam_p, subln_g, w_out, lambda_init):
    lam_p = lam_p.astype(jnp.float32)
    lam = jnp.exp(jnp.sum(lam_p[0] * lam_p[1])) - jnp.exp(jnp.sum(lam_p[2] * lam_p[3])) + lambda_init

    def heads_qk(t, rope):
        b, L = t.shape[:2]
        t = t.reshape(b, L, 2 * DIFF_HEADS, DIFF_DIM)
        if rope:
            t = axial_rope(t, *pos)
        return t.reshape(b, L, DIFF_HEADS, 2, DIFF_DIM)

    def heads_v(t):
        return t.reshape(t.shape[0], t.shape[1], DIFF_HEADS, 2 * DIFF_DIM)

    def project_out(o):
        o = rms_norm(o, subln_g) * (1.0 - lambda_init)
        return o.reshape(o.shape[0], o.shape[1], DIFF_HD) @ w_out

    z = h @ w_in
    q = heads_qk(z[..., :DIFF_HD], True)
    k = heads_qk(z[..., DIFF_HD:2 * DIFF_HD], True)
    v = heads_v(z[..., 2 * DIFF_HD:])
    if need_ctx:
        zc = hc @ w_in
        zc_kv = zc[..., DIFF_HD:]
    else:
        zc_kv = hc @ w_in[:, DIFF_HD:]
    kc = heads_qk(zc_kv[..., :DIFF_HD], False)
    vc = heads_v(zc_kv[..., DIFF_HD:])
    o = diff_attend_blocked(q, jnp.concatenate([kc, k], axis=1), jnp.concatenate([vc, v], axis=1), lam, DIFF_SCALE)
    y = project_out(o)
    if not need_ctx:
        return y, None
    qc = heads_qk(zc[..., :DIFF_HD], False)
    yc = project_out(diff_attend_blocked(qc, kc, vc, lam, DIFF_SCALE))
    return y, yc


def setup_inputs(seed: int = 0) -> dict:
    key = jax.random.key(seed)
    keys = iter(jax.random.split(key, 40))

    def nrm(shape, scale):
        return jax.random.normal(next(keys), shape, jnp.float32) * scale

    D = D_MODEL
    return {
        "x": nrm((BATCH, SEQ, D), 1.0),
        "c": nrm((BATCH, D), 1.0),
        "ctx": nrm((BATCH, CTX_LEN, D), 1.0),
        "c_ctx": nrm((D,), 1.0),
        "mod_w": nrm((DEPTH, D, N_MOD * D), 0.5 * D ** -0.5),
        "mod_b": nrm((DEPTH, N_MOD * D), 0.02),
        "norm_g": 1.0 + nrm((DEPTH, N_SUB, D), 0.02),
        "ffn_w_gate": nrm((DEPTH, 2, D, D_FF), D ** -0.5),
        "ffn_w_up": nrm((DEPTH, 2, D, D_FF), D ** -0.5),
        "ffn_w_down": nrm((DEPTH, 2, D_FF, D), D_FF ** -0.5),
        "ab_w_in": nrm((N_EVEN, D, AB_IN), D ** -0.5),
        "mla_q_norm_g": 1.0 + nrm((N_EVEN, Q_LORA), 0.02),
        "mla_kv_norm_g": 1.0 + nrm((N_EVEN, KV_LORA), 0.02),
        "mla_w_uq": nrm((N_EVEN, Q_LORA, MLA_HEADS * (NOPE_DIM + ROPE_DIM)), Q_LORA ** -0.5),
        "mla_w_ukv": nrm((N_EVEN, KV_LORA, MLA_HEADS * (NOPE_DIM + V_DIM)), KV_LORA ** -0.5),
        "hy_conv_w": nrm((N_EVEN, 3, 3 * HY_WIDTH), 3.0 ** -0.5),
        "hy_conv_b": nrm((N_EVEN, 3 * HY_WIDTH), 0.02),
        "hy_w1": nrm((N_EVEN, HY_EMB, HY_HIDDEN), HY_EMB ** -0.5),
        "hy_b1": nrm((N_EVEN, HY_HIDDEN), 0.02),
        "hy_w2": nrm((N_EVEN, HY_HIDDEN, HY_HIDDEN), HY_HIDDEN ** -0.5),
        "hy_b2": nrm((N_EVEN, HY_HIDDEN), 0.02),
        "hy_w3": nrm((N_EVEN, HY_HIDDEN, HY_HIDDEN), HY_HIDDEN ** -0.5),
        "hy_b3": nrm((N_EVEN, HY_HIDDEN), 0.02),
        "hy_freq": 1.0 + nrm((N_EVEN, HY_HIDDEN), 0.02),
        "hy_w_out": nrm((N_EVEN, HY_HIDDEN, 2 * HY_WIDTH), 0.05 * HY_HIDDEN ** -0.5),
        "hy_bias": nrm((N_EVEN, HY_WIDTH), 0.3),
        "ab_w_out": nrm((N_EVEN, AB_MIX, D), AB_MIX ** -0.5),
        "c_w_in": nrm((N_ODD, D, C_IN), D ** -0.5),
        "c_lambda": nrm((N_ODD, 4, DIFF_DIM), 0.1),
        "c_subln_g": 1.0 + nrm((N_ODD, 2 * DIFF_DIM), 0.02),
        "c_w_out": nrm((N_ODD, DIFF_HD, D), DIFF_HD ** -0.5),
        "final_norm_g": 1.0 + nrm((D,), 0.02),
    }


def reference(x, c, ctx, c_ctx, mod_w, mod_b, norm_g, ffn_w_gate, ffn_w_up, ffn_w_down, ab_w_in, mla_q_norm_g, mla_kv_norm_g, mla_w_uq, mla_w_ukv, hy_conv_w, hy_conv_b, hy_w1, hy_b1, hy_w2, hy_b2, hy_w3, hy_b3, hy_freq, hy_w_out, hy_bias, ab_w_out, c_w_in, c_lambda, c_subln_g, c_w_out, final_norm_g):
    seq = x.shape[1]
    rows = seq // GRID_W
    row_pos = jnp.broadcast_to(jnp.arange(rows, dtype=jnp.float32)[:, None], (rows, GRID_W)).reshape(-1)
    col_pos = jnp.broadcast_to(jnp.arange(GRID_W, dtype=jnp.float32)[None, :], (rows, GRID_W)).reshape(-1)
    pos = (row_pos, col_pos)
    silu_c = jax.nn.silu(c)
    silu_cc = jax.nn.silu(c_ctx)[None, :]
    xc = ctx
    for layer in range(DEPTH):
        need_ctx = layer < DEPTH - 1
        mods = (silu_c @ mod_w[layer] + mod_b[layer]).reshape(-1, N_SUB, 3, D_MODEL)
        modc = (silu_cc @ mod_w[layer] + mod_b[layer]).reshape(1, N_SUB, 3, D_MODEL)
        x = half_ffn(x, mods[:, 0], norm_g[layer, 0], ffn_w_gate[layer, 0], ffn_w_up[layer, 0], ffn_w_down[layer, 0])
        xc = half_ffn(xc, modc[:, 0], norm_g[layer, 0], ffn_w_gate[layer, 0], ffn_w_up[layer, 0], ffn_w_down[layer, 0])
        h = modulate(rms_norm(x, norm_g[layer, 1]), mods[:, 1])
        hc = modulate(rms_norm(xc, norm_g[layer, 1]), modc[:, 1])
        i = layer // 2
        if layer % 2 == 0:
            filt = (hy_w1[i], hy_b1[i], hy_w2[i], hy_b2[i], hy_w3[i], hy_b3[i], hy_freq[i], hy_w_out[i])
            y, yc = ab_mixer(h, hc, pos, need_ctx, ab_w_in[i], mla_q_norm_g[i], mla_kv_norm_g[i], mla_w_uq[i], mla_w_ukv[i], hy_conv_w[i], hy_conv_b[i], filt, hy_bias[i], ab_w_out[i])
        else:
            lambda_init = 0.8 - 0.6 * math.exp(-0.3 * layer)
            y, yc = diff_mixer(h, hc, pos, need_ctx, c_w_in[i], c_lambda[i], c_subln_g[i], c_w_out[i], lambda_init)
        x = x + mods[:, 1, 2][:, None] * y
        x = half_ffn(x, mods[:, 2], norm_g[layer, 2], ffn_w_gate[layer, 1], ffn_w_up[layer, 1], ffn_w_down[layer, 1])
        if need_ctx:
            xc = xc + modc[:, 1, 2][:, None] * yc
            xc = half_ffn(xc, modc[:, 2], norm_g[layer, 2], ffn_w_gate[layer, 1], ffn_w_up[layer, 1], ffn_w_down[layer, 1])
    return rms_norm(x, final_norm_g)
```

```python
import functools
import math

import jax
import jax.numpy as jnp
from jax import lax
from jax.experimental import pallas as pl
from jax.experimental.pallas import tpu as pltpu

F32 = jnp.float32
BF16 = jnp.bfloat16

EPS = 1e-6
ROPE_THETA = 10000.0
GRID_W = 64
N_SUB = 3
NOPE_DIM = 128
ROPE_DIM = 64
V_DIM = 128
DIFF_DIM = 128
HY_EMB = 33
HY_TARGET = 1e-2
HY_SHORT_DECAY_PCT = 0.3
HY_LONG_DECAY_PCT = 1.5

LANES = 128
MXU_DIM_V7X = 256
VMEM_BYTES_V7X = 64 * 1024 * 1024
VMEM_COMPILER_RESERVE = 8 * 1024 * 1024
MLA_HEAD_PAD = MXU_DIM_V7X


def _params(semantics, block_bytes, temp_bytes=0):
    want = 2 * block_bytes + temp_bytes
    limit = min(VMEM_BYTES_V7X - VMEM_COMPILER_RESERVE, max(want, 32 * 1024 * 1024))
    return pltpu.CompilerParams(dimension_semantics=semantics, vmem_limit_bytes=int(limit))


def _nbytes(shape, dtype):
    return math.prod(shape) * jnp.dtype(dtype).itemsize


def _rms(x, g):
    return x * lax.rsqrt(jnp.mean(x * x, axis=-1, keepdims=True) + EPS) * g


def _matmul(a, w, *, tm, tn, tk, epilogue, out_shape, out_specs, extras=(), name):
    m, kdim = a.shape
    n = w.shape[1]
    tm, tn, tk = min(tm, m), min(tn, n), min(tk, kdim)
    assert m % tm == 0 and n % tn == 0 and kdim % tk == 0, (name, a.shape, w.shape, tm, tn, tk)
    nk = kdim // tk
    n_ex, n_out = len(extras), len(out_shape)

    def body(*refs):
        a_ref, w_ref = refs[0], refs[1]
        ex = refs[2:2 + n_ex]
        outs = refs[2 + n_ex:2 + n_ex + n_out]
        prod = jnp.dot(a_ref[...], w_ref[...], preferred_element_type=F32)
        if nk == 1:
            epilogue(prod, ex, outs)
            return
        acc_ref = refs[-1]
        k = pl.program_id(2)

        @pl.when(k == 0)
        def _():
            acc_ref[...] = prod

        if nk > 2:
            @pl.when((k > 0) & (k < nk - 1))
            def _():
                acc_ref[...] += prod

        @pl.when(k == nk - 1)
        def _():
            epilogue(acc_ref[...] + prod, ex, outs)

    in_specs = [pl.BlockSpec((tm, tk), lambda i, j, k: (i, k)),
                pl.BlockSpec((tk, tn), lambda i, j, k: (k, j))]
    in_specs += [spec for _, spec in extras]
    block_bytes = _nbytes((tm, tk), a.dtype) + _nbytes((tk, tn), w.dtype)
    for arr, spec in extras:
        block_bytes += _nbytes([d for d in spec.block_shape if d is not None], arr.dtype)
    for sds, spec in zip(out_shape, out_specs):
        block_bytes += _nbytes([d for d in spec.block_shape if d is not None], sds.dtype)
    acc_bytes = _nbytes((tm, tn), F32)
    return pl.pallas_call(
        body,
        grid=(m // tm, n // tn, nk),
        in_specs=in_specs,
        out_specs=list(out_specs),
        out_shape=list(out_shape),
        scratch_shapes=[pltpu.VMEM((tm, tn), F32)] if nk > 1 else [],
        compiler_params=_params(("parallel", "parallel", "arbitrary"), block_bytes, 8 * acc_bytes),
        name=name,
    )(a, w, *[arr for arr, _ in extras])


def _mm_plain(a, w, out_dtype, *, tm=1024, tn=1024, tk=2048, name):
    m, n = a.shape[0], w.shape[1]
    tm, tn = min(tm, m), min(tn, n)

    def epilogue(acc, ex, outs):
        outs[0][...] = acc.astype(out_dtype)

    return _matmul(a, w, tm=tm, tn=tn, tk=tk, epilogue=epilogue,
                   out_shape=[jax.ShapeDtypeStruct((m, n), out_dtype)],
                   out_specs=[pl.BlockSpec((tm, tn), lambda i, j, k: (i, j))], name=name)[0]


def _mm_residual(a, w, res, mods, sub, coef, rows_per_group, *, tm=1024, tn=1024, tk=2048, name):
    m, n = a.shape[0], w.shape[1]
    tm, tn = min(tm, m, rows_per_group), min(tn, n)
    tiles_per_group = rows_per_group // tm
    gate_row = 3 * sub + 2

    def epilogue(acc, ex, outs):
        res_ref, mod_ref = ex
        gate = mod_ref[gate_row:gate_row + 1, :]
        outs[0][...] = res_ref[...] + (coef * gate) * acc

    extras = [(res, pl.BlockSpec((tm, tn), lambda i, j, k: (i, j))),
              (mods, pl.BlockSpec((None, 3 * N_SUB, tn), lambda i, j, k: (i // tiles_per_group, 0, j)))]
    return _matmul(a, w, tm=tm, tn=tn, tk=tk, epilogue=epilogue, extras=extras,
                   out_shape=[jax.ShapeDtypeStruct((m, n), F32)],
                   out_specs=[pl.BlockSpec((tm, tn), lambda i, j, k: (i, j))], name=name)[0]


def _rope_tables(seq, n_freq, group, lo):
    pos = jnp.arange(seq, dtype=jnp.int32)
    row_pos = (pos // GRID_W).astype(F32)
    col_pos = (pos % GRID_W).astype(F32)
    inv = ROPE_THETA ** (-jnp.arange(n_freq, dtype=F32) / n_freq)
    ang = jnp.concatenate([row_pos[:, None] * inv, col_pos[:, None] * inv], axis=-1)
    cos, sin = jnp.cos(ang), jnp.sin(ang)
    hi = group - lo - 4 * n_freq
    cos_t = jnp.concatenate([jnp.ones((seq, lo), F32), cos, cos, jnp.ones((seq, hi), F32)], axis=-1)
    sin_t = jnp.concatenate([jnp.zeros((seq, lo), F32), -sin, sin, jnp.zeros((seq, hi), F32)], axis=-1)
    return cos_t, sin_t


def _rope_apply(x, cos_ref, sin_ref, group, lo, half):
    tn = x.shape[-1]
    reps = tn // group
    cos, sin = cos_ref[...], sin_ref[...]
    if reps > 1:
        cos = jnp.concatenate([cos] * reps, axis=1)
        sin = jnp.concatenate([sin] * reps, axis=1)
    lane = lax.broadcasted_iota(jnp.int32, x.shape, 1) % group
    first = (lane >= lo) & (lane < lo + half)
    partner = jnp.where(first, pltpu.roll(x, tn - half, 1), pltpu.roll(x, half, 1))
    return x * cos + partner * sin


def _deinterleave(w, group, lo, width):
    rows, cols = w.shape
    w3 = w.reshape(rows, cols // group, group)
    rot = w3[..., lo:lo + width]
    parts = [w3[..., :lo], rot[..., 0::2], rot[..., 1::2], w3[..., lo + width:]]
    return jnp.concatenate(parts, axis=-1).reshape(rows, cols)


def _mm_rope(a, w, tables, scale, seq, out_dtype, *, group, lo, half, tm=1024, tn=1024, tk=2048, name):
    m, n = a.shape[0], w.shape[1]
    tm, tn = min(tm, m, seq), min(tn, n)
    tiles_per_seq = seq // tm

    def epilogue(acc, ex, outs):
        y = acc if tables is None else _rope_apply(acc, ex[0], ex[1], group, lo, half)
        outs[0][...] = (y * scale).astype(out_dtype)

    extras = []
    if tables is not None:
        spec = pl.BlockSpec((tm, group), lambda i, j, k: (i % tiles_per_seq, 0))
        extras = [(tables[0], spec), (tables[1], spec)]
    return _matmul(a, w, tm=tm, tn=tn, tk=tk, epilogue=epilogue, extras=extras,
                   out_shape=[jax.ShapeDtypeStruct((m, n), out_dtype)],
                   out_specs=[pl.BlockSpec((tm, tn), lambda i, j, k: (i, j))], name=name)[0]


def _modulation(cvec, mod_w, mod_b, *, tn=512):
    depth, d, n = mod_w.shape
    rows = cvec.shape[0]
    tn = min(tn, n)

    def body(c_ref, w_ref, b_ref, o_ref):
        cv = c_ref[...]
        s = cv * (1.0 / (1.0 + jnp.exp(-cv)))
        o_ref[...] = jnp.dot(s, w_ref[...], preferred_element_type=F32) + b_ref[...]

    block_bytes = _nbytes((d, tn), F32) + _nbytes((rows, d), F32) + 2 * _nbytes((rows, tn), F32)
    return pl.pallas_call(
        body,
        grid=(depth, n // tn),
        in_specs=[pl.BlockSpec((rows, d), lambda l, j: (0, 0)),
                  pl.BlockSpec((None, d, tn), lambda l, j: (l, 0, j)),
                  pl.BlockSpec((None, 1, tn), lambda l, j: (l, 0, j))],
        out_specs=pl.BlockSpec((None, rows, tn), lambda l, j: (l, 0, j)),
        out_shape=jax.ShapeDtypeStruct((depth, rows, n), F32),
        compiler_params=_params(("parallel", "parallel"), block_bytes),
        name="adaln_modulation",
    )(cvec, mod_w, mod_b.reshape(depth, 1, n))


def _norm_mod(x, g, mods, sub, rows_per_group, out_dtype, *, tm=256):
    m, d = x.shape
    tm = min(tm, m, rows_per_group)
    tiles_per_group = rows_per_group // tm

    def body(*refs):
        if mods is None:
            x_ref, g_ref, o_ref = refs
        else:
            x_ref, g_ref, mod_ref, o_ref = refs
        y = _rms(x_ref[...], g_ref[...])
        if mods is not None:
            shift = mod_ref[3 * sub:3 * sub + 1, :]
            scale = mod_ref[3 * sub + 1:3 * sub + 2, :]
            y = y * (1.0 + scale) + shift
        o_ref[...] = y.astype(out_dtype)

    in_specs = [pl.BlockSpec((tm, d), lambda i: (i, 0)), pl.BlockSpec((1, d), lambda i: (0, 0))]
    args = [x, g.reshape(1, d)]
    if mods is not None:
        in_specs.append(pl.BlockSpec((None, 3 * N_SUB, d), lambda i: (i // tiles_per_group, 0, 0)))
        args.append(mods)
    block_bytes = _nbytes((tm, d), F32) + _nbytes((tm, d), out_dtype) + _nbytes((16, d), F32)
    return pl.pallas_call(
        body,
        grid=(m // tm,),
        in_specs=in_specs,
        out_specs=pl.BlockSpec((tm, d), lambda i: (i, 0)),
        out_shape=jax.ShapeDtypeStruct((m, d), out_dtype),
        compiler_params=_params(("parallel",), block_bytes, 2 * _nbytes((tm, d), F32)),
        name="rmsnorm_modulate",
    )(*args)


def _gate_up(h, w_gate, w_up, *, tm=1024, tn=512, tk=2048):
    m, kdim = h.shape
    n = w_gate.shape[1]
    tm, tn, tk = min(tm, m), min(tn, n), min(tk, kdim)
    assert m % tm == 0 and n % tn == 0 and kdim % tk == 0
    nk = kdim // tk

    def finish(g, u, o_ref):
        o_ref[...] = (g * (1.0 / (1.0 + jnp.exp(-g))) * u).astype(o_ref.dtype)

    def body(h_ref, wg_ref, wu_ref, o_ref, *acc):
        hh = h_ref[...]
        pg = jnp.dot(hh, wg_ref[...], preferred_element_type=F32)
        pu = jnp.dot(hh, wu_ref[...], preferred_element_type=F32)
        if nk == 1:
            finish(pg, pu, o_ref)
            return
        accg, accu = acc
        k = pl.program_id(2)

        @pl.when(k == 0)
        def _():
            accg[...] = pg
            accu[...] = pu

        if nk > 2:
            @pl.when((k > 0) & (k < nk - 1))
            def _():
                accg[...] += pg
                accu[...] += pu

        @pl.when(k == nk - 1)
        def _():
            finish(accg[...] + pg, accu[...] + pu, o_ref)

    block_bytes = _nbytes((tm, tk), BF16) + 2 * _nbytes((tk, tn), BF16) + _nbytes((tm, tn), BF16)
    acc_bytes = _nbytes((tm, tn), F32)
    return pl.pallas_call(
        body,
        grid=(m // tm, n // tn, nk),
        in_specs=[pl.BlockSpec((tm, tk), lambda i, j, k: (i, k)),
                  pl.BlockSpec((tk, tn), lambda i, j, k: (k, j)),
                  pl.BlockSpec((tk, tn), lambda i, j, k: (k, j))],
        out_specs=pl.BlockSpec((tm, tn), lambda i, j, k: (i, j)),
        out_shape=jax.ShapeDtypeStruct((m, n), BF16),
        scratch_shapes=[pltpu.VMEM((tm, tn), F32)] * 2 if nk > 1 else [],
        compiler_params=_params(("parallel", "parallel", "arbitrary"), block_bytes, 6 * acc_bytes),
        name="ffn_gate_up",
    )(h, w_gate, w_up)


def _half_ffn(x, mods, sub, rows_per_group, g, w_gate, w_up, w_down):
    h = _norm_mod(x, g, mods, sub, rows_per_group, BF16)
    u = _gate_up(h, w_gate, w_up)
    return _mm_residual(u, w_down, x, mods, sub, 0.5, rows_per_group, name="ffn_down_residual")


def _softmax_parts(q, k):
    s = lax.dot_general(q, k, (((1,), (1,)), ((), ())), preferred_element_type=F32)
    e = jnp.exp(s - jnp.max(s, axis=-1, keepdims=True))
    return e.astype(BF16), jnp.sum(e, axis=-1, keepdims=True)


def _mla_attention(q, k, v, heads, *, tq=256):
    b, sq, _ = q.shape
    sk = k.shape[1]
    tq = min(tq, sq)

    def body(q_ref, k_ref, v_ref, o_ref):
        e, l = _softmax_parts(q_ref[...], k_ref[...])
        o = jnp.dot(e, v_ref[...], preferred_element_type=F32)
        o_ref[...] = (o / l).astype(o_ref.dtype)

    block_bytes = (_nbytes((tq, MLA_HEAD_PAD), BF16) + _nbytes((sk, MLA_HEAD_PAD), BF16)
                   + _nbytes((sk, V_DIM), BF16) + _nbytes((tq, V_DIM), BF16))
    return pl.pallas_call(
        body,
        grid=(b, heads, sq // tq),
        in_specs=[pl.BlockSpec((None, tq, MLA_HEAD_PAD), lambda bb, h, i: (bb, i, h)),
                  pl.BlockSpec((None, sk, MLA_HEAD_PAD), lambda bb, h, i: (bb, 0, h)),
                  pl.BlockSpec((None, sk, V_DIM), lambda bb, h, i: (bb, 0, h))],
        out_specs=pl.BlockSpec((None, tq, V_DIM), lambda bb, h, i: (bb, i, h)),
        out_shape=jax.ShapeDtypeStruct((b, sq, heads * V_DIM), BF16),
        compiler_params=_params(("parallel", "parallel", "arbitrary"), block_bytes, 4 * _nbytes((tq, sk), F32)),
        name="mla_attention",
    )(q, k, v)


def _diff_attention(q, k, v, lam_p, subln_g, lambda_init, heads, *, tq=256):
    b, sq, _ = q.shape
    sk = k.shape[1]
    tq = min(tq, sq)
    hw = 2 * DIFF_DIM

    def body(lam_ref, g_ref, q_ref, k_ref, v_ref, o_ref):
        lp = lam_ref[...]
        lam = (jnp.exp(jnp.sum(lp[0:1] * lp[1:2], axis=-1, keepdims=True))
               - jnp.exp(jnp.sum(lp[2:3] * lp[3:4], axis=-1, keepdims=True)) + lambda_init)
        qq, kk, vv = q_ref[...], k_ref[...], v_ref[...]
        e1, l1 = _softmax_parts(qq[:, :DIFF_DIM], kk[:, :DIFF_DIM])
        e2, l2 = _softmax_parts(qq[:, DIFF_DIM:], kk[:, DIFF_DIM:])
        o1 = jnp.dot(e1, vv, preferred_element_type=F32)
        o2 = jnp.dot(e2, vv, preferred_element_type=F32)
        o = o1 / l1 - (lam / l2) * o2
        o_ref[...] = (_rms(o, g_ref[...]) * (1.0 - lambda_init)).astype(o_ref.dtype)

    block_bytes = 2 * _nbytes((tq, hw), BF16) + 2 * _nbytes((sk, hw), BF16)
    return pl.pallas_call(
        body,
        grid=(b, heads, sq // tq),
        in_specs=[pl.BlockSpec((4, DIFF_DIM), lambda bb, h, i: (0, 0)),
                  pl.BlockSpec((1, hw), lambda bb, h, i: (0, 0)),
                  pl.BlockSpec((None, tq, hw), lambda bb, h, i: (bb, i, h)),
                  pl.BlockSpec((None, sk, hw), lambda bb, h, i: (bb, 0, h)),
                  pl.BlockSpec((None, sk, hw), lambda bb, h, i: (bb, 0, h))],
        out_specs=pl.BlockSpec((None, tq, hw), lambda bb, h, i: (bb, i, h)),
        out_shape=jax.ShapeDtypeStruct((b, sq, heads * hw), BF16),
        compiler_params=_params(("parallel", "parallel", "arbitrary"), block_bytes, 6 * _nbytes((tq, sk), F32)),
        name="diff_attention",
    )(lam_p.astype(F32), subln_g.reshape(1, hw).astype(F32), q, k, v)


def _hyena_filters(seq, w1, b1, w2, b2, w3, b3, freq, w_out, *, tl=512, tn=1024):
    hid = w1.shape[1]
    width2 = w_out.shape[1]
    width = width2 // 2
    bands = (HY_EMB - 1) // 2
    t = jnp.linspace(0.0, 1.0, seq, dtype=F32)[:, None]
    ang = ((2.0 * math.pi / seq) * jnp.arange(seq, dtype=F32)[:, None]
           * jnp.linspace(1e-4, bands - 1, bands, dtype=F32)[None, :])
    z = jnp.concatenate([t, jnp.cos(ang), -jnp.sin(ang), jnp.zeros((seq, LANES - HY_EMB), F32)], axis=-1)
    deltas = jnp.abs(jnp.linspace(math.log(HY_TARGET) / HY_LONG_DECAY_PCT,
                                  math.log(HY_TARGET) / HY_SHORT_DECAY_PCT, width, dtype=F32))
    deltas2 = jnp.concatenate([deltas, deltas])[None, :]

    def pad2(w):
        return jnp.pad(w.astype(F32), ((0, LANES - w.shape[0]), (0, LANES - w.shape[1])))

    def pad_row(v):
        return jnp.pad(v.astype(F32), (0, LANES - v.shape[0]))[None, :]

    w_out_p = jnp.pad(w_out.astype(F32), ((0, LANES - hid), (0, 0)))
    tl, tn = min(tl, seq), min(tn, width2)
    exact = lax.Precision.HIGHEST

    def body(z_ref, w1_ref, b1_ref, w2_ref, b2_ref, w3_ref, b3_ref, f_ref, wo_ref, d_ref, o_ref):
        zz, f = z_ref[...], f_ref[...]
        a = jnp.sin(f * (jnp.dot(zz, w1_ref[...], precision=exact, preferred_element_type=F32) + b1_ref[...]))
        a = jnp.sin(f * (jnp.dot(a, w2_ref[...], precision=exact, preferred_element_type=F32) + b2_ref[...]))
        a = jnp.sin(f * (jnp.dot(a, w3_ref[...], precision=exact, preferred_element_type=F32) + b3_ref[...]))
        h = jnp.dot(a, wo_ref[...], precision=exact, preferred_element_type=F32)
        h = h * jnp.exp(-zz[:, 0:1] * d_ref[...])
        row = pl.program_id(0) * tl + lax.broadcasted_iota(jnp.int32, h.shape, 0)
        col = pl.program_id(1) * tn + lax.broadcasted_iota(jnp.int32, h.shape, 1)
        o_ref[...] = jnp.where((row == 0) & (col >= width), 0.0, h).astype(o_ref.dtype)

    sq = pl.BlockSpec((LANES, LANES), lambda i, j: (0, 0))
    vec = pl.BlockSpec((1, LANES), lambda i, j: (0, 0))
    block_bytes = _nbytes((tl, LANES), F32) + _nbytes((LANES, tn), F32) + 2 * _nbytes((tl, tn), F32)
    return pl.pallas_call(
        body,
        grid=(seq // tl, width2 // tn),
        in_specs=[pl.BlockSpec((tl, LANES), lambda i, j: (i, 0)), sq, vec, sq, vec, sq, vec, vec,
                  pl.BlockSpec((LANES, tn), lambda i, j: (0, j)),
                  pl.BlockSpec((1, tn), lambda i, j: (0, j))],
        out_specs=pl.BlockSpec((tl, tn), lambda i, j: (i, j)),
        out_shape=jax.ShapeDtypeStruct((seq, width2), BF16),
        compiler_params=_params(("parallel", "parallel"), block_bytes, 4 * _nbytes((tl, tn), F32)),
        name="hyena_filters",
    )(z, pad2(w1), pad_row(b1), pad2(w2), pad_row(b2), pad2(w3), pad_row(b3), pad_row(freq), w_out_p, deltas2)


def _hyena_prep(zh, conv_w, conv_b, *, tc=128):
    b, seq, w3 = zh.shape
    width = w3 // 3
    tc = min(tc, width)
    nw = width // tc

    def body(z0_ref, z1_ref, z2_ref, w0_ref, w1_ref, w2_ref, b0_ref, b1_ref, b2_ref, x0_ref, vx_ref):
        row = lax.broadcasted_iota(jnp.int32, (seq, 1), 0)

        def conv(z_ref, w_ref, b_ref):
            z = z_ref[...]
            prev = jnp.where(row == 0, 0.0, pltpu.roll(z, 1, 0))
            nxt = jnp.where(row == seq - 1, 0.0, pltpu.roll(z, seq - 1, 0))
            w = w_ref[...]
            return prev * w[0:1] + z * w[1:2] + nxt * w[2:3] + b_ref[...]

        x0 = conv(z0_ref, w0_ref, b0_ref)
        x1 = conv(z1_ref, w1_ref, b1_ref)
        v = conv(z2_ref, w2_ref, b2_ref)
        x0_ref[...] = x0
        vx_ref[...] = (v * x1).astype(vx_ref.dtype)

    def zspec(part):
        return pl.BlockSpec((None, seq, tc), lambda bb, j: (bb, 0, part * nw + j))

    def wspec(rows, part):
        return pl.BlockSpec((rows, tc), lambda bb, j: (0, part * nw + j))

    out_spec = pl.BlockSpec((seq, tc), lambda bb, j: (0, bb * nw + j))
    block_bytes = 4 * _nbytes((seq, tc), F32) + _nbytes((seq, tc), BF16)
    return pl.pallas_call(
        body,
        grid=(b, nw),
        in_specs=[zspec(0), zspec(1), zspec(2), wspec(3, 0), wspec(3, 1), wspec(3, 2),
                  wspec(1, 0), wspec(1, 1), wspec(1, 2)],
        out_specs=[out_spec, out_spec],
        out_shape=[jax.ShapeDtypeStruct((seq, b * width), F32), jax.ShapeDtypeStruct((seq, b * width), BF16)],
        compiler_params=_params(("parallel", "parallel"), block_bytes, 8 * _nbytes((seq, tc), F32)),
        name="hyena_short_conv",
    )(zh, zh, zh, conv_w, conv_w, conv_w, conv_b[None, :], conv_b[None, :], conv_b[None, :])


def _dft_matrices(seq):
    n = 2 * seq
    r = jnp.arange(n, dtype=jnp.int32)[:, None]
    s = jnp.arange(seq, dtype=jnp.int32)[None, :]
    k = jnp.where(r <= seq, r, r - seq)
    ang = ((k * s) % n).astype(F32) * (2.0 * math.pi / n)
    fwd = jnp.where(r <= seq, jnp.cos(ang), jnp.sin(ang))
    weight = jnp.where((r == 0) | (r == seq), 1.0 / n, 2.0 / n)
    inv = (fwd * weight).T
    return fwd.astype(BF16), inv.astype(BF16)


def _spectrum_multiply(spec, batch, width, *, tr=512, tc=512):
    n = spec.shape[0]
    half = n // 2
    tr, tc = min(tr, half), min(tc, width)
    nwc = width // tc
    s3 = spec.reshape(2, half, spec.shape[1])

    def body(u_ref, f_ref, b_ref, y_ref):
        ua, ub = u_ref[0], u_ref[1]
        fa, fb = f_ref[0], f_ref[1]
        ba, bb_ = b_ref[0], b_ref[1]
        ga = fa + ba
        gb = fb - bb_
        row = pl.program_id(0) * tr + lax.broadcasted_iota(jnp.int32, ua.shape, 0)
        dc = row == 0
        ya = ua * ga - jnp.where(dc, 0.0, ub * gb)
        yb = jnp.where(dc, ub * (fb + bb_), ua * gb + ub * ga)
        y_ref[0] = ya.astype(y_ref.dtype)
        y_ref[1] = yb.astype(y_ref.dtype)

    block_bytes = 3 * _nbytes((2, tr, tc), F32) + _nbytes((2, tr, tc), BF16)
    y = pl.pallas_call(
        body,
        grid=(half // tr, nwc, batch),
        in_specs=[pl.BlockSpec((2, tr, tc), lambda i, j, bb: (0, i, bb * nwc + j)),
                  pl.BlockSpec((2, tr, tc), lambda i, j, bb: (0, i, batch * nwc + j)),
                  pl.BlockSpec((2, tr, tc), lambda i, j, bb: (0, i, (batch + 1) * nwc + j))],
        out_specs=pl.BlockSpec((2, tr, tc), lambda i, j, bb: (0, i, bb * nwc + j)),
        out_shape=jax.ShapeDtypeStruct((2, half, batch * width), BF16),
        compiler_params=_params(("parallel", "parallel", "arbitrary"), block_bytes, 8 * _nbytes((tr, tc), F32)),
        name="hyena_spectrum_multiply",
    )(s3, s3, s3)
    return y.reshape(n, batch * width)


def _hyena_branch(zh, conv_w, conv_b, filt, hy_bias):
    b, seq, w3 = zh.shape
    width = w3 // 3
    x0, vx = _hyena_prep(zh, conv_w.astype(F32), conv_b.astype(F32))
    hfilt = _hyena_filters(seq, *filt)
    fwd, inv = _dft_matrices(seq)
    spec = _mm_plain(fwd, jnp.concatenate([vx, hfilt], axis=1), F32, name="hyena_dft_forward")
    y_hat = _spectrum_multiply(spec, b, width)

    tm, tn = min(1024, seq), min(1024, width)
    nw = width // tn

    def epilogue(acc, ex, outs):
        x0_ref, vx_ref, bias_ref = ex
        outs[0][...] = (x0_ref[...] * (acc + bias_ref[...] * vx_ref[...].astype(F32))).astype(BF16)

    extras = [(x0, pl.BlockSpec((tm, tn), lambda i, j, k: (i, j))),
              (vx, pl.BlockSpec((tm, tn), lambda i, j, k: (i, j))),
              (hy_bias.astype(F32)[None, :], pl.BlockSpec((1, tn), lambda i, j, k: (0, j % nw)))]
    return _matmul(inv, y_hat, tm=tm, tn=tn, tk=2048, epilogue=epilogue, extras=extras,
                   out_shape=[jax.ShapeDtypeStruct((b, seq, width), BF16)],
                   out_specs=[pl.BlockSpec((None, tm, tn), lambda i, j, k: (j // nw, i, j % nw))],
                   name="hyena_dft_inverse")[0]


def _ab_latents(h, w1, q_norm_g, kv_norm_g, tables, seq, q_lora, kv_lora, *, tm=512, tk=1024):
    m = h.shape[0]
    n1 = w1.shape[1]
    tm = min(tm, m, seq)
    tiles_per_seq = seq // tm

    def epilogue(acc, ex, outs):
        outs[0][...] = _rms(acc[:, :q_lora], ex[0][...]).astype(BF16)
        outs[1][...] = _rms(acc[:, q_lora:q_lora + kv_lora], ex[1][...]).astype(BF16)
        kr = acc[:, q_lora + kv_lora:]
        if tables is not None:
            kr = _rope_apply(kr, ex[2], ex[3], MLA_HEAD_PAD, NOPE_DIM, ROPE_DIM // 2)
        outs[2][...] = kr

    extras = [(q_norm_g.astype(F32)[None, :], pl.BlockSpec((1, q_lora), lambda i, j, k: (0, 0))),
              (kv_norm_g.astype(F32)[None, :], pl.BlockSpec((1, kv_lora), lambda i, j, k: (0, 0)))]
    if tables is not None:
        spec = pl.BlockSpec((tm, MLA_HEAD_PAD), lambda i, j, k: (i % tiles_per_seq, 0))
        extras += [(tables[0], spec), (tables[1], spec)]
    widths = (q_lora, kv_lora, MLA_HEAD_PAD)
    dtypes = (BF16, BF16, F32)
    return _matmul(h, w1, tm=tm, tn=n1, tk=tk, epilogue=epilogue, extras=extras,
                   out_shape=[jax.ShapeDtypeStruct((m, wd), dt) for wd, dt in zip(widths, dtypes)],
                   out_specs=[pl.BlockSpec((tm, wd), lambda i, j, k: (i, 0)) for wd in widths],
                   name="mla_latent_projection")


def _mla_keys(ckv, w_uk, kr, heads, *, tm=1024, tn=1024):
    m = ckv.shape[0]
    n = heads * MLA_HEAD_PAD
    tm, tn = min(tm, m), min(tn, n)
    reps = tn // MLA_HEAD_PAD

    def epilogue(acc, ex, outs):
        rot = ex[0][...]
        if reps > 1:
            rot = jnp.concatenate([rot] * reps, axis=1)
        outs[0][...] = (acc + rot).astype(BF16)

    extras = [(kr, pl.BlockSpec((tm, MLA_HEAD_PAD), lambda i, j, k: (i, 0)))]
    return _matmul(ckv, w_uk, tm=tm, tn=tn, tk=ckv.shape[1], epilogue=epilogue, extras=extras,
                   out_shape=[jax.ShapeDtypeStruct((m, n), BF16)],
                   out_specs=[pl.BlockSpec((tm, tn), lambda i, j, k: (i, j))], name="mla_key_up")[0]


def _ab_mixer(h, hc, batch, seq, seq_c, need_ctx, w_in, q_norm_g, kv_norm_g, w_uq, w_ukv,
              conv_w, conv_b, filt, hy_bias):
    d = w_in.shape[0]
    heads = d // (2 * V_DIM)
    q_lora, kv_lora = w_uq.shape[0], w_ukv.shape[0]
    kv_end = q_lora + kv_lora + ROPE_DIM
    mla_scale = (NOPE_DIM + ROPE_DIM) ** -0.5
    zero_pad = MLA_HEAD_PAD - NOPE_DIM - ROPE_DIM

    w_rope = w_in[:, q_lora + kv_lora:kv_end]
    w1 = jnp.concatenate([w_in[:, :q_lora + kv_lora], jnp.zeros((d, NOPE_DIM), w_in.dtype),
                          w_rope[:, 0::2], w_rope[:, 1::2], jnp.zeros((d, zero_pad), w_in.dtype)],
                         axis=1).astype(BF16)
    w_hy = w_in[:, kv_end:].astype(BF16)
    uq = w_uq.reshape(q_lora, heads, NOPE_DIM + ROPE_DIM)
    w_uq_p = jnp.concatenate([uq[..., :NOPE_DIM], uq[..., NOPE_DIM::2], uq[..., NOPE_DIM + 1::2],
                              jnp.zeros((q_lora, heads, zero_pad), w_uq.dtype)],
                             axis=-1).reshape(q_lora, heads * MLA_HEAD_PAD).astype(BF16)
    ukv = w_ukv.reshape(kv_lora, heads, NOPE_DIM + V_DIM)
    w_uk_p = jnp.concatenate([ukv[..., :NOPE_DIM], jnp.zeros((kv_lora, heads, MLA_HEAD_PAD - NOPE_DIM), w_ukv.dtype)],
                             axis=-1).reshape(kv_lora, heads * MLA_HEAD_PAD).astype(BF16)
    w_uv = ukv[..., NOPE_DIM:].reshape(kv_lora, heads * V_DIM).astype(BF16)

    tables = _rope_tables(seq, ROPE_DIM // 4, MLA_HEAD_PAD, NOPE_DIM)
    rope_kw = dict(group=MLA_HEAD_PAD, lo=NOPE_DIM, half=ROPE_DIM // 2)

    cq, ckv, kr = _ab_latents(h, w1, q_norm_g, kv_norm_g, tables, seq, q_lora, kv_lora)
    cqc, ckvc, krc = _ab_latents(hc, w1, q_norm_g, kv_norm_g, None, seq_c, q_lora, kv_lora)
    q = _mm_rope(cq, w_uq_p, tables, mla_scale, seq, BF16, name="mla_query_up", **rope_kw)
    k = _mla_keys(ckv, w_uk_p, kr, heads)
    v = _mm_plain(ckv, w_uv, BF16, name="mla_value_up")
    kc = _mla_keys(ckvc, w_uk_p, krc, heads)
    vc = _mm_plain(ckvc, w_uv, BF16, name="mla_value_up")

    def split(t, s):
        return t.reshape(batch, s, t.shape[-1])

    att = _mla_attention(split(q, seq), jnp.concatenate([split(kc, seq_c), split(k, seq)], axis=1),
                         jnp.concatenate([split(vc, seq_c), split(v, seq)], axis=1), heads)
    zh = _mm_plain(h, w_hy, F32, tn=512, name="hyena_in_projection")
    hy = _hyena_branch(split(zh, seq), conv_w, conv_b, filt, hy_bias)
    mix = jnp.concatenate([att, hy], axis=-1).reshape(batch * seq, -1)
    if not need_ctx:
        return mix, None
    qc = _mm_rope(cqc, w_uq_p, None, mla_scale, seq_c, BF16, name="mla_query_up", **rope_kw)
    attc = _mla_attention(split(qc, seq_c), split(kc, seq_c), split(vc, seq_c), heads)
    zhc = _mm_plain(hc, w_hy, F32, tn=512, name="hyena_in_projection")
    hyc = _hyena_branch(split(zhc, seq_c), conv_w, conv_b, filt, hy_bias)
    return mix, jnp.concatenate([attc, hyc], axis=-1).reshape(batch * seq_c, -1)


def _diff_mixer(h, hc, batch, seq, seq_c, need_ctx, w_in, lam_p, subln_g, lambda_init):
    d = w_in.shape[0]
    hd = w_in.shape[1] // 3
    heads = hd // (2 * DIFF_DIM)
    scale = DIFF_DIM ** -0.5
    w_q = _deinterleave(w_in[:, :hd], DIFF_DIM, 0, DIFF_DIM).astype(BF16)
    w_k = _deinterleave(w_in[:, hd:2 * hd], DIFF_DIM, 0, DIFF_DIM).astype(BF16)
    w_v = w_in[:, 2 * hd:].astype(BF16)
    tables = _rope_tables(seq, DIFF_DIM // 4, DIFF_DIM, 0)
    rope_kw = dict(group=DIFF_DIM, lo=0, half=DIFF_DIM // 2)

    q = _mm_rope(h, w_q, tables, scale, seq, BF16, name="diff_query_projection", **rope_kw)
    k = _mm_rope(h, w_k, tables, 1.0, seq, BF16, name="diff_key_projection", **rope_kw)
    v = _mm_plain(h, w_v, BF16, name="diff_value_projection")
    kc = _mm_rope(hc, w_k, None, 1.0, seq_c, BF16, name="diff_key_projection", **rope_kw)
    vc = _mm_plain(hc, w_v, BF16, name="diff_value_projection")

    def split(t, s):
        return t.reshape(batch, s, t.shape[-1])

    o = _diff_attention(split(q, seq), jnp.concatenate([split(kc, seq_c), split(k, seq)], axis=1),
                        jnp.concatenate([split(vc, seq_c), split(v, seq)], axis=1),
                        lam_p, subln_g, lambda_init, heads)
    mix = o.reshape(batch * seq, hd)
    if not need_ctx:
        return mix, None
    qc = _mm_rope(hc, w_q, None, scale, seq_c, BF16, name="diff_query_projection", **rope_kw)
    oc = _diff_attention(split(qc, seq_c), split(kc, seq_c), split(vc, seq_c), lam_p, subln_g, lambda_init, heads)
    return mix, oc.reshape(batch * seq_c, hd)


def kernel(x, c, ctx, c_ctx, mod_w, mod_b, norm_g, ffn_w_gate, ffn_w_up, ffn_w_down, ab_w_in, mla_q_norm_g, mla_kv_norm_g, mla_w_uq, mla_w_ukv, hy_conv_w, hy_conv_b, hy_w1, hy_b1, hy_w2, hy_b2, hy_w3, hy_b3, hy_freq, hy_w_out, hy_bias, ab_w_out, c_w_in, c_lambda, c_subln_g, c_w_out, final_norm_g):
    batch, seq, d = x.shape
    seq_c = ctx.shape[1]
    depth = mod_w.shape[0]
    rows_c = batch * seq_c
    xs = x.reshape(batch * seq, d).astype(F32)
    xc = ctx.reshape(rows_c, d).astype(F32)

    cvec = jnp.concatenate([c.astype(F32), c_ctx.astype(F32)[None, :]], axis=0)
    cvec = jnp.pad(cvec, ((0, -cvec.shape[0] % 8), (0, 0)))
    mods_all = _modulation(cvec, mod_w.astype(F32), mod_b.astype(F32))

    for layer in range(depth):
        need_ctx = layer < depth - 1
        mods = mods_all[layer, :batch].reshape(batch, 3 * N_SUB, d)
        modc = mods_all[layer, batch:batch + 1].reshape(1, 3 * N_SUB, d)
        g = norm_g[layer].astype(F32)

        def ffn(t, m, rows, sub, which):
            return _half_ffn(t, m, sub, rows, g[sub], ffn_w_gate[layer, which].astype(BF16),
                             ffn_w_up[layer, which].astype(BF16), ffn_w_down[layer, which].astype(BF16))

        xs = ffn(xs, mods, seq, 0, 0)
        xc = ffn(xc, modc, rows_c, 0, 0)
        h = _norm_mod(xs, g[1], mods, 1, seq, BF16)
        hc = _norm_mod(xc, g[1], modc, 1, rows_c, BF16)
        i = layer // 2
        if layer % 2 == 0:
            filt = (hy_w1[i], hy_b1[i], hy_w2[i], hy_b2[i], hy_w3[i], hy_b3[i], hy_freq[i], hy_w_out[i])
            mix, mixc = _ab_mixer(h, hc, batch, seq, seq_c, need_ctx, ab_w_in[i], mla_q_norm_g[i],
                                  mla_kv_norm_g[i], mla_w_uq[i], mla_w_ukv[i], hy_conv_w[i], hy_conv_b[i],
                                  filt, hy_bias[i])
            w_out = ab_w_out[i].astype(BF16)
        else:
            lambda_init = 0.8 - 0.6 * math.exp(-0.3 * layer)
            mix, mixc = _diff_mixer(h, hc, batch, seq, seq_c, need_ctx, c_w_in[i], c_lambda[i], c_subln_g[i],
                                    lambda_init)
            w_out = c_w_out[i].astype(BF16)
        xs = _mm_residual(mix, w_out, xs, mods, 1, 1.0, seq, name="mixer_out_residual")
        xs = ffn(xs, mods, seq, 2, 1)
        if need_ctx:
            xc = _mm_residual(mixc, w_out, xc, modc, 1, 1.0, rows_c, name="mixer_out_residual")
            xc = ffn(xc, modc, rows_c, 2, 1)

    out = _norm_mod(xs, final_norm_g.astype(F32), None, 0, seq, x.dtype)
    return out.reshape(batch, seq, d)
```

```python
import functools
import math
from typing import NamedTuple

import numpy as np
import jax
import jax.numpy as jnp
from jax import lax
from jax.experimental import pallas as pl
from jax.experimental.pallas import tpu as pltpu

F32 = jnp.float32
BF16 = jnp.bfloat16

EPS = 1e-6
ROPE_THETA = 10000.0
GRID_W = 64
N_SUB = 3
NOPE_DIM = 128
ROPE_DIM = 64
V_DIM = 128
DIFF_DIM = 128
HY_EMB = 33
HY_TARGET = 1e-2
HY_SHORT_DECAY_PCT = 0.3
HY_LONG_DECAY_PCT = 1.5

LANES = 128
MXU_DIM_V7X = 256
VMEM_BYTES_V7X = 64 * 1024 * 1024
VMEM_COMPILER_RESERVE = 8 * 1024 * 1024
MLA_HEAD_PAD = MXU_DIM_V7X

TILE_M = 1024
TILE_N = 512
TILE_K = 4096


def _params(semantics, block_bytes, temp_bytes=0):
    want = 2 * block_bytes + temp_bytes
    limit = min(VMEM_BYTES_V7X - VMEM_COMPILER_RESERVE, max(want, 32 * 1024 * 1024))
    return pltpu.CompilerParams(dimension_semantics=semantics, vmem_limit_bytes=int(limit))


def _nbytes(shape, dtype):
    return math.prod(shape) * jnp.dtype(dtype).itemsize


def _rms(x, g):
    return x * lax.rsqrt(jnp.mean(x * x, axis=-1, keepdims=True) + EPS) * g


def _matmul(a, w, *, tm, tn, tk, epilogue, out_shape, out_specs, extras=(), w_index=(), name):
    m, kdim = a.shape
    assert w.ndim == 2 + len(w_index) and w.shape[-2] == kdim
    n = w.shape[-1]
    tm, tn, tk = min(tm, m), min(tn, n), min(tk, kdim)
    assert m % tm == 0 and n % tn == 0 and kdim % tk == 0, (name, a.shape, w.shape, tm, tn, tk)
    nk = kdim // tk
    n_ex, n_out = len(extras), len(out_shape)

    def body(*refs):
        a_ref, w_ref = refs[0], refs[1]
        ex = refs[2:2 + n_ex]
        outs = refs[2 + n_ex:2 + n_ex + n_out]
        prod = jnp.dot(a_ref[...], w_ref[...], preferred_element_type=F32)
        if nk == 1:
            epilogue(prod, ex, outs)
            return
        acc_ref = refs[-1]
        k = pl.program_id(2)

        @pl.when(k == 0)
        def _():
            acc_ref[...] = prod

        if nk > 2:
            @pl.when((k > 0) & (k < nk - 1))
            def _():
                acc_ref[...] += prod

        @pl.when(k == nk - 1)
        def _():
            epilogue(acc_ref[...] + prod, ex, outs)

    in_specs = [pl.BlockSpec((tm, tk), lambda i, j, k: (i, k)),
                pl.BlockSpec((None,) * len(w_index) + (tk, tn), lambda i, j, k: (*w_index, k, j))]
    in_specs += [spec for _, spec in extras]
    block_bytes = _nbytes((tm, tk), a.dtype) + _nbytes((tk, tn), w.dtype)
    for arr, spec in extras:
        block_bytes += _nbytes([d for d in spec.block_shape if d is not None], arr.dtype)
    for sds, spec in zip(out_shape, out_specs):
        block_bytes += _nbytes([d for d in spec.block_shape if d is not None], sds.dtype)
    acc_bytes = _nbytes((tm, tn), F32)
    return pl.pallas_call(
        body,
        grid=(m // tm, n // tn, nk),
        in_specs=in_specs,
        out_specs=list(out_specs),
        out_shape=list(out_shape),
        scratch_shapes=[pltpu.VMEM((tm, tn), F32)] if nk > 1 else [],
        compiler_params=_params(("parallel", "parallel", "arbitrary"), block_bytes, 8 * acc_bytes),
        name=name,
    )(a, w, *[arr for arr, _ in extras])


def _mm_plain(a, w, out_dtype, *, tm=TILE_M, tn=TILE_N, tk=TILE_K, name):
    m, n = a.shape[0], w.shape[1]
    tm, tn = min(tm, m), min(tn, n)

    def epilogue(acc, ex, outs):
        outs[0][...] = acc.astype(out_dtype)

    return _matmul(a, w, tm=tm, tn=tn, tk=tk, epilogue=epilogue,
                   out_shape=[jax.ShapeDtypeStruct((m, n), out_dtype)],
                   out_specs=[pl.BlockSpec((tm, tn), lambda i, j, k: (i, j))], name=name)[0]


def _mm_residual(a, w, res, mods, sub, coef, rows_per_group, *, tm=TILE_M, tn=TILE_N, tk=TILE_K, w_index=(), name):
    m, n = a.shape[0], w.shape[-1]
    tm, tn = min(tm, m, rows_per_group), min(tn, n)
    tiles_per_group = rows_per_group // tm
    gate_row = 3 * sub + 2

    def epilogue(acc, ex, outs):
        res_ref, mod_ref = ex
        gate = mod_ref[gate_row:gate_row + 1, :]
        outs[0][...] = res_ref[...] + (coef * gate) * acc

    extras = [(res, pl.BlockSpec((tm, tn), lambda i, j, k: (i, j))),
              (mods, pl.BlockSpec((None, 3 * N_SUB, tn), lambda i, j, k: (i // tiles_per_group, 0, j)))]
    return _matmul(a, w, tm=tm, tn=tn, tk=tk, epilogue=epilogue, extras=extras, w_index=w_index,
                   out_shape=[jax.ShapeDtypeStruct((m, n), F32)],
                   out_specs=[pl.BlockSpec((tm, tn), lambda i, j, k: (i, j))], name=name)[0]


class RopeLayout(NamedTuple):
    group: int
    lo: int
    n_freq: int
    split: bool


MLA_ROPE_LAYOUT = RopeLayout(group=MLA_HEAD_PAD, lo=NOPE_DIM, n_freq=ROPE_DIM // 4, split=False)
DIFF_ROPE_LAYOUT = RopeLayout(group=DIFF_DIM, lo=0, n_freq=DIFF_DIM // 4, split=True)


def _rope_tables(seq, layout):
    n_freq = layout.n_freq
    pos = jnp.arange(seq, dtype=jnp.int32)
    row_pos = (pos // GRID_W).astype(F32)
    col_pos = (pos % GRID_W).astype(F32)
    inv = ROPE_THETA ** (-jnp.arange(n_freq, dtype=F32) / n_freq)
    ang = jnp.concatenate([row_pos[:, None] * inv, col_pos[:, None] * inv], axis=-1)
    cos, sin = jnp.cos(ang), jnp.sin(ang)
    if layout.split:
        assert 4 * n_freq == LANES
        return jnp.concatenate([cos, cos], axis=-1), jnp.concatenate([-sin, sin], axis=-1)
    cos = jnp.repeat(cos, 2, axis=-1)
    sin = jnp.repeat(sin, 2, axis=-1) * jnp.tile(jnp.array([-1.0, 1.0], F32), 2 * n_freq)
    rest = LANES - 4 * n_freq
    return (jnp.concatenate([cos, jnp.ones((seq, rest), F32)], axis=-1),
            jnp.concatenate([sin, jnp.zeros((seq, rest), F32)], axis=-1))


def _rope_apply(x, cos_ref, sin_ref, layout):
    assert layout.lo % LANES == 0 and layout.group % LANES == 0 and x.shape[-1] % layout.group == 0
    cos, sin = cos_ref[...], sin_ref[...]
    slabs = []
    for c0 in range(0, x.shape[-1], LANES):
        slab = x[:, c0:c0 + LANES]
        if c0 % layout.group == layout.lo:
            if layout.split:
                partner = pltpu.roll(slab, LANES // 2, 1)
            else:
                even = (lax.broadcasted_iota(jnp.int32, slab.shape, 1) & 1) == 0
                partner = jnp.where(even, pltpu.roll(slab, LANES - 1, 1), pltpu.roll(slab, 1, 1))
            slab = slab * cos + partner * sin
        slabs.append(slab)
    return jnp.concatenate(slabs, axis=1) if len(slabs) > 1 else slabs[0]


def _deinterleave_columns(w):
    kdim, n = w.shape
    perm = np.concatenate([np.arange(0, LANES, 2), np.arange(1, LANES, 2)])
    p = np.zeros((LANES, LANES), np.float32)
    p[perm, np.arange(LANES)] = 1.0
    out = _mm_plain(w.reshape(kdim * n // LANES, LANES), jnp.asarray(p, BF16), BF16, tm=8192, tn=LANES, tk=LANES,
                    name="deinterleave_rotary_columns")
    return out.reshape(kdim, n)


def _mm_rope(a, w, tables, scale, seq, out_dtype, *, layout, tm=TILE_M, tn=TILE_N, tk=TILE_K, name):
    m, n = a.shape[0], w.shape[1]
    tm, tn = min(tm, m, seq), min(tn, n)
    tiles_per_seq = seq // tm

    def epilogue(acc, ex, outs):
        y = acc if tables is None else _rope_apply(acc, ex[0], ex[1], layout)
        outs[0][...] = (y * scale).astype(out_dtype)

    extras = []
    if tables is not None:
        spec = pl.BlockSpec((tm, LANES), lambda i, j, k: (i % tiles_per_seq, 0))
        extras = [(tables[0], spec), (tables[1], spec)]
    return _matmul(a, w, tm=tm, tn=tn, tk=tk, epilogue=epilogue, extras=extras,
                   out_shape=[jax.ShapeDtypeStruct((m, n), out_dtype)],
                   out_specs=[pl.BlockSpec((tm, tn), lambda i, j, k: (i, j))], name=name)[0]


def _modulation(cvec, mod_w, mod_b, *, tn=512):
    depth, d, n = mod_w.shape
    rows = cvec.shape[0]
    tn = min(tn, n)

    def body(c_ref, w_ref, b_ref, o_ref):
        cv = c_ref[...]
        s = cv * (1.0 / (1.0 + jnp.exp(-cv)))
        o_ref[...] = jnp.dot(s, w_ref[...], preferred_element_type=F32) + b_ref[...]

    block_bytes = _nbytes((d, tn), F32) + _nbytes((rows, d), F32) + 2 * _nbytes((rows, tn), F32)
    return pl.pallas_call(
        body,
        grid=(depth, n // tn),
        in_specs=[pl.BlockSpec((rows, d), lambda l, j: (0, 0)),
                  pl.BlockSpec((None, d, tn), lambda l, j: (l, 0, j)),
                  pl.BlockSpec((None, 1, tn), lambda l, j: (l, 0, j))],
        out_specs=pl.BlockSpec((None, rows, tn), lambda l, j: (l, 0, j)),
        out_shape=jax.ShapeDtypeStruct((depth, rows, n), F32),
        compiler_params=_params(("parallel", "parallel"), block_bytes),
        name="adaln_modulation",
    )(cvec, mod_w, mod_b.reshape(depth, 1, n))


def _norm_mod(x, g, mods, sub, rows_per_group, out_dtype, *, tm=256):
    m, d = x.shape
    tm = min(tm, m, rows_per_group)
    tiles_per_group = rows_per_group // tm

    def body(*refs):
        if mods is None:
            x_ref, g_ref, o_ref = refs
        else:
            x_ref, g_ref, mod_ref, o_ref = refs
        y = _rms(x_ref[...], g_ref[...])
        if mods is not None:
            shift = mod_ref[3 * sub:3 * sub + 1, :]
            scale = mod_ref[3 * sub + 1:3 * sub + 2, :]
            y = y * (1.0 + scale) + shift
        o_ref[...] = y.astype(out_dtype)

    in_specs = [pl.BlockSpec((tm, d), lambda i: (i, 0)), pl.BlockSpec((1, d), lambda i: (0, 0))]
    args = [x, g.reshape(1, d)]
    if mods is not None:
        in_specs.append(pl.BlockSpec((None, 3 * N_SUB, d), lambda i: (i // tiles_per_group, 0, 0)))
        args.append(mods)
    block_bytes = _nbytes((tm, d), F32) + _nbytes((tm, d), out_dtype) + _nbytes((16, d), F32)
    return pl.pallas_call(
        body,
        grid=(m // tm,),
        in_specs=in_specs,
        out_specs=pl.BlockSpec((tm, d), lambda i: (i, 0)),
        out_shape=jax.ShapeDtypeStruct((m, d), out_dtype),
        compiler_params=_params(("parallel",), block_bytes, 2 * _nbytes((tm, d), F32)),
        name="rmsnorm_modulate",
    )(*args)


def _gate_up(h, w_gate, w_up, w_index, *, tm=TILE_M, tn=TILE_N, tk=TILE_K):
    m, kdim = h.shape
    n = w_gate.shape[-1]
    tm, tn, tk = min(tm, m), min(tn, n), min(tk, kdim)
    assert m % tm == 0 and n % tn == 0 and kdim % tk == 0
    nk = kdim // tk

    def finish(g, u, o_ref):
        o_ref[...] = (g * (1.0 / (1.0 + jnp.exp(-g))) * u).astype(o_ref.dtype)

    def body(h_ref, wg_ref, wu_ref, o_ref, *acc):
        hh = h_ref[...]
        pg = jnp.dot(hh, wg_ref[...], preferred_element_type=F32)
        pu = jnp.dot(hh, wu_ref[...], preferred_element_type=F32)
        if nk == 1:
            finish(pg, pu, o_ref)
            return
        accg, accu = acc
        k = pl.program_id(2)

        @pl.when(k == 0)
        def _():
            accg[...] = pg
            accu[...] = pu

        if nk > 2:
            @pl.when((k > 0) & (k < nk - 1))
            def _():
                accg[...] += pg
                accu[...] += pu

        @pl.when(k == nk - 1)
        def _():
            finish(accg[...] + pg, accu[...] + pu, o_ref)

    block_bytes = _nbytes((tm, tk), BF16) + 2 * _nbytes((tk, tn), BF16) + _nbytes((tm, tn), BF16)
    acc_bytes = _nbytes((tm, tn), F32)
    w_spec = pl.BlockSpec((None,) * len(w_index) + (tk, tn), lambda i, j, k: (*w_index, k, j))
    return pl.pallas_call(
        body,
        grid=(m // tm, n // tn, nk),
        in_specs=[pl.BlockSpec((tm, tk), lambda i, j, k: (i, k)), w_spec, w_spec],
        out_specs=pl.BlockSpec((tm, tn), lambda i, j, k: (i, j)),
        out_shape=jax.ShapeDtypeStruct((m, n), BF16),
        scratch_shapes=[pltpu.VMEM((tm, tn), F32)] * 2 if nk > 1 else [],
        compiler_params=_params(("parallel", "parallel", "arbitrary"), block_bytes, 6 * acc_bytes),
        name="ffn_gate_up",
    )(h, w_gate, w_up)


def _half_ffn(x, mods, sub, rows_per_group, g, w_gate, w_up, w_down, w_index):
    h = _norm_mod(x, g, mods, sub, rows_per_group, BF16)
    u = _gate_up(h, w_gate, w_up, w_index)
    return _mm_residual(u, w_down, x, mods, sub, 0.5, rows_per_group, w_index=w_index,
                        name="ffn_down_residual")


def _softmax_parts(q, ks):
    ss = [lax.dot_general(q, k, (((1,), (1,)), ((), ())), preferred_element_type=F32) for k in ks]
    m = functools.reduce(jnp.maximum, [jnp.max(s, axis=-1, keepdims=True) for s in ss])
    es = [jnp.exp(s - m) for s in ss]
    l = functools.reduce(jnp.add, [jnp.sum(e, axis=-1, keepdims=True) for e in es])
    return [e.astype(BF16) for e in es], l


def _weighted_values(es, vs):
    return functools.reduce(jnp.add, [jnp.dot(e, v, preferred_element_type=F32) for e, v in zip(es, vs)])


def _kv_specs(arrays, width):
    return [pl.BlockSpec((None, a.shape[1], width), lambda bb, h, i: (bb, 0, h)) for a in arrays]


def _mla_attention(q, ks, vs, heads, *, tq=512, ts=256):
    b, sq, _ = q.shape
    sk = sum(k.shape[1] for k in ks)
    tq = min(tq, sq)
    ts = min(ts, tq)
    nkv = len(ks)

    def body(q_ref, *refs):
        k_vals = [r[...] for r in refs[:nkv]]
        v_vals = [r[...] for r in refs[nkv:2 * nkv]]
        o_ref = refs[2 * nkv]
        for r0 in range(0, tq, ts):
            es, l = _softmax_parts(q_ref[r0:r0 + ts, :], k_vals)
            o_ref[r0:r0 + ts, :] = (_weighted_values(es, v_vals) / l).astype(o_ref.dtype)

    block_bytes = (_nbytes((tq, MLA_HEAD_PAD), BF16) + _nbytes((sk, MLA_HEAD_PAD), BF16)
                   + _nbytes((sk, V_DIM), BF16) + _nbytes((tq, V_DIM), BF16))
    return pl.pallas_call(
        body,
        grid=(b, heads, sq // tq),
        in_specs=([pl.BlockSpec((None, tq, MLA_HEAD_PAD), lambda bb, h, i: (bb, i, h))]
                  + _kv_specs(ks, MLA_HEAD_PAD) + _kv_specs(vs, V_DIM)),
        out_specs=pl.BlockSpec((None, tq, V_DIM), lambda bb, h, i: (bb, i, h)),
        out_shape=jax.ShapeDtypeStruct((b, sq, heads * V_DIM), BF16),
        compiler_params=_params(("parallel", "parallel", "arbitrary"), block_bytes, 4 * _nbytes((tq, sk), F32)),
        name="mla_attention",
    )(q, *ks, *vs)


def _diff_attention(q, ks, vs, lam_p, subln_g, lambda_init, heads, *, tq=256, ts=128):
    b, sq, _ = q.shape
    sk = sum(k.shape[1] for k in ks)
    tq = min(tq, sq)
    ts = min(ts, tq)
    hw = 2 * DIFF_DIM
    nkv = len(ks)

    def body(lam_ref, g_ref, q_ref, *refs):
        lp = lam_ref[...]
        lam = (jnp.exp(jnp.sum(lp[0:1] * lp[1:2], axis=-1, keepdims=True))
               - jnp.exp(jnp.sum(lp[2:3] * lp[3:4], axis=-1, keepdims=True)) + lambda_init)
        k_vals = [r[...] for r in refs[:nkv]]
        v_vals = [r[...] for r in refs[nkv:2 * nkv]]
        o_ref = refs[2 * nkv]
        k1 = [k[:, :DIFF_DIM] for k in k_vals]
        k2 = [k[:, DIFF_DIM:] for k in k_vals]
        for r0 in range(0, tq, ts):
            qq = q_ref[r0:r0 + ts, :]
            e1, l1 = _softmax_parts(qq[:, :DIFF_DIM], k1)
            e2, l2 = _softmax_parts(qq[:, DIFF_DIM:], k2)
            o = _weighted_values(e1, v_vals) / l1 - (lam / l2) * _weighted_values(e2, v_vals)
            o_ref[r0:r0 + ts, :] = (_rms(o, g_ref[...]) * (1.0 - lambda_init)).astype(o_ref.dtype)

    block_bytes = 2 * _nbytes((tq, hw), BF16) + 2 * _nbytes((sk, hw), BF16)
    return pl.pallas_call(
        body,
        grid=(b, heads, sq // tq),
        in_specs=([pl.BlockSpec((4, DIFF_DIM), lambda bb, h, i: (0, 0)),
                   pl.BlockSpec((1, hw), lambda bb, h, i: (0, 0)),
                   pl.BlockSpec((None, tq, hw), lambda bb, h, i: (bb, i, h))]
                  + _kv_specs(ks, hw) + _kv_specs(vs, hw)),
        out_specs=pl.BlockSpec((None, tq, hw), lambda bb, h, i: (bb, i, h)),
        out_shape=jax.ShapeDtypeStruct((b, sq, heads * hw), BF16),
        compiler_params=_params(("parallel", "parallel", "arbitrary"), block_bytes, 6 * _nbytes((tq, sk), F32)),
        name="diff_attention",
    )(lam_p.astype(F32), subln_g.reshape(1, hw).astype(F32), q, *ks, *vs)


def _hyena_filters(seq, w1, b1, w2, b2, w3, b3, freq, w_out, *, tl=512, tn=1024):
    hid = w1.shape[1]
    width2 = w_out.shape[1]
    width = width2 // 2
    bands = (HY_EMB - 1) // 2
    t = jnp.linspace(0.0, 1.0, seq, dtype=F32)[:, None]
    ang = ((2.0 * math.pi / seq) * jnp.arange(seq, dtype=F32)[:, None]
           * jnp.linspace(1e-4, bands - 1, bands, dtype=F32)[None, :])
    z = jnp.concatenate([t, jnp.cos(ang), -jnp.sin(ang), jnp.zeros((seq, LANES - HY_EMB), F32)], axis=-1)
    deltas = jnp.abs(jnp.linspace(math.log(HY_TARGET) / HY_LONG_DECAY_PCT,
                                  math.log(HY_TARGET) / HY_SHORT_DECAY_PCT, width, dtype=F32))
    deltas2 = jnp.concatenate([deltas, deltas])[None, :]

    def pad2(w):
        return jnp.pad(w.astype(F32), ((0, LANES - w.shape[0]), (0, LANES - w.shape[1])))

    def pad_row(v):
        return jnp.pad(v.astype(F32), (0, LANES - v.shape[0]))[None, :]

    w_out_p = jnp.pad(w_out.astype(F32), ((0, LANES - hid), (0, 0)))
    tl, tn = min(tl, seq), min(tn, width2)
    exact = lax.Precision.HIGHEST

    def body(z_ref, w1_ref, b1_ref, w2_ref, b2_ref, w3_ref, b3_ref, f_ref, wo_ref, d_ref, o_ref):
        zz, f = z_ref[...], f_ref[...]
        a = jnp.sin(f * (jnp.dot(zz, w1_ref[...], precision=exact, preferred_element_type=F32) + b1_ref[...]))
        a = jnp.sin(f * (jnp.dot(a, w2_ref[...], precision=exact, preferred_element_type=F32) + b2_ref[...]))
        a = jnp.sin(f * (jnp.dot(a, w3_ref[...], precision=exact, preferred_element_type=F32) + b3_ref[...]))
        h = jnp.dot(a, wo_ref[...], precision=exact, preferred_element_type=F32)
        h = h * jnp.exp(-zz[:, 0:1] * d_ref[...])
        row = pl.program_id(0) * tl + lax.broadcasted_iota(jnp.int32, h.shape, 0)
        col = pl.program_id(1) * tn + lax.broadcasted_iota(jnp.int32, h.shape, 1)
        o_ref[...] = jnp.where((row == 0) & (col >= width), 0.0, h).astype(o_ref.dtype)

    sq = pl.BlockSpec((LANES, LANES), lambda i, j: (0, 0))
    vec = pl.BlockSpec((1, LANES), lambda i, j: (0, 0))
    block_bytes = _nbytes((tl, LANES), F32) + _nbytes((LANES, tn), F32) + 2 * _nbytes((tl, tn), F32)
    return pl.pallas_call(
        body,
        grid=(seq // tl, width2 // tn),
        in_specs=[pl.BlockSpec((tl, LANES), lambda i, j: (i, 0)), sq, vec, sq, vec, sq, vec, vec,
                  pl.BlockSpec((LANES, tn), lambda i, j: (0, j)),
                  pl.BlockSpec((1, tn), lambda i, j: (0, j))],
        out_specs=pl.BlockSpec((tl, tn), lambda i, j: (i, j)),
        out_shape=jax.ShapeDtypeStruct((seq, width2), BF16),
        compiler_params=_params(("parallel", "parallel"), block_bytes, 4 * _nbytes((tl, tn), F32)),
        name="hyena_filters",
    )(z, pad2(w1), pad_row(b1), pad2(w2), pad_row(b2), pad2(w3), pad_row(b3), pad_row(freq), w_out_p, deltas2)


def _hyena_prep(zh, conv_w, conv_b, *, tc=128):
    b, seq, w3 = zh.shape
    width = w3 // 3
    tc = min(tc, width)
    nw = width // tc

    def body(z0_ref, z1_ref, z2_ref, w0_ref, w1_ref, w2_ref, b0_ref, b1_ref, b2_ref, x0_ref, vx_ref):
        row = lax.broadcasted_iota(jnp.int32, (seq, 1), 0)

        def conv(z_ref, w_ref, b_ref):
            z = z_ref[...]
            prev = jnp.where(row == 0, 0.0, pltpu.roll(z, 1, 0))
            nxt = jnp.where(row == seq - 1, 0.0, pltpu.roll(z, seq - 1, 0))
            w = w_ref[...]
            return prev * w[0:1] + z * w[1:2] + nxt * w[2:3] + b_ref[...]

        x0 = conv(z0_ref, w0_ref, b0_ref)
        x1 = conv(z1_ref, w1_ref, b1_ref)
        v = conv(z2_ref, w2_ref, b2_ref)
        x0_ref[...] = x0
        vx_ref[...] = (v * x1).astype(vx_ref.dtype)

    def zspec(part):
        return pl.BlockSpec((None, seq, tc), lambda bb, j: (bb, 0, part * nw + j))

    def wspec(rows, part):
        return pl.BlockSpec((rows, tc), lambda bb, j: (0, part * nw + j))

    out_spec = pl.BlockSpec((seq, tc), lambda bb, j: (0, bb * nw + j))
    block_bytes = 4 * _nbytes((seq, tc), F32) + _nbytes((seq, tc), BF16)
    return pl.pallas_call(
        body,
        grid=(b, nw),
        in_specs=[zspec(0), zspec(1), zspec(2), wspec(3, 0), wspec(3, 1), wspec(3, 2),
                  wspec(1, 0), wspec(1, 1), wspec(1, 2)],
        out_specs=[out_spec, out_spec],
        out_shape=[jax.ShapeDtypeStruct((seq, b * width), F32), jax.ShapeDtypeStruct((seq, b * width), BF16)],
        compiler_params=_params(("parallel", "parallel"), block_bytes, 8 * _nbytes((seq, tc), F32)),
        name="hyena_short_conv",
    )(zh, zh, zh, conv_w, conv_w, conv_w, conv_b[None, :], conv_b[None, :], conv_b[None, :])


def _dft_matrices(seq):
    n = 2 * seq
    assert n % 4 == 0

    def table(r, s):
        k = jnp.where(r <= seq, r, r - seq)
        phase = (k * s + jnp.where(r <= seq, 0, n - n // 4)) % n
        return _cos_turns(phase, n)

    r = jnp.arange(n, dtype=jnp.int32)
    s = jnp.arange(seq, dtype=jnp.int32)
    weight = jnp.where((r == 0) | (r == seq), 1.0 / n, 2.0 / n)
    fwd = table(r[:, None], s[None, :])
    inv = table(r[None, :], s[:, None]) * weight[None, :]
    return fwd.astype(BF16), inv.astype(BF16)


def _cos_turns(phase, n):
    quarter = n // 4
    quad = phase // quarter
    rem = phase - quad * quarter
    odd = (quad & 1) == 1
    x = jnp.where(odd, quarter - rem, rem).astype(F32) * (2.0 * math.pi / n)
    x2 = x * x
    acc = jnp.full_like(x2, 1.0 / math.factorial(16))
    for order in range(14, -1, -2):
        acc = acc * (-x2) + 1.0 / math.factorial(order)
    return jnp.where((quad == 1) | (quad == 2), -acc, acc)


def _spectrum_multiply(spec, batch, width, *, tr=512, tc=512):
    n = spec.shape[0]
    half = n // 2
    tr, tc = min(tr, half), min(tc, width)
    nwc = width // tc
    s3 = spec.reshape(2, half, spec.shape[1])

    def body(u_ref, f_ref, b_ref, y_ref):
        ua, ub = u_ref[0], u_ref[1]
        fa, fb = f_ref[0], f_ref[1]
        ba, bb_ = b_ref[0], b_ref[1]
        ga = fa + ba
        gb = fb - bb_
        row = pl.program_id(0) * tr + lax.broadcasted_iota(jnp.int32, ua.shape, 0)
        dc = row == 0
        ya = ua * ga - jnp.where(dc, 0.0, ub * gb)
        yb = jnp.where(dc, ub * (fb + bb_), ua * gb + ub * ga)
        y_ref[0] = ya.astype(y_ref.dtype)
        y_ref[1] = yb.astype(y_ref.dtype)

    block_bytes = 3 * _nbytes((2, tr, tc), F32) + _nbytes((2, tr, tc), BF16)
    y = pl.pallas_call(
        body,
        grid=(half // tr, nwc, batch),
        in_specs=[pl.BlockSpec((2, tr, tc), lambda i, j, bb: (0, i, bb * nwc + j)),
                  pl.BlockSpec((2, tr, tc), lambda i, j, bb: (0, i, batch * nwc + j)),
                  pl.BlockSpec((2, tr, tc), lambda i, j, bb: (0, i, (batch + 1) * nwc + j))],
        out_specs=pl.BlockSpec((2, tr, tc), lambda i, j, bb: (0, i, bb * nwc + j)),
        out_shape=jax.ShapeDtypeStruct((2, half, batch * width), BF16),
        compiler_params=_params(("parallel", "parallel", "arbitrary"), block_bytes, 8 * _nbytes((tr, tc), F32)),
        name="hyena_spectrum_multiply",
    )(s3, s3, s3)
    return y.reshape(n, batch * width)


def _hyena_branch(zh, conv_w, conv_b, filt, hy_bias):
    b, seq, w3 = zh.shape
    width = w3 // 3
    x0, vx = _hyena_prep(zh, conv_w.astype(F32), conv_b.astype(F32))
    hfilt = _hyena_filters(seq, *filt)
    fwd, inv = _dft_matrices(seq)
    spec = _mm_plain(fwd, jnp.concatenate([vx, hfilt], axis=1), F32, name="hyena_dft_forward")
    y_hat = _spectrum_multiply(spec, b, width)

    tm, tn = min(TILE_M, seq), min(TILE_N, width)
    nw = width // tn

    def epilogue(acc, ex, outs):
        x0_ref, vx_ref, bias_ref = ex
        outs[0][...] = (x0_ref[...] * (acc + bias_ref[...] * vx_ref[...].astype(F32))).astype(BF16)

    extras = [(x0, pl.BlockSpec((tm, tn), lambda i, j, k: (i, j))),
              (vx, pl.BlockSpec((tm, tn), lambda i, j, k: (i, j))),
              (hy_bias.astype(F32)[None, :], pl.BlockSpec((1, tn), lambda i, j, k: (0, j % nw)))]
    return _matmul(inv, y_hat, tm=tm, tn=tn, tk=TILE_K, epilogue=epilogue, extras=extras,
                   out_shape=[jax.ShapeDtypeStruct((b, seq, width), BF16)],
                   out_specs=[pl.BlockSpec((None, tm, tn), lambda i, j, k: (j // nw, i, j % nw))],
                   name="hyena_dft_inverse")[0]


def _ab_latents(h, w1, q_norm_g, kv_norm_g, tables, seq, q_lora, kv_lora, *, tm=512, tk=1024):
    m = h.shape[0]
    n1 = w1.shape[1]
    tm = min(tm, m, seq)
    tiles_per_seq = seq // tm

    def epilogue(acc, ex, outs):
        outs[0][...] = _rms(acc[:, :q_lora], ex[0][...]).astype(BF16)
        outs[1][...] = _rms(acc[:, q_lora:q_lora + kv_lora], ex[1][...]).astype(BF16)
        kr = acc[:, q_lora + kv_lora:]
        if tables is not None:
            kr = _rope_apply(kr, ex[2], ex[3], MLA_ROPE_LAYOUT)
        outs[2][...] = kr

    extras = [(q_norm_g.astype(F32)[None, :], pl.BlockSpec((1, q_lora), lambda i, j, k: (0, 0))),
              (kv_norm_g.astype(F32)[None, :], pl.BlockSpec((1, kv_lora), lambda i, j, k: (0, 0)))]
    if tables is not None:
        spec = pl.BlockSpec((tm, LANES), lambda i, j, k: (i % tiles_per_seq, 0))
        extras += [(tables[0], spec), (tables[1], spec)]
    widths = (q_lora, kv_lora, MLA_HEAD_PAD)
    dtypes = (BF16, BF16, F32)
    return _matmul(h, w1, tm=tm, tn=n1, tk=tk, epilogue=epilogue, extras=extras,
                   out_shape=[jax.ShapeDtypeStruct((m, wd), dt) for wd, dt in zip(widths, dtypes)],
                   out_specs=[pl.BlockSpec((tm, wd), lambda i, j, k: (i, 0)) for wd in widths],
                   name="mla_latent_projection")


def _mla_keys(ckv, w_uk, kr, heads, *, tm=1024, tn=1024):
    m = ckv.shape[0]
    n = heads * MLA_HEAD_PAD
    tm, tn = min(tm, m), min(tn, n)
    reps = tn // MLA_HEAD_PAD

    def epilogue(acc, ex, outs):
        rot = ex[0][...]
        if reps > 1:
            rot = jnp.concatenate([rot] * reps, axis=1)
        outs[0][...] = (acc + rot).astype(BF16)

    extras = [(kr, pl.BlockSpec((tm, MLA_HEAD_PAD), lambda i, j, k: (i, 0)))]
    return _matmul(ckv, w_uk, tm=tm, tn=tn, tk=ckv.shape[1], epilogue=epilogue, extras=extras,
                   out_shape=[jax.ShapeDtypeStruct((m, n), BF16)],
                   out_specs=[pl.BlockSpec((tm, tn), lambda i, j, k: (i, j))], name="mla_key_up")[0]


def _ab_mixer(h, hc, batch, seq, seq_c, need_ctx, w_in, q_norm_g, kv_norm_g, w_uq, w_ukv,
              conv_w, conv_b, filt, hy_bias):
    d = w_in.shape[0]
    heads = d // (2 * V_DIM)
    q_lora, kv_lora = w_uq.shape[0], w_ukv.shape[0]
    kv_end = q_lora + kv_lora + ROPE_DIM
    mla_scale = (NOPE_DIM + ROPE_DIM) ** -0.5
    zero_pad = MLA_HEAD_PAD - NOPE_DIM - ROPE_DIM

    w1 = jnp.concatenate([w_in[:, :q_lora + kv_lora], jnp.zeros((d, NOPE_DIM), w_in.dtype),
                          w_in[:, q_lora + kv_lora:kv_end], jnp.zeros((d, zero_pad), w_in.dtype)],
                         axis=1).astype(BF16)
    w_hy = w_in[:, kv_end:].astype(BF16)
    uq = w_uq.reshape(q_lora, heads, NOPE_DIM + ROPE_DIM)
    w_uq_p = jnp.pad(uq, ((0, 0), (0, 0), (0, zero_pad))).reshape(q_lora, heads * MLA_HEAD_PAD).astype(BF16)
    ukv = w_ukv.reshape(kv_lora, heads, NOPE_DIM + V_DIM)
    w_uk_p = jnp.pad(ukv[..., :NOPE_DIM], ((0, 0), (0, 0), (0, MLA_HEAD_PAD - NOPE_DIM))
                     ).reshape(kv_lora, heads * MLA_HEAD_PAD).astype(BF16)
    w_uv = ukv[..., NOPE_DIM:].reshape(kv_lora, heads * V_DIM).astype(BF16)

    tables = _rope_tables(seq, MLA_ROPE_LAYOUT)
    rope_kw = dict(layout=MLA_ROPE_LAYOUT)

    cq, ckv, kr = _ab_latents(h, w1, q_norm_g, kv_norm_g, tables, seq, q_lora, kv_lora)
    cqc, ckvc, krc = _ab_latents(hc, w1, q_norm_g, kv_norm_g, None, seq_c, q_lora, kv_lora)
    q = _mm_rope(cq, w_uq_p, tables, mla_scale, seq, BF16, name="mla_query_up", **rope_kw)
    k = _mla_keys(ckv, w_uk_p, kr, heads)
    v = _mm_plain(ckv, w_uv, BF16, name="mla_value_up")
    kc = _mla_keys(ckvc, w_uk_p, krc, heads)
    vc = _mm_plain(ckvc, w_uv, BF16, name="mla_value_up")

    def split(t, s):
        return t.reshape(batch, s, t.shape[-1])

    att = _mla_attention(split(q, seq), [split(kc, seq_c), split(k, seq)], [split(vc, seq_c), split(v, seq)], heads)
    zh = _mm_plain(h, w_hy, F32, tn=512, name="hyena_in_projection")
    hy = _hyena_branch(split(zh, seq), conv_w, conv_b, filt, hy_bias)
    mix = jnp.concatenate([att, hy], axis=-1).reshape(batch * seq, -1)
    if not need_ctx:
        return mix, None
    qc = _mm_rope(cqc, w_uq_p, None, mla_scale, seq_c, BF16, name="mla_query_up", **rope_kw)
    attc = _mla_attention(split(qc, seq_c), [split(kc, seq_c)], [split(vc, seq_c)], heads)
    zhc = _mm_plain(hc, w_hy, F32, tn=512, name="hyena_in_projection")
    hyc = _hyena_branch(split(zhc, seq_c), conv_w, conv_b, filt, hy_bias)
    return mix, jnp.concatenate([attc, hyc], axis=-1).reshape(batch * seq_c, -1)


def _diff_mixer(h, hc, batch, seq, seq_c, need_ctx, w_in, lam_p, subln_g, lambda_init):
    d = w_in.shape[0]
    hd = w_in.shape[1] // 3
    heads = hd // (2 * DIFF_DIM)
    scale = DIFF_DIM ** -0.5
    w_qk = _deinterleave_columns(w_in[:, :2 * hd].astype(BF16))
    w_q, w_k = w_qk[:, :hd], w_qk[:, hd:]
    w_v = w_in[:, 2 * hd:].astype(BF16)
    tables = _rope_tables(seq, DIFF_ROPE_LAYOUT)
    rope_kw = dict(layout=DIFF_ROPE_LAYOUT)

    q = _mm_rope(h, w_q, tables, scale, seq, BF16, name="diff_query_projection", **rope_kw)
    k = _mm_rope(h, w_k, tables, 1.0, seq, BF16, name="diff_key_projection", **rope_kw)
    v = _mm_plain(h, w_v, BF16, name="diff_value_projection")
    kc = _mm_rope(hc, w_k, None, 1.0, seq_c, BF16, name="diff_key_projection", **rope_kw)
    vc = _mm_plain(hc, w_v, BF16, name="diff_value_projection")

    def split(t, s):
        return t.reshape(batch, s, t.shape[-1])

    o = _diff_attention(split(q, seq), [split(kc, seq_c), split(k, seq)], [split(vc, seq_c), split(v, seq)],
                        lam_p, subln_g, lambda_init, heads)
    mix = o.reshape(batch * seq, hd)
    if not need_ctx:
        return mix, None
    qc = _mm_rope(hc, w_q, None, scale, seq_c, BF16, name="diff_query_projection", **rope_kw)
    oc = _diff_attention(split(qc, seq_c), [split(kc, seq_c)], [split(vc, seq_c)], lam_p, subln_g, lambda_init, heads)
    return mix, oc.reshape(batch * seq_c, hd)


def kernel(x, c, ctx, c_ctx, mod_w, mod_b, norm_g, ffn_w_gate, ffn_w_up, ffn_w_down, ab_w_in, mla_q_norm_g, mla_kv_norm_g, mla_w_uq, mla_w_ukv, hy_conv_w, hy_conv_b, hy_w1, hy_b1, hy_w2, hy_b2, hy_w3, hy_b3, hy_freq, hy_w_out, hy_bias, ab_w_out, c_w_in, c_lambda, c_subln_g, c_w_out, final_norm_g):
    batch, seq, d = x.shape
    seq_c = ctx.shape[1]
    depth = mod_w.shape[0]
    rows_c = batch * seq_c
    xs = x.reshape(batch * seq, d).astype(F32)
    xc = ctx.reshape(rows_c, d).astype(F32)

    cvec = jnp.concatenate([c.astype(F32), c_ctx.astype(F32)[None, :]], axis=0)
    cvec = jnp.pad(cvec, ((0, -cvec.shape[0] % 8), (0, 0)))
    mods_all = _modulation(cvec, mod_w.astype(F32), mod_b.astype(F32))
    w_gate_bf, w_up_bf, w_down_bf = ffn_w_gate.astype(BF16), ffn_w_up.astype(BF16), ffn_w_down.astype(BF16)

    for layer in range(depth):
        need_ctx = layer < depth - 1
        mods = mods_all[layer, :batch].reshape(batch, 3 * N_SUB, d)
        modc = mods_all[layer, batch:batch + 1].reshape(1, 3 * N_SUB, d)
        g = norm_g[layer].astype(F32)

        def ffn(t, m, rows, sub, which):
            return _half_ffn(t, m, sub, rows, g[sub], w_gate_bf, w_up_bf, w_down_bf, (layer, which))

        xs = ffn(xs, mods, seq, 0, 0)
        xc = ffn(xc, modc, rows_c, 0, 0)
        h = _norm_mod(xs, g[1], mods, 1, seq, BF16)
        hc = _norm_mod(xc, g[1], modc, 1, rows_c, BF16)
        i = layer // 2
        if layer % 2 == 0:
            filt = (hy_w1[i], hy_b1[i], hy_w2[i], hy_b2[i], hy_w3[i], hy_b3[i], hy_freq[i], hy_w_out[i])
            mix, mixc = _ab_mixer(h, hc, batch, seq, seq_c, need_ctx, ab_w_in[i], mla_q_norm_g[i],
                                  mla_kv_norm_g[i], mla_w_uq[i], mla_w_ukv[i], hy_conv_w[i], hy_conv_b[i],
                                  filt, hy_bias[i])
            w_out = ab_w_out[i].astype(BF16)
        else:
            lambda_init = 0.8 - 0.6 * math.exp(-0.3 * layer)
            mix, mixc = _diff_mixer(h, hc, batch, seq, seq_c, need_ctx, c_w_in[i], c_lambda[i], c_subln_g[i],
                                    lambda_init)
            w_out = c_w_out[i].astype(BF16)
        xs = _mm_residual(mix, w_out, xs, mods, 1, 1.0, seq, name="mixer_out_residual")
        xs = ffn(xs, mods, seq, 2, 1)
        if need_ctx:
            xc = _mm_residual(mixc, w_out, xc, modc, 1, 1.0, rows_c, name="mixer_out_residual")
            xc = ffn(xc, modc, rows_c, 2, 1)

    out = _norm_mod(xs, final_norm_g.astype(F32), None, 0, seq, x.dtype)
    return out.reshape(batch, seq, d)
```

```python
import functools
import math
from typing import NamedTuple

import numpy as np
import jax
import jax.numpy as jnp
from jax import lax
from jax.experimental import pallas as pl
from jax.experimental.pallas import tpu as pltpu

F32 = jnp.float32
BF16 = jnp.bfloat16

EPS = 1e-6
ROPE_THETA = 10000.0
GRID_W = 64
N_SUB = 3
NOPE_DIM = 128
ROPE_DIM = 64
V_DIM = 128
DIFF_DIM = 128
HY_EMB = 33
HY_TARGET = 1e-2
HY_SHORT_DECAY_PCT = 0.3
HY_LONG_DECAY_PCT = 1.5

LANES = 128
MXU_DIM_V7X = 256
VMEM_BYTES_V7X = 64 * 1024 * 1024
VMEM_COMPILER_RESERVE = 8 * 1024 * 1024
MLA_HEAD_PAD = MXU_DIM_V7X

TILE_M = 1024
TILE_N = 512
TILE_K = 4096
TILE_M_LONG_K = TILE_M // 2
TILE_K_LONG = 2 * TILE_K
ATTN_TQ = 256
ATTN_CHUNK = 256


def _params(semantics, block_bytes, temp_bytes=0):
    want = 2 * block_bytes + temp_bytes
    limit = min(VMEM_BYTES_V7X - VMEM_COMPILER_RESERVE, max(want, 32 * 1024 * 1024))
    return pltpu.CompilerParams(dimension_semantics=semantics, vmem_limit_bytes=int(limit))


def _nbytes(shape, dtype):
    return math.prod(shape) * jnp.dtype(dtype).itemsize


def _rms(x, g):
    return x * lax.rsqrt(jnp.mean(x * x, axis=-1, keepdims=True) + EPS) * g


def _matmul(a, w, *, tm, tn, tk, epilogue, out_shape, out_specs, extras=(), w_index=(), w_cols=None, name):
    m, kdim = a.shape
    assert w.ndim == 2 + len(w_index) and w.shape[-2] == kdim
    col0, n = (0, w.shape[-1]) if w_cols is None else w_cols
    tm, tn, tk = min(tm, m), min(tn, n), min(tk, kdim)
    assert m % tm == 0 and n % tn == 0 and kdim % tk == 0 and col0 % tn == 0, (name, a.shape, w.shape, tm, tn, tk)
    j0 = col0 // tn
    nk = kdim // tk
    n_ex, n_out = len(extras), len(out_shape)

    def body(*refs):
        a_ref, w_ref = refs[0], refs[1]
        ex = refs[2:2 + n_ex]
        outs = refs[2 + n_ex:2 + n_ex + n_out]
        prod = jnp.dot(a_ref[...], w_ref[...], preferred_element_type=F32)
        if nk == 1:
            epilogue(prod, ex, outs)
            return
        acc_ref = refs[-1]
        k = pl.program_id(2)

        @pl.when(k == 0)
        def _():
            acc_ref[...] = prod

        if nk > 2:
            @pl.when((k > 0) & (k < nk - 1))
            def _():
                acc_ref[...] += prod

        @pl.when(k == nk - 1)
        def _():
            epilogue(acc_ref[...] + prod, ex, outs)

    in_specs = [pl.BlockSpec((tm, tk), lambda i, j, k: (i, k)),
                pl.BlockSpec((None,) * len(w_index) + (tk, tn), lambda i, j, k: (*w_index, k, j + j0))]
    in_specs += [spec for _, spec in extras]
    block_bytes = _nbytes((tm, tk), a.dtype) + _nbytes((tk, tn), w.dtype)
    for arr, spec in extras:
        block_bytes += _nbytes([d for d in spec.block_shape if d is not None], arr.dtype)
    for sds, spec in zip(out_shape, out_specs):
        block_bytes += _nbytes([d for d in spec.block_shape if d is not None], sds.dtype)
    acc_bytes = _nbytes((tm, tn), F32)
    return pl.pallas_call(
        body,
        grid=(m // tm, n // tn, nk),
        in_specs=in_specs,
        out_specs=list(out_specs),
        out_shape=list(out_shape),
        scratch_shapes=[pltpu.VMEM((tm, tn), F32)] if nk > 1 else [],
        compiler_params=_params(("parallel", "parallel", "arbitrary"), block_bytes, 8 * acc_bytes),
        name=name,
    )(a, w, *[arr for arr, _ in extras])


def _mm_plain(a, w, out_dtype, *, tm=TILE_M, tn=TILE_N, tk=TILE_K, w_cols=None, name):
    m, n = a.shape[0], (w.shape[1] if w_cols is None else w_cols[1])
    tm, tn = min(tm, m), min(tn, n)

    def epilogue(acc, ex, outs):
        outs[0][...] = acc.astype(out_dtype)

    return _matmul(a, w, tm=tm, tn=tn, tk=tk, epilogue=epilogue, w_cols=w_cols,
                   out_shape=[jax.ShapeDtypeStruct((m, n), out_dtype)],
                   out_specs=[pl.BlockSpec((tm, tn), lambda i, j, k: (i, j))], name=name)[0]


def _mm_residual(a, w, res, mods, sub, coef, rows_per_group, *, tm=TILE_M, tn=TILE_N, tk=TILE_K, w_index=(), name):
    m, n = a.shape[0], w.shape[-1]
    tm, tn = min(tm, m, rows_per_group), min(tn, n)
    tiles_per_group = rows_per_group // tm
    gate_row = 3 * sub + 2

    def epilogue(acc, ex, outs):
        res_ref, mod_ref = ex
        gate = mod_ref[gate_row:gate_row + 1, :]
        outs[0][...] = res_ref[...] + (coef * gate) * acc

    extras = [(res, pl.BlockSpec((tm, tn), lambda i, j, k: (i, j))),
              (mods, pl.BlockSpec((None, 3 * N_SUB, tn), lambda i, j, k: (i // tiles_per_group, 0, j)))]
    return _matmul(a, w, tm=tm, tn=tn, tk=tk, epilogue=epilogue, extras=extras, w_index=w_index,
                   out_shape=[jax.ShapeDtypeStruct((m, n), F32)],
                   out_specs=[pl.BlockSpec((tm, tn), lambda i, j, k: (i, j))], name=name)[0]


class RopeLayout(NamedTuple):
    group: int
    lo: int
    n_freq: int
    split: bool


MLA_ROPE_LAYOUT = RopeLayout(group=MLA_HEAD_PAD, lo=NOPE_DIM, n_freq=ROPE_DIM // 4, split=False)
DIFF_ROPE_LAYOUT = RopeLayout(group=DIFF_DIM, lo=0, n_freq=DIFF_DIM // 4, split=True)


def _rope_tables(seq, layout):
    n_freq = layout.n_freq
    pos = jnp.arange(seq, dtype=jnp.int32)
    row_pos = (pos // GRID_W).astype(F32)
    col_pos = (pos % GRID_W).astype(F32)
    inv = ROPE_THETA ** (-jnp.arange(n_freq, dtype=F32) / n_freq)
    ang = jnp.concatenate([row_pos[:, None] * inv, col_pos[:, None] * inv], axis=-1)
    cos, sin = jnp.cos(ang), jnp.sin(ang)
    if layout.split:
        assert 4 * n_freq == LANES
        return jnp.concatenate([cos, cos], axis=-1), jnp.concatenate([-sin, sin], axis=-1)
    cos = jnp.repeat(cos, 2, axis=-1)
    sin = jnp.repeat(sin, 2, axis=-1) * jnp.tile(jnp.array([-1.0, 1.0], F32), 2 * n_freq)
    rest = LANES - 4 * n_freq
    return (jnp.concatenate([cos, jnp.ones((seq, rest), F32)], axis=-1),
            jnp.concatenate([sin, jnp.zeros((seq, rest), F32)], axis=-1))


def _rope_apply(x, cos_ref, sin_ref, layout):
    assert layout.lo % LANES == 0 and layout.group % LANES == 0 and x.shape[-1] % layout.group == 0
    cos, sin = cos_ref[...], sin_ref[...]
    slabs = []
    for c0 in range(0, x.shape[-1], LANES):
        slab = x[:, c0:c0 + LANES]
        if c0 % layout.group == layout.lo:
            if layout.split:
                partner = pltpu.roll(slab, LANES // 2, 1)
            else:
                even = (lax.broadcasted_iota(jnp.int32, slab.shape, 1) & 1) == 0
                partner = jnp.where(even, pltpu.roll(slab, LANES - 1, 1), pltpu.roll(slab, 1, 1))
            slab = slab * cos + partner * sin
        slabs.append(slab)
    return jnp.concatenate(slabs, axis=1) if len(slabs) > 1 else slabs[0]


def _cast_deinterleave_columns(w, n_perm):
    kdim, n = w.shape
    perm = np.concatenate([np.arange(0, LANES, 2), np.arange(1, LANES, 2)])
    p = np.zeros((LANES, LANES), np.float32)
    p[perm, np.arange(LANES)] = 1.0
    tm, tn = min(TILE_M, kdim), min(TILE_N, n)
    assert n_perm % tn == 0
    perm_blocks = n_perm // tn

    def body(w_ref, p_ref, o_ref):
        @pl.when(pl.program_id(1) < perm_blocks)
        def _():
            pm = p_ref[...]
            for c0 in range(0, tn, LANES):
                o_ref[:, c0:c0 + LANES] = jnp.dot(w_ref[:, c0:c0 + LANES].astype(BF16), pm,
                                                  preferred_element_type=F32).astype(BF16)

        @pl.when(pl.program_id(1) >= perm_blocks)
        def _():
            o_ref[...] = w_ref[...].astype(BF16)

    return pl.pallas_call(
        body,
        grid=(kdim // tm, n // tn),
        in_specs=[pl.BlockSpec((tm, tn), lambda i, j: (i, j)), pl.BlockSpec((LANES, LANES), lambda i, j: (0, 0))],
        out_specs=pl.BlockSpec((tm, tn), lambda i, j: (i, j)),
        out_shape=jax.ShapeDtypeStruct((kdim, n), BF16),
        compiler_params=_params(("parallel", "parallel"), _nbytes((tm, tn), w.dtype) + _nbytes((tm, tn), BF16)),
        name="cast_deinterleave_rotary_columns",
    )(w, jnp.asarray(p, BF16))


def _mm_rope(a, w, tables, scale, seq, out_dtype, *, layout, tm=TILE_M, tn=TILE_N, tk=TILE_K, w_cols=None, name):
    m, n = a.shape[0], (w.shape[1] if w_cols is None else w_cols[1])
    tm, tn = min(tm, m, seq), min(tn, n)
    tiles_per_seq = seq // tm

    def epilogue(acc, ex, outs):
        y = acc if tables is None else _rope_apply(acc, ex[0], ex[1], layout)
        outs[0][...] = (y * scale).astype(out_dtype)

    extras = []
    if tables is not None:
        spec = pl.BlockSpec((tm, LANES), lambda i, j, k: (i % tiles_per_seq, 0))
        extras = [(tables[0], spec), (tables[1], spec)]
    return _matmul(a, w, tm=tm, tn=tn, tk=tk, epilogue=epilogue, extras=extras, w_cols=w_cols,
                   out_shape=[jax.ShapeDtypeStruct((m, n), out_dtype)],
                   out_specs=[pl.BlockSpec((tm, tn), lambda i, j, k: (i, j))], name=name)[0]


def _modulation(cvec, mod_w, mod_b, *, tn=512):
    depth, d, n = mod_w.shape
    rows = cvec.shape[0]
    tn = min(tn, n)

    def body(c_ref, w_ref, b_ref, o_ref):
        cv = c_ref[...]
        s = cv * (1.0 / (1.0 + jnp.exp(-cv)))
        o_ref[...] = jnp.dot(s, w_ref[...], preferred_element_type=F32) + b_ref[...]

    block_bytes = _nbytes((d, tn), F32) + _nbytes((rows, d), F32) + 2 * _nbytes((rows, tn), F32)
    return pl.pallas_call(
        body,
        grid=(depth, n // tn),
        in_specs=[pl.BlockSpec((rows, d), lambda l, j: (0, 0)),
                  pl.BlockSpec((None, d, tn), lambda l, j: (l, 0, j)),
                  pl.BlockSpec((None, 1, tn), lambda l, j: (l, 0, j))],
        out_specs=pl.BlockSpec((None, rows, tn), lambda l, j: (l, 0, j)),
        out_shape=jax.ShapeDtypeStruct((depth, rows, n), F32),
        compiler_params=_params(("parallel", "parallel"), block_bytes),
        name="adaln_modulation",
    )(cvec, mod_w, mod_b.reshape(depth, 1, n))


def _norm_mod(x, g, mods, sub, rows_per_group, out_dtype, *, tm=256):
    m, d = x.shape
    tm = min(tm, m, rows_per_group)
    tiles_per_group = rows_per_group // tm

    def body(*refs):
        if mods is None:
            x_ref, g_ref, o_ref = refs
        else:
            x_ref, g_ref, mod_ref, o_ref = refs
        y = _rms(x_ref[...], g_ref[...])
        if mods is not None:
            shift = mod_ref[3 * sub:3 * sub + 1, :]
            scale = mod_ref[3 * sub + 1:3 * sub + 2, :]
            y = y * (1.0 + scale) + shift
        o_ref[...] = y.astype(out_dtype)

    in_specs = [pl.BlockSpec((tm, d), lambda i: (i, 0)), pl.BlockSpec((1, d), lambda i: (0, 0))]
    args = [x, g.reshape(1, d)]
    if mods is not None:
        in_specs.append(pl.BlockSpec((None, 3 * N_SUB, d), lambda i: (i // tiles_per_group, 0, 0)))
        args.append(mods)
    block_bytes = _nbytes((tm, d), F32) + _nbytes((tm, d), out_dtype) + _nbytes((16, d), F32)
    return pl.pallas_call(
        body,
        grid=(m // tm,),
        in_specs=in_specs,
        out_specs=pl.BlockSpec((tm, d), lambda i: (i, 0)),
        out_shape=jax.ShapeDtypeStruct((m, d), out_dtype),
        compiler_params=_params(("parallel",), block_bytes, 2 * _nbytes((tm, d), F32)),
        name="rmsnorm_modulate",
    )(*args)


def _gate_up(h, w_gate, w_up, w_index, *, tm=TILE_M, tn=TILE_N, tk=TILE_K):
    m, kdim = h.shape
    n = w_gate.shape[-1]
    tm, tn, tk = min(tm, m), min(tn, n), min(tk, kdim)
    assert m % tm == 0 and n % tn == 0 and kdim % tk == 0
    nk = kdim // tk

    def finish(g, u, o_ref):
        o_ref[...] = (g * (1.0 / (1.0 + jnp.exp(-g))) * u).astype(o_ref.dtype)

    def body(h_ref, wg_ref, wu_ref, o_ref, *acc):
        hh = h_ref[...]
        pg = jnp.dot(hh, wg_ref[...], preferred_element_type=F32)
        pu = jnp.dot(hh, wu_ref[...], preferred_element_type=F32)
        if nk == 1:
            finish(pg, pu, o_ref)
            return
        accg, accu = acc
        k = pl.program_id(2)

        @pl.when(k == 0)
        def _():
            accg[...] = pg
            accu[...] = pu

        if nk > 2:
            @pl.when((k > 0) & (k < nk - 1))
            def _():
                accg[...] += pg
                accu[...] += pu

        @pl.when(k == nk - 1)
        def _():
            finish(accg[...] + pg, accu[...] + pu, o_ref)

    block_bytes = _nbytes((tm, tk), BF16) + 2 * _nbytes((tk, tn), BF16) + _nbytes((tm, tn), BF16)
    acc_bytes = _nbytes((tm, tn), F32)
    w_spec = pl.BlockSpec((None,) * len(w_index) + (tk, tn), lambda i, j, k: (*w_index, k, j))
    return pl.pallas_call(
        body,
        grid=(m // tm, n // tn, nk),
        in_specs=[pl.BlockSpec((tm, tk), lambda i, j, k: (i, k)), w_spec, w_spec],
        out_specs=pl.BlockSpec((tm, tn), lambda i, j, k: (i, j)),
        out_shape=jax.ShapeDtypeStruct((m, n), BF16),
        scratch_shapes=[pltpu.VMEM((tm, tn), F32)] * 2 if nk > 1 else [],
        compiler_params=_params(("parallel", "parallel", "arbitrary"), block_bytes, 6 * acc_bytes),
        name="ffn_gate_up",
    )(h, w_gate, w_up)


def _half_ffn(x, mods, sub, rows_per_group, g, w_gate, w_up, w_down, w_index):
    h = _norm_mod(x, g, mods, sub, rows_per_group, BF16)
    u = _gate_up(h, w_gate, w_up, w_index)
    return _mm_residual(u, w_down, x, mods, sub, 0.5, rows_per_group, w_index=w_index,
                        tm=TILE_M_LONG_K, tk=TILE_K_LONG, name="ffn_down_residual")


class ScoreSet(NamedTuple):
    q: object
    k_cols: slice
    write: object
    read: object


def _lane_fold(x, op):
    return functools.reduce(op, [x[:, c0:c0 + LANES] for c0 in range(0, x.shape[-1], LANES)])


def _attention_step(sets, k_refs, v_refs, chunk):
    state = [dict(mx=None, ls=None, acc=None, m_old=s.read[1][...]) for s in sets]
    off = 0
    for k_ref, v_ref in zip(k_refs, v_refs):
        total = k_ref.shape[0]
        for c0 in range(0, total, chunk):
            rows = min(chunk, total - c0)
            cols = slice(off + c0, off + c0 + rows)
            for s, st in zip(sets, state):
                if s.write is not None:
                    new = lax.dot_general(s.q, k_ref[c0:c0 + rows, s.k_cols], (((1,), (1,)), ((), ())),
                                          preferred_element_type=F32)
                    s.write[0][:, cols] = new
                    part = _lane_fold(new, jnp.maximum)
                    st["mx"] = part if st["mx"] is None else jnp.maximum(st["mx"], part)
            for s, st in zip(sets, state):
                m_old = jnp.concatenate([st["m_old"]] * (rows // LANES), axis=1) if rows > LANES else st["m_old"]
                e = jnp.exp(s.read[0][:, cols] - m_old)
                part = _lane_fold(e, jnp.add)
                st["ls"] = part if st["ls"] is None else st["ls"] + part
                pv = jnp.dot(e.astype(BF16), v_ref[c0:c0 + rows, :], preferred_element_type=F32)
                st["acc"] = pv if st["acc"] is None else st["acc"] + pv
        off += total
    out = []
    for s, st in zip(sets, state):
        if s.write is not None:
            s.write[1][...] = jnp.broadcast_to(jnp.max(st["mx"], axis=-1, keepdims=True), st["mx"].shape)
        out.append((st["acc"], jnp.sum(st["ls"], axis=-1, keepdims=True)))
    return out


def _skewed_steps(nq, score_bufs, emit):
    i = pl.program_id(2)
    buf_a, buf_b = score_bufs

    @pl.when(i == 0)
    def _():
        for pair in buf_b:
            for ref in pair:
                ref[...] = jnp.zeros_like(ref)

    even = lax.rem(i, 2) == 0

    @pl.when((i < nq) & even)
    def _():
        emit(buf_a, buf_b)

    @pl.when((i < nq) & jnp.logical_not(even))
    def _():
        emit(buf_b, buf_a)

    @pl.when(i == nq)
    def _():
        emit(None, buf_a if (nq - 1) % 2 == 0 else buf_b)


def _skewed_maps(nq):
    return (lambda bb, h, i: (bb, jnp.minimum(i, nq - 1), h)), (lambda bb, h, i: (bb, jnp.maximum(i - 1, 0), h))


def _kv_specs(arrays, width):
    return [pl.BlockSpec((None, a.shape[1], width), lambda bb, h, i: (bb, 0, h)) for a in arrays]


def _mla_attention(q, ks, vs, heads, *, tq=ATTN_TQ):
    b, sq, _ = q.shape
    sk = sum(k.shape[1] for k in ks)
    tq = min(tq, sq)
    nkv = len(ks)

    nq = sq // tq
    q_map, o_map = _skewed_maps(nq)

    def body(q_ref, *refs):
        k_refs, v_refs = refs[:nkv], refs[nkv:2 * nkv]
        o_ref = refs[2 * nkv]
        s_a, m_a, s_b, m_b = refs[2 * nkv + 1:]

        def emit(write, read):
            sets = [ScoreSet(q_ref[...], slice(None), None if write is None else write[0], read[0])]
            (o, l), = _attention_step(sets, k_refs, v_refs, ATTN_CHUNK)
            o_ref[...] = (o / l).astype(o_ref.dtype)

        _skewed_steps(nq, ([(s_a, m_a)], [(s_b, m_b)]), emit)

    block_bytes = (_nbytes((tq, MLA_HEAD_PAD), BF16) + _nbytes((sk, MLA_HEAD_PAD), BF16)
                   + _nbytes((sk, V_DIM), BF16) + _nbytes((tq, V_DIM), BF16))
    score_bytes = _nbytes((tq, sk), F32)
    return pl.pallas_call(
        body,
        grid=(b, heads, nq + 1),
        in_specs=([pl.BlockSpec((None, tq, MLA_HEAD_PAD), q_map)]
                  + _kv_specs(ks, MLA_HEAD_PAD) + _kv_specs(vs, V_DIM)),
        out_specs=pl.BlockSpec((None, tq, V_DIM), o_map),
        out_shape=jax.ShapeDtypeStruct((b, sq, heads * V_DIM), BF16),
        scratch_shapes=[pltpu.VMEM((tq, sk), F32), pltpu.VMEM((tq, LANES), F32)] * 2,
        compiler_params=_params(("parallel", "parallel", "arbitrary"), block_bytes, 4 * score_bytes),
        name="mla_attention",
    )(q, *ks, *vs)


def _diff_attention(q, ks, vs, lam_p, subln_g, lambda_init, heads, *, tq=ATTN_TQ):
    b, sq, _ = q.shape
    sk = sum(k.shape[1] for k in ks)
    tq = min(tq, sq)
    hw = 2 * DIFF_DIM
    nkv = len(ks)
    nq = sq // tq
    q_map, o_map = _skewed_maps(nq)

    def body(lam_ref, g_ref, q_ref, *refs):
        k_refs, v_refs = refs[:nkv], refs[nkv:2 * nkv]
        o_ref = refs[2 * nkv]
        bufs = refs[2 * nkv + 1:]
        set_a = [(bufs[0], bufs[1]), (bufs[2], bufs[3])]
        set_b = [(bufs[4], bufs[5]), (bufs[6], bufs[7])]

        def emit(write, read):
            halves = [(q_ref[:, :DIFF_DIM], slice(0, DIFF_DIM)), (q_ref[:, DIFF_DIM:], slice(DIFF_DIM, hw))]
            sets = [ScoreSet(qh, cols, None if write is None else write[n], read[n])
                    for n, (qh, cols) in enumerate(halves)]
            (o1, l1), (o2, l2) = _attention_step(sets, k_refs, v_refs, ATTN_CHUNK)
            lp = lam_ref[...]
            lam = (jnp.exp(jnp.sum(lp[0:1] * lp[1:2], axis=-1, keepdims=True))
                   - jnp.exp(jnp.sum(lp[2:3] * lp[3:4], axis=-1, keepdims=True)) + lambda_init)
            o = o1 / l1 - (lam / l2) * o2
            o_ref[...] = (_rms(o, g_ref[...]) * (1.0 - lambda_init)).astype(o_ref.dtype)

        _skewed_steps(nq, (set_a, set_b), emit)

    block_bytes = 2 * _nbytes((tq, hw), BF16) + 2 * _nbytes((sk, hw), BF16)
    score_bytes = _nbytes((tq, sk), F32)
    return pl.pallas_call(
        body,
        grid=(b, heads, nq + 1),
        in_specs=([pl.BlockSpec((4, DIFF_DIM), lambda bb, h, i: (0, 0)),
                   pl.BlockSpec((1, hw), lambda bb, h, i: (0, 0)),
                   pl.BlockSpec((None, tq, hw), q_map)]
                  + _kv_specs(ks, hw) + _kv_specs(vs, hw)),
        out_specs=pl.BlockSpec((None, tq, hw), o_map),
        out_shape=jax.ShapeDtypeStruct((b, sq, heads * hw), BF16),
        scratch_shapes=[pltpu.VMEM((tq, sk), F32), pltpu.VMEM((tq, LANES), F32)] * 4,
        compiler_params=_params(("parallel", "parallel", "arbitrary"), block_bytes, 6 * score_bytes),
        name="diff_attention",
    )(lam_p.astype(F32), subln_g.reshape(1, hw).astype(F32), q, *ks, *vs)


def _hyena_filters(seq, w1, b1, w2, b2, w3, b3, freq, w_out, *, tl=256, tn=4096):
    hid = w1.shape[1]
    width2 = w_out.shape[1]
    width = width2 // 2
    bands = (HY_EMB - 1) // 2
    t = jnp.linspace(0.0, 1.0, seq, dtype=F32)[:, None]
    ang = ((2.0 * math.pi / seq) * jnp.arange(seq, dtype=F32)[:, None]
           * jnp.linspace(1e-4, bands - 1, bands, dtype=F32)[None, :])
    z = jnp.concatenate([t, jnp.cos(ang), -jnp.sin(ang), jnp.zeros((seq, LANES - HY_EMB), F32)], axis=-1)
    deltas = jnp.abs(jnp.linspace(math.log(HY_TARGET) / HY_LONG_DECAY_PCT,
                                  math.log(HY_TARGET) / HY_SHORT_DECAY_PCT, width, dtype=F32))
    deltas2 = jnp.concatenate([deltas, deltas])[None, :]

    def pad2(w):
        return jnp.pad(w.astype(F32), ((0, LANES - w.shape[0]), (0, LANES - w.shape[1])))

    def pad_row(v):
        return jnp.pad(v.astype(F32), (0, LANES - v.shape[0]))[None, :]

    w_out_p = jnp.pad(w_out.astype(F32), ((0, LANES - hid), (0, 0)))
    tl, tn = min(tl, seq), min(tn, width2)
    exact = lax.Precision.HIGHEST

    def body(z_ref, w1_ref, b1_ref, w2_ref, b2_ref, w3_ref, b3_ref, f_ref, wo_ref, d_ref, o_ref):
        zz, f = z_ref[...], f_ref[...]
        a = jnp.sin(f * (jnp.dot(zz, w1_ref[...], precision=exact, preferred_element_type=F32) + b1_ref[...]))
        a = jnp.sin(f * (jnp.dot(a, w2_ref[...], precision=exact, preferred_element_type=F32) + b2_ref[...]))
        a = jnp.sin(f * (jnp.dot(a, w3_ref[...], precision=exact, preferred_element_type=F32) + b3_ref[...]))
        h = jnp.dot(a, wo_ref[...], precision=exact, preferred_element_type=F32)
        h = h * jnp.exp(-zz[:, 0:1] * d_ref[...])
        row = pl.program_id(0) * tl + lax.broadcasted_iota(jnp.int32, h.shape, 0)
        col = pl.program_id(1) * tn + lax.broadcasted_iota(jnp.int32, h.shape, 1)
        o_ref[...] = jnp.where((row == 0) & (col >= width), 0.0, h).astype(o_ref.dtype)

    sq = pl.BlockSpec((LANES, LANES), lambda i, j: (0, 0))
    vec = pl.BlockSpec((1, LANES), lambda i, j: (0, 0))
    block_bytes = _nbytes((tl, LANES), F32) + _nbytes((LANES, tn), F32) + 2 * _nbytes((tl, tn), F32)
    return pl.pallas_call(
        body,
        grid=(seq // tl, width2 // tn),
        in_specs=[pl.BlockSpec((tl, LANES), lambda i, j: (i, 0)), sq, vec, sq, vec, sq, vec, vec,
                  pl.BlockSpec((LANES, tn), lambda i, j: (0, j)),
                  pl.BlockSpec((1, tn), lambda i, j: (0, j))],
        out_specs=pl.BlockSpec((tl, tn), lambda i, j: (i, j)),
        out_shape=jax.ShapeDtypeStruct((seq, width2), BF16),
        compiler_params=_params(("parallel", "parallel"), block_bytes, 4 * _nbytes((tl, tn), F32)),
        name="hyena_filters",
    )(z, pad2(w1), pad_row(b1), pad2(w2), pad_row(b2), pad2(w3), pad_row(b3), pad_row(freq), w_out_p, deltas2)


def _hyena_prep(zh, conv_w, conv_b, *, tc=128):
    b, seq, w3 = zh.shape
    width = w3 // 3
    tc = min(tc, width)
    nw = width // tc

    def body(z0_ref, z1_ref, z2_ref, w0_ref, w1_ref, w2_ref, b0_ref, b1_ref, b2_ref, x0_ref, vx_ref):
        row = lax.broadcasted_iota(jnp.int32, (seq, 1), 0)

        def conv(z_ref, w_ref, b_ref):
            z = z_ref[...]
            prev = jnp.where(row == 0, 0.0, pltpu.roll(z, 1, 0))
            nxt = jnp.where(row == seq - 1, 0.0, pltpu.roll(z, seq - 1, 0))
            w = w_ref[...]
            return prev * w[0:1] + z * w[1:2] + nxt * w[2:3] + b_ref[...]

        x0 = conv(z0_ref, w0_ref, b0_ref)
        x1 = conv(z1_ref, w1_ref, b1_ref)
        v = conv(z2_ref, w2_ref, b2_ref)
        x0_ref[...] = x0
        vx_ref[...] = (v * x1).astype(vx_ref.dtype)

    def zspec(part):
        return pl.BlockSpec((None, seq, tc), lambda bb, j: (bb, 0, part * nw + j))

    def wspec(rows, part):
        return pl.BlockSpec((rows, tc), lambda bb, j: (0, part * nw + j))

    out_spec = pl.BlockSpec((seq, tc), lambda bb, j: (0, bb * nw + j))
    block_bytes = 4 * _nbytes((seq, tc), F32) + _nbytes((seq, tc), BF16)
    return pl.pallas_call(
        body,
        grid=(b, nw),
        in_specs=[zspec(0), zspec(1), zspec(2), wspec(3, 0), wspec(3, 1), wspec(3, 2),
                  wspec(1, 0), wspec(1, 1), wspec(1, 2)],
        out_specs=[out_spec, out_spec],
        out_shape=[jax.ShapeDtypeStruct((seq, b * width), F32), jax.ShapeDtypeStruct((seq, b * width), BF16)],
        compiler_params=_params(("parallel", "parallel"), block_bytes, 8 * _nbytes((seq, tc), F32)),
        name="hyena_short_conv",
    )(zh, zh, zh, conv_w, conv_w, conv_w, conv_b[None, :], conv_b[None, :], conv_b[None, :])


def _dft_matrices(seq):
    n = 2 * seq
    assert n % 4 == 0

    def table(r, s):
        k = jnp.where(r <= seq, r, r - seq)
        return _cos_turns(_mod_nonneg(k * s + jnp.where(r <= seq, 0, n - n // 4), n), n)

    r = jnp.arange(n, dtype=jnp.int32)
    s = jnp.arange(seq, dtype=jnp.int32)
    weight = jnp.where((r == 0) | (r == seq), 1.0 / n, 2.0 / n)
    fwd = table(r[:, None], s[None, :])
    inv = table(r[None, :], s[:, None]) * weight[None, :]
    return fwd.astype(BF16), inv.astype(BF16)


def _div_nonneg(x, d):
    return x >> (d.bit_length() - 1) if d & (d - 1) == 0 else x // d


def _mod_nonneg(x, d):
    return x & (d - 1) if d & (d - 1) == 0 else x % d


def _cos_turns(phase, n):
    quarter = n // 4
    quad = _div_nonneg(phase, quarter)
    rem = phase - quad * quarter
    odd = (quad & 1) == 1
    x = jnp.where(odd, quarter - rem, rem).astype(F32) * (2.0 * math.pi / n)
    x2 = x * x
    acc = jnp.full_like(x2, 1.0 / math.factorial(16))
    for order in range(14, -1, -2):
        acc = acc * (-x2) + 1.0 / math.factorial(order)
    return jnp.where((quad == 1) | (quad == 2), -acc, acc)


def _spectrum_multiply(spec, batch, width, *, tr=512, tc=512):
    n = spec.shape[0]
    half = n // 2
    tr, tc = min(tr, half), min(tc, width)
    nwc = width // tc
    s3 = spec.reshape(2, half, spec.shape[1])

    def body(u_ref, f_ref, b_ref, y_ref):
        ua, ub = u_ref[0], u_ref[1]
        fa, fb = f_ref[0], f_ref[1]
        ba, bb_ = b_ref[0], b_ref[1]
        ga = fa + ba
        gb = fb - bb_
        row = pl.program_id(0) * tr + lax.broadcasted_iota(jnp.int32, ua.shape, 0)
        dc = row == 0
        ya = ua * ga - jnp.where(dc, 0.0, ub * gb)
        yb = jnp.where(dc, ub * (fb + bb_), ua * gb + ub * ga)
        y_ref[0] = ya.astype(y_ref.dtype)
        y_ref[1] = yb.astype(y_ref.dtype)

    block_bytes = 3 * _nbytes((2, tr, tc), F32) + _nbytes((2, tr, tc), BF16)
    y = pl.pallas_call(
        body,
        grid=(half // tr, nwc, batch),
        in_specs=[pl.BlockSpec((2, tr, tc), lambda i, j, bb: (0, i, bb * nwc + j)),
                  pl.BlockSpec((2, tr, tc), lambda i, j, bb: (0, i, batch * nwc + j)),
                  pl.BlockSpec((2, tr, tc), lambda i, j, bb: (0, i, (batch + 1) * nwc + j))],
        out_specs=pl.BlockSpec((2, tr, tc), lambda i, j, bb: (0, i, bb * nwc + j)),
        out_shape=jax.ShapeDtypeStruct((2, half, batch * width), BF16),
        compiler_params=_params(("parallel", "parallel", "arbitrary"), block_bytes, 8 * _nbytes((tr, tc), F32)),
        name="hyena_spectrum_multiply",
    )(s3, s3, s3)
    return y.reshape(n, batch * width)


def _hyena_branch(zh, conv_w, conv_b, filt, hy_bias):
    b, seq, w3 = zh.shape
    width = w3 // 3
    x0, vx = _hyena_prep(zh, conv_w.astype(F32), conv_b.astype(F32))
    hfilt = _hyena_filters(seq, *filt)
    fwd, inv = _dft_matrices(seq)
    spec = _mm_plain(fwd, jnp.concatenate([vx, hfilt], axis=1), F32, name="hyena_dft_forward")
    y_hat = _spectrum_multiply(spec, b, width)

    tm, tn = min(TILE_M_LONG_K, seq), min(TILE_N, width)
    nw = width // tn

    def epilogue(acc, ex, outs):
        x0_ref, vx_ref, bias_ref = ex
        outs[0][...] = (x0_ref[...] * (acc + bias_ref[...] * vx_ref[...].astype(F32))).astype(BF16)

    extras = [(x0, pl.BlockSpec((tm, tn), lambda i, j, k: (i, j))),
              (vx, pl.BlockSpec((tm, tn), lambda i, j, k: (i, j))),
              (hy_bias.astype(F32)[None, :], pl.BlockSpec((1, tn), lambda i, j, k: (0, j % nw)))]
    return _matmul(inv, y_hat, tm=tm, tn=tn, tk=TILE_K_LONG, epilogue=epilogue, extras=extras,
                   out_shape=[jax.ShapeDtypeStruct((b, seq, width), BF16)],
                   out_specs=[pl.BlockSpec((None, tm, tn), lambda i, j, k: (j // nw, i, j % nw))],
                   name="hyena_dft_inverse")[0]


def _ab_latents(h, w1, q_norm_g, kv_norm_g, tables, seq, q_lora, kv_lora, *, tm=512, tk=1024):
    m = h.shape[0]
    n1 = w1.shape[1]
    tm = min(tm, m, seq)
    tiles_per_seq = seq // tm

    def epilogue(acc, ex, outs):
        outs[0][...] = _rms(acc[:, :q_lora], ex[0][...]).astype(BF16)
        outs[1][...] = _rms(acc[:, q_lora:q_lora + kv_lora], ex[1][...]).astype(BF16)
        kr = acc[:, q_lora + kv_lora:]
        if tables is not None:
            kr = _rope_apply(kr, ex[2], ex[3], MLA_ROPE_LAYOUT)
        outs[2][...] = kr

    extras = [(q_norm_g.astype(F32)[None, :], pl.BlockSpec((1, q_lora), lambda i, j, k: (0, 0))),
              (kv_norm_g.astype(F32)[None, :], pl.BlockSpec((1, kv_lora), lambda i, j, k: (0, 0)))]
    if tables is not None:
        spec = pl.BlockSpec((tm, LANES), lambda i, j, k: (i % tiles_per_seq, 0))
        extras += [(tables[0], spec), (tables[1], spec)]
    widths = (q_lora, kv_lora, MLA_HEAD_PAD)
    dtypes = (BF16, BF16, F32)
    return _matmul(h, w1, tm=tm, tn=n1, tk=tk, epilogue=epilogue, extras=extras,
                   out_shape=[jax.ShapeDtypeStruct((m, wd), dt) for wd, dt in zip(widths, dtypes)],
                   out_specs=[pl.BlockSpec((tm, wd), lambda i, j, k: (i, 0)) for wd in widths],
                   name="mla_latent_projection")


def _mla_keys(ckv, w_uk, kr, heads, *, tm=1024, tn=1024):
    m = ckv.shape[0]
    n = heads * MLA_HEAD_PAD
    tm, tn = min(tm, m), min(tn, n)
    reps = tn // MLA_HEAD_PAD

    def epilogue(acc, ex, outs):
        rot = ex[0][...]
        if reps > 1:
            rot = jnp.concatenate([rot] * reps, axis=1)
        outs[0][...] = (acc + rot).astype(BF16)

    extras = [(kr, pl.BlockSpec((tm, MLA_HEAD_PAD), lambda i, j, k: (i, 0)))]
    return _matmul(ckv, w_uk, tm=tm, tn=tn, tk=ckv.shape[1], epilogue=epilogue, extras=extras,
                   out_shape=[jax.ShapeDtypeStruct((m, n), BF16)],
                   out_specs=[pl.BlockSpec((tm, tn), lambda i, j, k: (i, j))], name="mla_key_up")[0]


def _ab_mixer(h, hc, batch, seq, seq_c, need_ctx, w_in, q_norm_g, kv_norm_g, w_uq, w_ukv,
              conv_w, conv_b, filt, hy_bias):
    d = w_in.shape[0]
    heads = d // (2 * V_DIM)
    q_lora, kv_lora = w_uq.shape[0], w_ukv.shape[0]
    kv_end = q_lora + kv_lora + ROPE_DIM
    mla_scale = (NOPE_DIM + ROPE_DIM) ** -0.5
    zero_pad = MLA_HEAD_PAD - NOPE_DIM - ROPE_DIM

    w1 = jnp.concatenate([w_in[:, :q_lora + kv_lora], jnp.zeros((d, NOPE_DIM), w_in.dtype),
                          w_in[:, q_lora + kv_lora:kv_end], jnp.zeros((d, zero_pad), w_in.dtype)],
                         axis=1).astype(BF16)
    w_hy = w_in[:, kv_end:].astype(BF16)
    uq = w_uq.reshape(q_lora, heads, NOPE_DIM + ROPE_DIM)
    w_uq_p = jnp.pad(uq, ((0, 0), (0, 0), (0, zero_pad))).reshape(q_lora, heads * MLA_HEAD_PAD).astype(BF16)
    ukv = w_ukv.reshape(kv_lora, heads, NOPE_DIM + V_DIM)
    w_uk_p = jnp.pad(ukv[..., :NOPE_DIM], ((0, 0), (0, 0), (0, MLA_HEAD_PAD - NOPE_DIM))
                     ).reshape(kv_lora, heads * MLA_HEAD_PAD).astype(BF16)
    w_uv = ukv[..., NOPE_DIM:].reshape(kv_lora, heads * V_DIM).astype(BF16)

    tables = _rope_tables(seq, MLA_ROPE_LAYOUT)
    rope_kw = dict(layout=MLA_ROPE_LAYOUT)

    cq, ckv, kr = _ab_latents(h, w1, q_norm_g, kv_norm_g, tables, seq, q_lora, kv_lora)
    cqc, ckvc, krc = _ab_latents(hc, w1, q_norm_g, kv_norm_g, None, seq_c, q_lora, kv_lora)
    q = _mm_rope(cq, w_uq_p, tables, mla_scale, seq, BF16, name="mla_query_up", **rope_kw)
    k = _mla_keys(ckv, w_uk_p, kr, heads)
    v = _mm_plain(ckv, w_uv, BF16, name="mla_value_up")
    kc = _mla_keys(ckvc, w_uk_p, krc, heads)
    vc = _mm_plain(ckvc, w_uv, BF16, name="mla_value_up")

    def split(t, s):
        return t.reshape(batch, s, t.shape[-1])

    att = _mla_attention(split(q, seq), [split(kc, seq_c), split(k, seq)], [split(vc, seq_c), split(v, seq)], heads)
    zh = _mm_plain(h, w_hy, F32, tn=512, name="hyena_in_projection")
    hy = _hyena_branch(split(zh, seq), conv_w, conv_b, filt, hy_bias)
    mix = jnp.concatenate([att, hy], axis=-1).reshape(batch * seq, -1)
    if not need_ctx:
        return mix, None
    qc = _mm_rope(cqc, w_uq_p, None, mla_scale, seq_c, BF16, name="mla_query_up", **rope_kw)
    attc = _mla_attention(split(qc, seq_c), [split(kc, seq_c)], [split(vc, seq_c)], heads)
    zhc = _mm_plain(hc, w_hy, F32, tn=512, name="hyena_in_projection")
    hyc = _hyena_branch(split(zhc, seq_c), conv_w, conv_b, filt, hy_bias)
    return mix, jnp.concatenate([attc, hyc], axis=-1).reshape(batch * seq_c, -1)


def _diff_mixer(h, hc, batch, seq, seq_c, need_ctx, w_in, lam_p, subln_g, lambda_init):
    d = w_in.shape[0]
    hd = w_in.shape[1] // 3
    heads = hd // (2 * DIFF_DIM)
    scale = DIFF_DIM ** -0.5
    w = _cast_deinterleave_columns(w_in, 2 * hd)
    q_cols, k_cols, v_cols = (0, hd), (hd, hd), (2 * hd, hd)
    tables = _rope_tables(seq, DIFF_ROPE_LAYOUT)
    rope_kw = dict(layout=DIFF_ROPE_LAYOUT)

    q = _mm_rope(h, w, tables, scale, seq, BF16, w_cols=q_cols, name="diff_query_projection", **rope_kw)
    k = _mm_rope(h, w, tables, 1.0, seq, BF16, w_cols=k_cols, name="diff_key_projection", **rope_kw)
    v = _mm_plain(h, w, BF16, w_cols=v_cols, name="diff_value_projection")
    kc = _mm_rope(hc, w, None, 1.0, seq_c, BF16, w_cols=k_cols, name="diff_key_projection", **rope_kw)
    vc = _mm_plain(hc, w, BF16, w_cols=v_cols, name="diff_value_projection")

    def split(t, s):
        return t.reshape(batch, s, t.shape[-1])

    o = _diff_attention(split(q, seq), [split(kc, seq_c), split(k, seq)], [split(vc, seq_c), split(v, seq)],
                        lam_p, subln_g, lambda_init, heads)
    mix = o.reshape(batch * seq, hd)
    if not need_ctx:
        return mix, None
    qc = _mm_rope(hc, w, None, scale, seq_c, BF16, w_cols=q_cols, name="diff_query_projection", **rope_kw)
    oc = _diff_attention(split(qc, seq_c), [split(kc, seq_c)], [split(vc, seq_c)], lam_p, subln_g, lambda_init, heads)
    return mix, oc.reshape(batch * seq_c, hd)


def kernel(x, c, ctx, c_ctx, mod_w, mod_b, norm_g, ffn_w_gate, ffn_w_up, ffn_w_down, ab_w_in, mla_q_norm_g, mla_kv_norm_g, mla_w_uq, mla_w_ukv, hy_conv_w, hy_conv_b, hy_w1, hy_b1, hy_w2, hy_b2, hy_w3, hy_b3, hy_freq, hy_w_out, hy_bias, ab_w_out, c_w_in, c_lambda, c_subln_g, c_w_out, final_norm_g):
    batch, seq, d = x.shape
    seq_c = ctx.shape[1]
    depth = mod_w.shape[0]
    rows_c = batch * seq_c
    xs = x.reshape(batch * seq, d).astype(F32)
    xc = ctx.reshape(rows_c, d).astype(F32)

    cvec = jnp.concatenate([c.astype(F32), c_ctx.astype(F32)[None, :]], axis=0)
    cvec = jnp.pad(cvec, ((0, -cvec.shape[0] % 8), (0, 0)))
    mods_all = _modulation(cvec, mod_w.astype(F32), mod_b.astype(F32))
    w_gate_bf, w_up_bf, w_down_bf = ffn_w_gate.astype(BF16), ffn_w_up.astype(BF16), ffn_w_down.astype(BF16)

    for layer in range(depth):
        need_ctx = layer < depth - 1
        mods = mods_all[layer, :batch].reshape(batch, 3 * N_SUB, d)
        modc = mods_all[layer, batch:batch + 1].reshape(1, 3 * N_SUB, d)
        g = norm_g[layer].astype(F32)

        def ffn(t, m, rows, sub, which):
            return _half_ffn(t, m, sub, rows, g[sub], w_gate_bf, w_up_bf, w_down_bf, (layer, which))

        xs = ffn(xs, mods, seq, 0, 0)
        xc = ffn(xc, modc, rows_c, 0, 0)
        h = _norm_mod(xs, g[1], mods, 1, seq, BF16)
        hc = _norm_mod(xc, g[1], modc, 1, rows_c, BF16)
        i = layer // 2
        if layer % 2 == 0:
            filt = (hy_w1[i], hy_b1[i], hy_w2[i], hy_b2[i], hy_w3[i], hy_b3[i], hy_freq[i], hy_w_out[i])
            mix, mixc = _ab_mixer(h, hc, batch, seq, seq_c, need_ctx, ab_w_in[i], mla_q_norm_g[i],
                                  mla_kv_norm_g[i], mla_w_uq[i], mla_w_ukv[i], hy_conv_w[i], hy_conv_b[i],
                                  filt, hy_bias[i])
            w_out = ab_w_out[i].astype(BF16)
        else:
            lambda_init = 0.8 - 0.6 * math.exp(-0.3 * layer)
            mix, mixc = _diff_mixer(h, hc, batch, seq, seq_c, need_ctx, c_w_in[i], c_lambda[i], c_subln_g[i],
                                    lambda_init)
            w_out = c_w_out[i].astype(BF16)
        xs = _mm_residual(mix, w_out, xs, mods, 1, 1.0, seq, name="mixer_out_residual")
        xs = ffn(xs, mods, seq, 2, 1)
        if need_ctx:
            xc = _mm_residual(mixc, w_out, xc, modc, 1, 1.0, rows_c, name="mixer_out_residual")
            xc = ffn(xc, modc, rows_c, 2, 1)

    out = _norm_mod(xs, final_norm_g.astype(F32), None, 0, seq, x.dtype)
    return out.reshape(batch, seq, d)
```

```python
import functools
import math
from typing import NamedTuple

import numpy as np
import jax
import jax.numpy as jnp
from jax import lax
from jax.experimental import pallas as pl
from jax.experimental.pallas import tpu as pltpu

F32 = jnp.float32
BF16 = jnp.bfloat16

EPS = 1e-6
ROPE_THETA = 10000.0
GRID_W = 64
N_SUB = 3
NOPE_DIM = 128
ROPE_DIM = 64
V_DIM = 128
DIFF_DIM = 128
HY_EMB = 33
HY_TARGET = 1e-2
HY_SHORT_DECAY_PCT = 0.3
HY_LONG_DECAY_PCT = 1.5

LANES = 128
MXU_DIM_V7X = 256
VMEM_BYTES_V7X = 64 * 1024 * 1024
VMEM_COMPILER_RESERVE = 8 * 1024 * 1024
MLA_HEAD_PAD = MXU_DIM_V7X

TILE_M = 1024
TILE_N = 512
TILE_K = 4096
TILE_M_LONG_K = TILE_M // 2
TILE_K_LONG = 2 * TILE_K
ATTN_TQ = 256
ATTN_CHUNK = 256


def _params(semantics, block_bytes, temp_bytes=0):
    want = 2 * block_bytes + temp_bytes
    limit = min(VMEM_BYTES_V7X - VMEM_COMPILER_RESERVE, max(want, 32 * 1024 * 1024))
    return pltpu.CompilerParams(dimension_semantics=semantics, vmem_limit_bytes=int(limit))


def _nbytes(shape, dtype):
    return math.prod(shape) * jnp.dtype(dtype).itemsize


def _rms(x, g):
    return x * lax.rsqrt(jnp.mean(x * x, axis=-1, keepdims=True) + EPS) * g


def _matmul(a, w, *, tm, tn, tk, epilogue, out_shape, out_specs, extras=(), w_index=(), w_cols=None, name):
    m, kdim = a.shape
    assert w.ndim == 2 + len(w_index) and w.shape[-2] == kdim
    col0, n = (0, w.shape[-1]) if w_cols is None else w_cols
    tm, tn, tk = min(tm, m), min(tn, n), min(tk, kdim)
    assert m % tm == 0 and n % tn == 0 and kdim % tk == 0 and col0 % tn == 0, (name, a.shape, w.shape, tm, tn, tk)
    j0 = col0 // tn
    nk = kdim // tk
    n_ex, n_out = len(extras), len(out_shape)

    def body(*refs):
        a_ref, w_ref = refs[0], refs[1]
        ex = refs[2:2 + n_ex]
        outs = refs[2 + n_ex:2 + n_ex + n_out]
        prod = jnp.dot(a_ref[...], w_ref[...].astype(BF16), preferred_element_type=F32)
        if nk == 1:
            epilogue(prod, ex, outs)
            return
        acc_ref = refs[-1]
        k = pl.program_id(2)

        @pl.when(k == 0)
        def _():
            acc_ref[...] = prod

        if nk > 2:
            @pl.when((k > 0) & (k < nk - 1))
            def _():
                acc_ref[...] += prod

        @pl.when(k == nk - 1)
        def _():
            epilogue(acc_ref[...] + prod, ex, outs)

    in_specs = [pl.BlockSpec((tm, tk), lambda i, j, k: (i, k)),
                pl.BlockSpec((None,) * len(w_index) + (tk, tn), lambda i, j, k: (*w_index, k, j + j0))]
    in_specs += [spec for _, spec in extras]
    block_bytes = _nbytes((tm, tk), a.dtype) + _nbytes((tk, tn), w.dtype)
    for arr, spec in extras:
        block_bytes += _nbytes([d for d in spec.block_shape if d is not None], arr.dtype)
    for sds, spec in zip(out_shape, out_specs):
        block_bytes += _nbytes([d for d in spec.block_shape if d is not None], sds.dtype)
    acc_bytes = _nbytes((tm, tn), F32)
    return pl.pallas_call(
        body,
        grid=(m // tm, n // tn, nk),
        in_specs=in_specs,
        out_specs=list(out_specs),
        out_shape=list(out_shape),
        scratch_shapes=[pltpu.VMEM((tm, tn), F32)] if nk > 1 else [],
        compiler_params=_params(("parallel", "parallel", "arbitrary"), block_bytes, 8 * acc_bytes),
        name=name,
    )(a, w, *[arr for arr, _ in extras])


def _mm_plain(a, w, out_dtype, *, tm=TILE_M, tn=TILE_N, tk=TILE_K, w_cols=None, name):
    m, n = a.shape[0], (w.shape[1] if w_cols is None else w_cols[1])
    tm, tn = min(tm, m), min(tn, n)

    def epilogue(acc, ex, outs):
        outs[0][...] = acc.astype(out_dtype)

    return _matmul(a, w, tm=tm, tn=tn, tk=tk, epilogue=epilogue, w_cols=w_cols,
                   out_shape=[jax.ShapeDtypeStruct((m, n), out_dtype)],
                   out_specs=[pl.BlockSpec((tm, tn), lambda i, j, k: (i, j))], name=name)[0]


def _mm_residual(a, w, res, mods, sub, coef, rows_per_group, *, tm=TILE_M, tn=TILE_N, tk=TILE_K, w_index=(), name):
    m, n = a.shape[0], w.shape[-1]
    tm, tn = min(tm, m, rows_per_group), min(tn, n)
    tiles_per_group = rows_per_group // tm
    gate_row = 3 * sub + 2

    def epilogue(acc, ex, outs):
        res_ref, mod_ref = ex
        gate = mod_ref[gate_row:gate_row + 1, :]
        outs[0][...] = res_ref[...] + (coef * gate) * acc

    extras = [(res, pl.BlockSpec((tm, tn), lambda i, j, k: (i, j))),
              (mods, pl.BlockSpec((None, 3 * N_SUB, tn), lambda i, j, k: (i // tiles_per_group, 0, j)))]
    return _matmul(a, w, tm=tm, tn=tn, tk=tk, epilogue=epilogue, extras=extras, w_index=w_index,
                   out_shape=[jax.ShapeDtypeStruct((m, n), F32)],
                   out_specs=[pl.BlockSpec((tm, tn), lambda i, j, k: (i, j))], name=name)[0]


class RopeLayout(NamedTuple):
    group: int
    lo: int
    n_freq: int
    split: bool


MLA_ROPE_LAYOUT = RopeLayout(group=MLA_HEAD_PAD, lo=NOPE_DIM, n_freq=ROPE_DIM // 4, split=False)
DIFF_ROPE_LAYOUT = RopeLayout(group=DIFF_DIM, lo=0, n_freq=DIFF_DIM // 4, split=True)


def _rope_tables(seq, layout):
    n_freq = layout.n_freq
    pos = jnp.arange(seq, dtype=jnp.int32)
    row_pos = (pos // GRID_W).astype(F32)
    col_pos = (pos % GRID_W).astype(F32)
    inv = ROPE_THETA ** (-jnp.arange(n_freq, dtype=F32) / n_freq)
    ang = jnp.concatenate([row_pos[:, None] * inv, col_pos[:, None] * inv], axis=-1)
    cos, sin = jnp.cos(ang), jnp.sin(ang)
    if layout.split:
        assert 4 * n_freq == LANES
        return jnp.concatenate([cos, cos], axis=-1), jnp.concatenate([-sin, sin], axis=-1)
    cos = jnp.repeat(cos, 2, axis=-1)
    sin = jnp.repeat(sin, 2, axis=-1) * jnp.tile(jnp.array([-1.0, 1.0], F32), 2 * n_freq)
    rest = LANES - 4 * n_freq
    return (jnp.concatenate([cos, jnp.ones((seq, rest), F32)], axis=-1),
            jnp.concatenate([sin, jnp.zeros((seq, rest), F32)], axis=-1))


def _rope_apply(x, cos_ref, sin_ref, layout):
    assert layout.lo % LANES == 0 and layout.group % LANES == 0 and x.shape[-1] % layout.group == 0
    cos, sin = cos_ref[...], sin_ref[...]
    slabs = []
    for c0 in range(0, x.shape[-1], LANES):
        slab = x[:, c0:c0 + LANES]
        if c0 % layout.group == layout.lo:
            if layout.split:
                partner = pltpu.roll(slab, LANES // 2, 1)
            else:
                even = (lax.broadcasted_iota(jnp.int32, slab.shape, 1) & 1) == 0
                partner = jnp.where(even, pltpu.roll(slab, LANES - 1, 1), pltpu.roll(slab, 1, 1))
            slab = slab * cos + partner * sin
        slabs.append(slab)
    return jnp.concatenate(slabs, axis=1) if len(slabs) > 1 else slabs[0]


def _cast_deinterleave_columns(w, n_perm):
    kdim, n = w.shape
    perm = np.concatenate([np.arange(0, LANES, 2), np.arange(1, LANES, 2)])
    p = np.zeros((LANES, LANES), np.float32)
    p[perm, np.arange(LANES)] = 1.0
    tm, tn = min(TILE_M, kdim), min(TILE_N, n)
    assert n_perm % tn == 0
    perm_blocks = n_perm // tn

    def body(w_ref, p_ref, o_ref):
        @pl.when(pl.program_id(1) < perm_blocks)
        def _():
            pm = p_ref[...]
            for c0 in range(0, tn, LANES):
                o_ref[:, c0:c0 + LANES] = jnp.dot(w_ref[:, c0:c0 + LANES].astype(BF16), pm,
                                                  preferred_element_type=F32).astype(BF16)

        @pl.when(pl.program_id(1) >= perm_blocks)
        def _():
            o_ref[...] = w_ref[...].astype(BF16)

    return pl.pallas_call(
        body,
        grid=(kdim // tm, n // tn),
        in_specs=[pl.BlockSpec((tm, tn), lambda i, j: (i, j)), pl.BlockSpec((LANES, LANES), lambda i, j: (0, 0))],
        out_specs=pl.BlockSpec((tm, tn), lambda i, j: (i, j)),
        out_shape=jax.ShapeDtypeStruct((kdim, n), BF16),
        compiler_params=_params(("parallel", "parallel"), _nbytes((tm, tn), w.dtype) + _nbytes((tm, tn), BF16)),
        name="cast_deinterleave_rotary_columns",
    )(w, jnp.asarray(p, BF16))


def _mm_rope(a, w, tables, scale, seq, out_dtype, *, layout, tm=TILE_M, tn=TILE_N, tk=TILE_K, w_cols=None, name):
    m, n = a.shape[0], (w.shape[1] if w_cols is None else w_cols[1])
    tm, tn = min(tm, m, seq), min(tn, n)
    tiles_per_seq = seq // tm

    def epilogue(acc, ex, outs):
        y = acc if tables is None else _rope_apply(acc, ex[0], ex[1], layout)
        outs[0][...] = (y * scale).astype(out_dtype)

    extras = []
    if tables is not None:
        spec = pl.BlockSpec((tm, LANES), lambda i, j, k: (i % tiles_per_seq, 0))
        extras = [(tables[0], spec), (tables[1], spec)]
    return _matmul(a, w, tm=tm, tn=tn, tk=tk, epilogue=epilogue, extras=extras, w_cols=w_cols,
                   out_shape=[jax.ShapeDtypeStruct((m, n), out_dtype)],
                   out_specs=[pl.BlockSpec((tm, tn), lambda i, j, k: (i, j))], name=name)[0]


def _modulation(cvec, mod_w, mod_b, *, tn=512):
    depth, d, n = mod_w.shape
    rows = cvec.shape[0]
    tn = min(tn, n)

    def body(c_ref, w_ref, b_ref, o_ref):
        cv = c_ref[...]
        s = cv * (1.0 / (1.0 + jnp.exp(-cv)))
        o_ref[...] = jnp.dot(s, w_ref[...], preferred_element_type=F32) + b_ref[...]

    block_bytes = _nbytes((d, tn), F32) + _nbytes((rows, d), F32) + 2 * _nbytes((rows, tn), F32)
    return pl.pallas_call(
        body,
        grid=(depth, n // tn),
        in_specs=[pl.BlockSpec((rows, d), lambda l, j: (0, 0)),
                  pl.BlockSpec((None, d, tn), lambda l, j: (l, 0, j)),
                  pl.BlockSpec((None, 1, tn), lambda l, j: (l, 0, j))],
        out_specs=pl.BlockSpec((None, rows, tn), lambda l, j: (l, 0, j)),
        out_shape=jax.ShapeDtypeStruct((depth, rows, n), F32),
        compiler_params=_params(("parallel", "parallel"), block_bytes),
        name="adaln_modulation",
    )(cvec, mod_w, mod_b.reshape(depth, 1, n))


def _norm_mod(x, g, mods, sub, rows_per_group, out_dtype, *, tm=256):
    m, d = x.shape
    tm = min(tm, m, rows_per_group)
    tiles_per_group = rows_per_group // tm

    def body(*refs):
        if mods is None:
            x_ref, g_ref, o_ref = refs
        else:
            x_ref, g_ref, mod_ref, o_ref = refs
        y = _rms(x_ref[...], g_ref[...])
        if mods is not None:
            shift = mod_ref[3 * sub:3 * sub + 1, :]
            scale = mod_ref[3 * sub + 1:3 * sub + 2, :]
            y = y * (1.0 + scale) + shift
        o_ref[...] = y.astype(out_dtype)

    in_specs = [pl.BlockSpec((tm, d), lambda i: (i, 0)), pl.BlockSpec((1, d), lambda i: (0, 0))]
    args = [x, g.reshape(1, d)]
    if mods is not None:
        in_specs.append(pl.BlockSpec((None, 3 * N_SUB, d), lambda i: (i // tiles_per_group, 0, 0)))
        args.append(mods)
    block_bytes = _nbytes((tm, d), F32) + _nbytes((tm, d), out_dtype) + _nbytes((16, d), F32)
    return pl.pallas_call(
        body,
        grid=(m // tm,),
        in_specs=in_specs,
        out_specs=pl.BlockSpec((tm, d), lambda i: (i, 0)),
        out_shape=jax.ShapeDtypeStruct((m, d), out_dtype),
        compiler_params=_params(("parallel",), block_bytes, 2 * _nbytes((tm, d), F32)),
        name="rmsnorm_modulate",
    )(*args)


def _gate_up(h, w_gate, w_up, w_index, *, tm=TILE_M, tn=TILE_N, tk=TILE_K):
    m, kdim = h.shape
    n = w_gate.shape[-1]
    tn = tn * jnp.dtype(BF16).itemsize // jnp.dtype(w_gate.dtype).itemsize
    tm, tn, tk = min(tm, m), min(tn, n), min(tk, kdim)
    assert m % tm == 0 and n % tn == 0 and kdim % tk == 0
    nk = kdim // tk

    def finish(g, u, o_ref):
        o_ref[...] = (g * (1.0 / (1.0 + jnp.exp(-g))) * u).astype(o_ref.dtype)

    def body(h_ref, wg_ref, wu_ref, o_ref, *acc):
        hh = h_ref[...]
        pg = jnp.dot(hh, wg_ref[...].astype(BF16), preferred_element_type=F32)
        pu = jnp.dot(hh, wu_ref[...].astype(BF16), preferred_element_type=F32)
        if nk == 1:
            finish(pg, pu, o_ref)
            return
        accg, accu = acc
        k = pl.program_id(2)

        @pl.when(k == 0)
        def _():
            accg[...] = pg
            accu[...] = pu

        if nk > 2:
            @pl.when((k > 0) & (k < nk - 1))
            def _():
                accg[...] += pg
                accu[...] += pu

        @pl.when(k == nk - 1)
        def _():
            finish(accg[...] + pg, accu[...] + pu, o_ref)

    block_bytes = _nbytes((tm, tk), BF16) + 2 * _nbytes((tk, tn), w_gate.dtype) + _nbytes((tm, tn), BF16)
    acc_bytes = _nbytes((tm, tn), F32) + _nbytes((tk, tn), BF16)
    w_spec = pl.BlockSpec((None,) * len(w_index) + (tk, tn), lambda i, j, k: (*w_index, k, j))
    return pl.pallas_call(
        body,
        grid=(m // tm, n // tn, nk),
        in_specs=[pl.BlockSpec((tm, tk), lambda i, j, k: (i, k)), w_spec, w_spec],
        out_specs=pl.BlockSpec((tm, tn), lambda i, j, k: (i, j)),
        out_shape=jax.ShapeDtypeStruct((m, n), BF16),
        scratch_shapes=[pltpu.VMEM((tm, tn), F32)] * 2 if nk > 1 else [],
        compiler_params=_params(("parallel", "parallel", "arbitrary"), block_bytes, 6 * acc_bytes),
        name="ffn_gate_up",
    )(h, w_gate, w_up)


def _half_ffn(x, mods, sub, rows_per_group, g, w_gate, w_up, w_down, w_index):
    h = _norm_mod(x, g, mods, sub, rows_per_group, BF16)
    u = _gate_up(h, w_gate, w_up, w_index)
    return _mm_residual(u, w_down, x, mods, sub, 0.5, rows_per_group, w_index=w_index,
                        tm=TILE_M_LONG_K, tk=TILE_K_LONG, name="ffn_down_residual")


class ScoreSet(NamedTuple):
    q: object
    k_cols: slice
    write: object
    read: object


def _lane_fold(x, op):
    return functools.reduce(op, [x[:, c0:c0 + LANES] for c0 in range(0, x.shape[-1], LANES)])


def _attention_step(sets, k_refs, v_refs, chunk):
    state = [dict(mx=None, ls=None, acc=None, m_old=s.read[1][...]) for s in sets]
    off = 0
    for k_ref, v_ref in zip(k_refs, v_refs):
        total = k_ref.shape[0]
        for c0 in range(0, total, chunk):
            rows = min(chunk, total - c0)
            cols = slice(off + c0, off + c0 + rows)
            for s, st in zip(sets, state):
                if s.write is not None:
                    new = lax.dot_general(s.q, k_ref[c0:c0 + rows, s.k_cols], (((1,), (1,)), ((), ())),
                                          preferred_element_type=F32)
                    s.write[0][:, cols] = new
                    part = _lane_fold(new, jnp.maximum)
                    st["mx"] = part if st["mx"] is None else jnp.maximum(st["mx"], part)
            for s, st in zip(sets, state):
                m_old = jnp.concatenate([st["m_old"]] * (rows // LANES), axis=1) if rows > LANES else st["m_old"]
                e = jnp.exp(s.read[0][:, cols] - m_old)
                part = _lane_fold(e, jnp.add)
                st["ls"] = part if st["ls"] is None else st["ls"] + part
                pv = jnp.dot(e.astype(BF16), v_ref[c0:c0 + rows, :], preferred_element_type=F32)
                st["acc"] = pv if st["acc"] is None else st["acc"] + pv
        off += total
    out = []
    for s, st in zip(sets, state):
        if s.write is not None:
            s.write[1][...] = jnp.broadcast_to(jnp.max(st["mx"], axis=-1, keepdims=True), st["mx"].shape)
        out.append((st["acc"], jnp.sum(st["ls"], axis=-1, keepdims=True)))
    return out


def _skewed_steps(nq, score_bufs, emit):
    i = pl.program_id(2)
    buf_a, buf_b = score_bufs

    @pl.when(i == 0)
    def _():
        for pair in buf_b:
            for ref in pair:
                ref[...] = jnp.zeros_like(ref)

    even = lax.rem(i, 2) == 0

    @pl.when((i < nq) & even)
    def _():
        emit(buf_a, buf_b)

    @pl.when((i < nq) & jnp.logical_not(even))
    def _():
        emit(buf_b, buf_a)

    @pl.when(i == nq)
    def _():
        emit(None, buf_a if (nq - 1) % 2 == 0 else buf_b)


def _skewed_maps(nq):
    return (lambda bb, h, i: (bb, jnp.minimum(i, nq - 1), h)), (lambda bb, h, i: (bb, jnp.maximum(i - 1, 0), h))


def _kv_specs(arrays, width):
    return [pl.BlockSpec((None, a.shape[1], width), lambda bb, h, i: (bb, 0, h)) for a in arrays]


def _mla_attention(q, ks, vs, heads, *, tq=ATTN_TQ):
    b, sq, _ = q.shape
    sk = sum(k.shape[1] for k in ks)
    tq = min(tq, sq)
    nkv = len(ks)

    nq = sq // tq
    q_map, o_map = _skewed_maps(nq)

    def body(q_ref, *refs):
        k_refs, v_refs = refs[:nkv], refs[nkv:2 * nkv]
        o_ref = refs[2 * nkv]
        s_a, m_a, s_b, m_b = refs[2 * nkv + 1:]

        def emit(write, read):
            sets = [ScoreSet(q_ref[...], slice(None), None if write is None else write[0], read[0])]
            (o, l), = _attention_step(sets, k_refs, v_refs, ATTN_CHUNK)
            o_ref[...] = (o / l).astype(o_ref.dtype)

        _skewed_steps(nq, ([(s_a, m_a)], [(s_b, m_b)]), emit)

    block_bytes = (_nbytes((tq, MLA_HEAD_PAD), BF16) + _nbytes((sk, MLA_HEAD_PAD), BF16)
                   + _nbytes((sk, V_DIM), BF16) + _nbytes((tq, V_DIM), BF16))
    score_bytes = _nbytes((tq, sk), F32)
    return pl.pallas_call(
        body,
        grid=(b, heads, nq + 1),
        in_specs=([pl.BlockSpec((None, tq, MLA_HEAD_PAD), q_map)]
                  + _kv_specs(ks, MLA_HEAD_PAD) + _kv_specs(vs, V_DIM)),
        out_specs=pl.BlockSpec((None, tq, V_DIM), o_map),
        out_shape=jax.ShapeDtypeStruct((b, sq, heads * V_DIM), BF16),
        scratch_shapes=[pltpu.VMEM((tq, sk), F32), pltpu.VMEM((tq, LANES), F32)] * 2,
        compiler_params=_params(("parallel", "parallel", "arbitrary"), block_bytes, 4 * score_bytes),
        name="mla_attention",
    )(q, *ks, *vs)


def _diff_attention(q, ks, vs, lam_p, subln_g, lambda_init, heads, *, tq=ATTN_TQ):
    b, sq, _ = q.shape
    sk = sum(k.shape[1] for k in ks)
    tq = min(tq, sq)
    hw = 2 * DIFF_DIM
    nkv = len(ks)
    nq = sq // tq
    q_map, o_map = _skewed_maps(nq)

    def body(lam_ref, g_ref, q_ref, *refs):
        k_refs, v_refs = refs[:nkv], refs[nkv:2 * nkv]
        o_ref = refs[2 * nkv]
        bufs = refs[2 * nkv + 1:]
        set_a = [(bufs[0], bufs[1]), (bufs[2], bufs[3])]
        set_b = [(bufs[4], bufs[5]), (bufs[6], bufs[7])]

        def emit(write, read):
            halves = [(q_ref[:, :DIFF_DIM], slice(0, DIFF_DIM)), (q_ref[:, DIFF_DIM:], slice(DIFF_DIM, hw))]
            sets = [ScoreSet(qh, cols, None if write is None else write[n], read[n])
                    for n, (qh, cols) in enumerate(halves)]
            (o1, l1), (o2, l2) = _attention_step(sets, k_refs, v_refs, ATTN_CHUNK)
            lp = lam_ref[...]
            lam = (jnp.exp(jnp.sum(lp[0:1] * lp[1:2], axis=-1, keepdims=True))
                   - jnp.exp(jnp.sum(lp[2:3] * lp[3:4], axis=-1, keepdims=True)) + lambda_init)
            o = o1 / l1 - (lam / l2) * o2
            o_ref[...] = (_rms(o, g_ref[...]) * (1.0 - lambda_init)).astype(o_ref.dtype)

        _skewed_steps(nq, (set_a, set_b), emit)

    block_bytes = 2 * _nbytes((tq, hw), BF16) + 2 * _nbytes((sk, hw), BF16)
    score_bytes = _nbytes((tq, sk), F32)
    return pl.pallas_call(
        body,
        grid=(b, heads, nq + 1),
        in_specs=([pl.BlockSpec((4, DIFF_DIM), lambda bb, h, i: (0, 0)),
                   pl.BlockSpec((1, hw), lambda bb, h, i: (0, 0)),
                   pl.BlockSpec((None, tq, hw), q_map)]
                  + _kv_specs(ks, hw) + _kv_specs(vs, hw)),
        out_specs=pl.BlockSpec((None, tq, hw), o_map),
        out_shape=jax.ShapeDtypeStruct((b, sq, heads * hw), BF16),
        scratch_shapes=[pltpu.VMEM((tq, sk), F32), pltpu.VMEM((tq, LANES), F32)] * 4,
        compiler_params=_params(("parallel", "parallel", "arbitrary"), block_bytes, 6 * score_bytes),
        name="diff_attention",
    )(lam_p.astype(F32), subln_g.reshape(1, hw).astype(F32), q, *ks, *vs)


def _hyena_filters(seq, w1, b1, w2, b2, w3, b3, freq, w_out, *, tl=256, tn=4096):
    hid = w1.shape[1]
    width2 = w_out.shape[1]
    width = width2 // 2
    bands = (HY_EMB - 1) // 2
    t = jnp.linspace(0.0, 1.0, seq, dtype=F32)[:, None]
    ang = ((2.0 * math.pi / seq) * jnp.arange(seq, dtype=F32)[:, None]
           * jnp.linspace(1e-4, bands - 1, bands, dtype=F32)[None, :])
    z = jnp.concatenate([t, jnp.cos(ang), -jnp.sin(ang), jnp.zeros((seq, LANES - HY_EMB), F32)], axis=-1)
    deltas = jnp.abs(jnp.linspace(math.log(HY_TARGET) / HY_LONG_DECAY_PCT,
                                  math.log(HY_TARGET) / HY_SHORT_DECAY_PCT, width, dtype=F32))
    deltas2 = jnp.concatenate([deltas, deltas])[None, :]

    def pad2(w):
        return jnp.pad(w.astype(F32), ((0, LANES - w.shape[0]), (0, LANES - w.shape[1])))

    def pad_row(v):
        return jnp.pad(v.astype(F32), (0, LANES - v.shape[0]))[None, :]

    w_out_p = jnp.pad(w_out.astype(F32), ((0, LANES - hid), (0, 0)))
    tl, tn = min(tl, seq), min(tn, width2)
    exact = lax.Precision.HIGHEST

    def body(z_ref, w1_ref, b1_ref, w2_ref, b2_ref, w3_ref, b3_ref, f_ref, wo_ref, d_ref, o_ref):
        zz, f = z_ref[...], f_ref[...]
        a = jnp.sin(f * (jnp.dot(zz, w1_ref[...], precision=exact, preferred_element_type=F32) + b1_ref[...]))
        a = jnp.sin(f * (jnp.dot(a, w2_ref[...], precision=exact, preferred_element_type=F32) + b2_ref[...]))
        a = jnp.sin(f * (jnp.dot(a, w3_ref[...], precision=exact, preferred_element_type=F32) + b3_ref[...]))
        h = jnp.dot(a, wo_ref[...], precision=exact, preferred_element_type=F32)
        h = h * jnp.exp(-zz[:, 0:1] * d_ref[...])
        row = pl.program_id(0) * tl + lax.broadcasted_iota(jnp.int32, h.shape, 0)
        col = pl.program_id(1) * tn + lax.broadcasted_iota(jnp.int32, h.shape, 1)
        o_ref[...] = jnp.where((row == 0) & (col >= width), 0.0, h).astype(o_ref.dtype)

    sq = pl.BlockSpec((LANES, LANES), lambda i, j: (0, 0))
    vec = pl.BlockSpec((1, LANES), lambda i, j: (0, 0))
    block_bytes = _nbytes((tl, LANES), F32) + _nbytes((LANES, tn), F32) + 2 * _nbytes((tl, tn), F32)
    return pl.pallas_call(
        body,
        grid=(seq // tl, width2 // tn),
        in_specs=[pl.BlockSpec((tl, LANES), lambda i, j: (i, 0)), sq, vec, sq, vec, sq, vec, vec,
                  pl.BlockSpec((LANES, tn), lambda i, j: (0, j)),
                  pl.BlockSpec((1, tn), lambda i, j: (0, j))],
        out_specs=pl.BlockSpec((tl, tn), lambda i, j: (i, j)),
        out_shape=jax.ShapeDtypeStruct((seq, width2), BF16),
        compiler_params=_params(("parallel", "parallel"), block_bytes, 4 * _nbytes((tl, tn), F32)),
        name="hyena_filters",
    )(z, pad2(w1), pad_row(b1), pad2(w2), pad_row(b2), pad2(w3), pad_row(b3), pad_row(freq), w_out_p, deltas2)


def _hyena_prep(zh, conv_w, conv_b, *, tc=128):
    b, seq, w3 = zh.shape
    width = w3 // 3
    tc = min(tc, width)
    nw = width // tc

    def body(z0_ref, z1_ref, z2_ref, w0_ref, w1_ref, w2_ref, b0_ref, b1_ref, b2_ref, x0_ref, vx_ref):
        row = lax.broadcasted_iota(jnp.int32, (seq, 1), 0)

        def conv(z_ref, w_ref, b_ref):
            z = z_ref[...]
            prev = jnp.where(row == 0, 0.0, pltpu.roll(z, 1, 0))
            nxt = jnp.where(row == seq - 1, 0.0, pltpu.roll(z, seq - 1, 0))
            w = w_ref[...]
            return prev * w[0:1] + z * w[1:2] + nxt * w[2:3] + b_ref[...]

        x0 = conv(z0_ref, w0_ref, b0_ref)
        x1 = conv(z1_ref, w1_ref, b1_ref)
        v = conv(z2_ref, w2_ref, b2_ref)
        x0_ref[...] = x0
        vx_ref[...] = (v * x1).astype(vx_ref.dtype)

    def zspec(part):
        return pl.BlockSpec((None, seq, tc), lambda bb, j: (bb, 0, part * nw + j))

    def wspec(rows, part):
        return pl.BlockSpec((rows, tc), lambda bb, j: (0, part * nw + j))

    out_spec = pl.BlockSpec((seq, tc), lambda bb, j: (0, bb * nw + j))
    block_bytes = 4 * _nbytes((seq, tc), F32) + _nbytes((seq, tc), BF16)
    return pl.pallas_call(
        body,
        grid=(b, nw),
        in_specs=[zspec(0), zspec(1), zspec(2), wspec(3, 0), wspec(3, 1), wspec(3, 2),
                  wspec(1, 0), wspec(1, 1), wspec(1, 2)],
        out_specs=[out_spec, out_spec],
        out_shape=[jax.ShapeDtypeStruct((seq, b * width), F32), jax.ShapeDtypeStruct((seq, b * width), BF16)],
        compiler_params=_params(("parallel", "parallel"), block_bytes, 8 * _nbytes((seq, tc), F32)),
        name="hyena_short_conv",
    )(zh, zh, zh, conv_w, conv_w, conv_w, conv_b[None, :], conv_b[None, :], conv_b[None, :])


def _dft_matrices(seq):
    n = 2 * seq
    assert n % 4 == 0
    kb = math.gcd(seq, 64)
    s = jnp.arange(seq, dtype=jnp.int32)

    def cos_sin(k):
        phase = _mod_nonneg(k[:, None] * s[None, :], n)
        return _cos_turns(phase, n), _cos_turns(_mod_nonneg(phase + (n - n // 4), n), n)

    c_hi, s_hi = cos_sin(jnp.arange(seq // kb, dtype=jnp.int32) * kb)
    c_lo, s_lo = cos_sin(jnp.arange(kb, dtype=jnp.int32))
    cos = (c_hi[:, None, :] * c_lo[None] - s_hi[:, None, :] * s_lo[None]).reshape(seq, seq)
    sin = (s_hi[:, None, :] * c_lo[None] + c_hi[:, None, :] * s_lo[None]).reshape(seq, seq)
    nyquist = jnp.where((s & 1) == 0, 1.0, -1.0).astype(F32)
    weight = jnp.where(s == 0, 1.0 / n, 2.0 / n).astype(F32)
    fwd = jnp.concatenate([cos, jnp.where(s[:, None] == 0, nyquist[None, :], sin)], axis=0)
    inv = jnp.concatenate([cos * weight[None, :],
                           jnp.where(s[None, :] == 0, nyquist[:, None], sin) * weight[None, :]], axis=1)
    return fwd.astype(BF16), inv.astype(BF16)


def _div_nonneg(x, d):
    return x >> (d.bit_length() - 1) if d & (d - 1) == 0 else x // d


def _mod_nonneg(x, d):
    return x & (d - 1) if d & (d - 1) == 0 else x % d


def _cos_turns(phase, n):
    quarter = n // 4
    quad = _div_nonneg(phase, quarter)
    rem = phase - quad * quarter
    odd = (quad & 1) == 1
    x = jnp.where(odd, quarter - rem, rem).astype(F32) * (2.0 * math.pi / n)
    x2 = x * x
    acc = jnp.full_like(x2, 1.0 / math.factorial(16))
    for order in range(14, -1, -2):
        acc = acc * (-x2) + 1.0 / math.factorial(order)
    return jnp.where((quad == 1) | (quad == 2), -acc, acc)


def _spectrum_multiply(spec, batch, width, *, tr=512, tc=512):
    n = spec.shape[0]
    half = n // 2
    tr, tc = min(tr, half), min(tc, width)
    nwc = width // tc
    s3 = spec.reshape(2, half, spec.shape[1])

    def body(u_ref, f_ref, b_ref, y_ref):
        ua, ub = u_ref[0], u_ref[1]
        fa, fb = f_ref[0], f_ref[1]
        ba, bb_ = b_ref[0], b_ref[1]
        ga = fa + ba
        gb = fb - bb_
        row = pl.program_id(0) * tr + lax.broadcasted_iota(jnp.int32, ua.shape, 0)
        dc = row == 0
        ya = ua * ga - jnp.where(dc, 0.0, ub * gb)
        yb = jnp.where(dc, ub * (fb + bb_), ua * gb + ub * ga)
        y_ref[0] = ya.astype(y_ref.dtype)
        y_ref[1] = yb.astype(y_ref.dtype)

    block_bytes = 3 * _nbytes((2, tr, tc), F32) + _nbytes((2, tr, tc), BF16)
    y = pl.pallas_call(
        body,
        grid=(half // tr, nwc, batch),
        in_specs=[pl.BlockSpec((2, tr, tc), lambda i, j, bb: (0, i, bb * nwc + j)),
                  pl.BlockSpec((2, tr, tc), lambda i, j, bb: (0, i, batch * nwc + j)),
                  pl.BlockSpec((2, tr, tc), lambda i, j, bb: (0, i, (batch + 1) * nwc + j))],
        out_specs=pl.BlockSpec((2, tr, tc), lambda i, j, bb: (0, i, bb * nwc + j)),
        out_shape=jax.ShapeDtypeStruct((2, half, batch * width), BF16),
        compiler_params=_params(("parallel", "parallel", "arbitrary"), block_bytes, 8 * _nbytes((tr, tc), F32)),
        name="hyena_spectrum_multiply",
    )(s3, s3, s3)
    return y.reshape(n, batch * width)


def _hyena_branch(zh, conv_w, conv_b, filt, hy_bias):
    b, seq, w3 = zh.shape
    width = w3 // 3
    x0, vx = _hyena_prep(zh, conv_w.astype(F32), conv_b.astype(F32))
    hfilt = _hyena_filters(seq, *filt)
    fwd, inv = _dft_matrices(seq)
    spec = _mm_plain(fwd, jnp.concatenate([vx, hfilt], axis=1), F32, name="hyena_dft_forward")
    y_hat = _spectrum_multiply(spec, b, width)

    tm, tn = min(TILE_M_LONG_K, seq), min(TILE_N, width)
    nw = width // tn

    def epilogue(acc, ex, outs):
        x0_ref, vx_ref, bias_ref = ex
        outs[0][...] = (x0_ref[...] * (acc + bias_ref[...] * vx_ref[...].astype(F32))).astype(BF16)

    extras = [(x0, pl.BlockSpec((tm, tn), lambda i, j, k: (i, j))),
              (vx, pl.BlockSpec((tm, tn), lambda i, j, k: (i, j))),
              (hy_bias.astype(F32)[None, :], pl.BlockSpec((1, tn), lambda i, j, k: (0, j % nw)))]
    return _matmul(inv, y_hat, tm=tm, tn=tn, tk=TILE_K_LONG, epilogue=epilogue, extras=extras,
                   out_shape=[jax.ShapeDtypeStruct((b, seq, width), BF16)],
                   out_specs=[pl.BlockSpec((None, tm, tn), lambda i, j, k: (j // nw, i, j % nw))],
                   name="hyena_dft_inverse")[0]


def _ab_latents(h, w1, q_norm_g, kv_norm_g, tables, seq, q_lora, kv_lora, *, tm=512, tk=1024):
    m = h.shape[0]
    n1 = w1.shape[1]
    tm = min(tm, m, seq)
    tiles_per_seq = seq // tm

    def epilogue(acc, ex, outs):
        outs[0][...] = _rms(acc[:, :q_lora], ex[0][...]).astype(BF16)
        outs[1][...] = _rms(acc[:, q_lora:q_lora + kv_lora], ex[1][...]).astype(BF16)
        kr = acc[:, q_lora + kv_lora:]
        if tables is not None:
            kr = _rope_apply(kr, ex[2], ex[3], MLA_ROPE_LAYOUT)
        outs[2][...] = kr

    extras = [(q_norm_g.astype(F32)[None, :], pl.BlockSpec((1, q_lora), lambda i, j, k: (0, 0))),
              (kv_norm_g.astype(F32)[None, :], pl.BlockSpec((1, kv_lora), lambda i, j, k: (0, 0)))]
    if tables is not None:
        spec = pl.BlockSpec((tm, LANES), lambda i, j, k: (i % tiles_per_seq, 0))
        extras += [(tables[0], spec), (tables[1], spec)]
    widths = (q_lora, kv_lora, MLA_HEAD_PAD)
    dtypes = (BF16, BF16, F32)
    return _matmul(h, w1, tm=tm, tn=n1, tk=tk, epilogue=epilogue, extras=extras,
                   out_shape=[jax.ShapeDtypeStruct((m, wd), dt) for wd, dt in zip(widths, dtypes)],
                   out_specs=[pl.BlockSpec((tm, wd), lambda i, j, k: (i, 0)) for wd in widths],
                   name="mla_latent_projection")


def _mla_keys(ckv, w_uk, kr, heads, *, tm=1024, tn=1024):
    m = ckv.shape[0]
    n = heads * MLA_HEAD_PAD
    tm, tn = min(tm, m), min(tn, n)
    reps = tn // MLA_HEAD_PAD

    def epilogue(acc, ex, outs):
        rot = ex[0][...]
        if reps > 1:
            rot = jnp.concatenate([rot] * reps, axis=1)
        outs[0][...] = (acc + rot).astype(BF16)

    extras = [(kr, pl.BlockSpec((tm, MLA_HEAD_PAD), lambda i, j, k: (i, 0)))]
    return _matmul(ckv, w_uk, tm=tm, tn=tn, tk=ckv.shape[1], epilogue=epilogue, extras=extras,
                   out_shape=[jax.ShapeDtypeStruct((m, n), BF16)],
                   out_specs=[pl.BlockSpec((tm, tn), lambda i, j, k: (i, j))], name="mla_key_up")[0]


def _ab_mixer(h, hc, batch, seq, seq_c, need_ctx, w_in, q_norm_g, kv_norm_g, w_uq, w_ukv,
              conv_w, conv_b, filt, hy_bias):
    d = w_in.shape[0]
    heads = d // (2 * V_DIM)
    q_lora, kv_lora = w_uq.shape[0], w_ukv.shape[0]
    kv_end = q_lora + kv_lora + ROPE_DIM
    mla_scale = (NOPE_DIM + ROPE_DIM) ** -0.5
    zero_pad = MLA_HEAD_PAD - NOPE_DIM - ROPE_DIM

    w1 = jnp.concatenate([w_in[:, :q_lora + kv_lora], jnp.zeros((d, NOPE_DIM), w_in.dtype),
                          w_in[:, q_lora + kv_lora:kv_end], jnp.zeros((d, zero_pad), w_in.dtype)],
                         axis=1).astype(BF16)
    w_hy = w_in[:, kv_end:].astype(BF16)
    uq = w_uq.reshape(q_lora, heads, NOPE_DIM + ROPE_DIM)
    w_uq_p = jnp.pad(uq, ((0, 0), (0, 0), (0, zero_pad))).reshape(q_lora, heads * MLA_HEAD_PAD).astype(BF16)
    ukv = w_ukv.reshape(kv_lora, heads, NOPE_DIM + V_DIM)
    w_uk_p = jnp.pad(ukv[..., :NOPE_DIM], ((0, 0), (0, 0), (0, MLA_HEAD_PAD - NOPE_DIM))
                     ).reshape(kv_lora, heads * MLA_HEAD_PAD).astype(BF16)
    w_uv = ukv[..., NOPE_DIM:].reshape(kv_lora, heads * V_DIM).astype(BF16)

    tables = _rope_tables(seq, MLA_ROPE_LAYOUT)
    rope_kw = dict(layout=MLA_ROPE_LAYOUT)

    cq, ckv, kr = _ab_latents(h, w1, q_norm_g, kv_norm_g, tables, seq, q_lora, kv_lora)
    cqc, ckvc, krc = _ab_latents(hc, w1, q_norm_g, kv_norm_g, None, seq_c, q_lora, kv_lora)
    q = _mm_rope(cq, w_uq_p, tables, mla_scale, seq, BF16, name="mla_query_up", **rope_kw)
    k = _mla_keys(ckv, w_uk_p, kr, heads)
    v = _mm_plain(ckv, w_uv, BF16, name="mla_value_up")
    kc = _mla_keys(ckvc, w_uk_p, krc, heads)
    vc = _mm_plain(ckvc, w_uv, BF16, name="mla_value_up")

    def split(t, s):
        return t.reshape(batch, s, t.shape[-1])

    att = _mla_attention(split(q, seq), [split(kc, seq_c), split(k, seq)], [split(vc, seq_c), split(v, seq)], heads)
    zh = _mm_plain(h, w_hy, F32, tn=512, name="hyena_in_projection")
    hy = _hyena_branch(split(zh, seq), conv_w, conv_b, filt, hy_bias)
    mix = jnp.concatenate([att, hy], axis=-1).reshape(batch * seq, -1)
    if not need_ctx:
        return mix, None
    qc = _mm_rope(cqc, w_uq_p, None, mla_scale, seq_c, BF16, name="mla_query_up", **rope_kw)
    attc = _mla_attention(split(qc, seq_c), [split(kc, seq_c)], [split(vc, seq_c)], heads)
    zhc = _mm_plain(hc, w_hy, F32, tn=512, name="hyena_in_projection")
    hyc = _hyena_branch(split(zhc, seq_c), conv_w, conv_b, filt, hy_bias)
    return mix, jnp.concatenate([attc, hyc], axis=-1).reshape(batch * seq_c, -1)


def _diff_mixer(h, hc, batch, seq, seq_c, need_ctx, w_in, lam_p, subln_g, lambda_init):
    d = w_in.shape[0]
    hd = w_in.shape[1] // 3
    heads = hd // (2 * DIFF_DIM)
    scale = DIFF_DIM ** -0.5
    w = _cast_deinterleave_columns(w_in, 2 * hd)
    q_cols, k_cols, v_cols = (0, hd), (hd, hd), (2 * hd, hd)
    tables = _rope_tables(seq, DIFF_ROPE_LAYOUT)
    rope_kw = dict(layout=DIFF_ROPE_LAYOUT)

    q = _mm_rope(h, w, tables, scale, seq, BF16, w_cols=q_cols, name="diff_query_projection", **rope_kw)
    k = _mm_rope(h, w, tables, 1.0, seq, BF16, w_cols=k_cols, name="diff_key_projection", **rope_kw)
    v = _mm_plain(h, w, BF16, w_cols=v_cols, name="diff_value_projection")
    kc = _mm_rope(hc, w, None, 1.0, seq_c, BF16, w_cols=k_cols, name="diff_key_projection", **rope_kw)
    vc = _mm_plain(hc, w, BF16, w_cols=v_cols, name="diff_value_projection")

    def split(t, s):
        return t.reshape(batch, s, t.shape[-1])

    o = _diff_attention(split(q, seq), [split(kc, seq_c), split(k, seq)], [split(vc, seq_c), split(v, seq)],
                        lam_p, subln_g, lambda_init, heads)
    mix = o.reshape(batch * seq, hd)
    if not need_ctx:
        return mix, None
    qc = _mm_rope(hc, w, None, scale, seq_c, BF16, w_cols=q_cols, name="diff_query_projection", **rope_kw)
    oc = _diff_attention(split(qc, seq_c), [split(kc, seq_c)], [split(vc, seq_c)], lam_p, subln_g, lambda_init, heads)
    return mix, oc.reshape(batch * seq_c, hd)


def kernel(x, c, ctx, c_ctx, mod_w, mod_b, norm_g, ffn_w_gate, ffn_w_up, ffn_w_down, ab_w_in, mla_q_norm_g, mla_kv_norm_g, mla_w_uq, mla_w_ukv, hy_conv_w, hy_conv_b, hy_w1, hy_b1, hy_w2, hy_b2, hy_w3, hy_b3, hy_freq, hy_w_out, hy_bias, ab_w_out, c_w_in, c_lambda, c_subln_g, c_w_out, final_norm_g):
    batch, seq, d = x.shape
    seq_c = ctx.shape[1]
    depth = mod_w.shape[0]
    rows_c = batch * seq_c
    xs = x.reshape(batch * seq, d).astype(F32)
    xc = ctx.reshape(rows_c, d).astype(F32)

    cvec = jnp.concatenate([c.astype(F32), c_ctx.astype(F32)[None, :]], axis=0)
    cvec = jnp.pad(cvec, ((0, -cvec.shape[0] % 8), (0, 0)))
    mods_all = _modulation(cvec, mod_w.astype(F32), mod_b.astype(F32))
    w_down_bf = ffn_w_down.astype(BF16)

    for layer in range(depth):
        need_ctx = layer < depth - 1
        mods = mods_all[layer, :batch].reshape(batch, 3 * N_SUB, d)
        modc = mods_all[layer, batch:batch + 1].reshape(1, 3 * N_SUB, d)
        g = norm_g[layer].astype(F32)

        def ffn(t, m, rows, sub, which):
            return _half_ffn(t, m, sub, rows, g[sub], ffn_w_gate, ffn_w_up, w_down_bf, (layer, which))

        xs = ffn(xs, mods, seq, 0, 0)
        xc = ffn(xc, modc, rows_c, 0, 0)
        h = _norm_mod(xs, g[1], mods, 1, seq, BF16)
        hc = _norm_mod(xc, g[1], modc, 1, rows_c, BF16)
        i = layer // 2
        if layer % 2 == 0:
            filt = (hy_w1[i], hy_b1[i], hy_w2[i], hy_b2[i], hy_w3[i], hy_b3[i], hy_freq[i], hy_w_out[i])
            mix, mixc = _ab_mixer(h, hc, batch, seq, seq_c, need_ctx, ab_w_in[i], mla_q_norm_g[i],
                                  mla_kv_norm_g[i], mla_w_uq[i], mla_w_ukv[i], hy_conv_w[i], hy_conv_b[i],
                                  filt, hy_bias[i])
            w_out, w_out_index = ab_w_out, (i,)
        else:
            lambda_init = 0.8 - 0.6 * math.exp(-0.3 * layer)
            mix, mixc = _diff_mixer(h, hc, batch, seq, seq_c, need_ctx, c_w_in[i], c_lambda[i], c_subln_g[i],
                                    lambda_init)
            w_out, w_out_index = c_w_out, (i,)
        out_kw = dict(w_index=w_out_index, tn=TILE_N * jnp.dtype(BF16).itemsize // w_out.dtype.itemsize,
                      name="mixer_out_residual")
        xs = _mm_residual(mix, w_out, xs, mods, 1, 1.0, seq, **out_kw)
        xs = ffn(xs, mods, seq, 2, 1)
        if need_ctx:
            xc = _mm_residual(mixc, w_out, xc, modc, 1, 1.0, rows_c, **out_kw)
            xc = ffn(xc, modc, rows_c, 2, 1)

    out = _norm_mod(xs, final_norm_g.astype(F32), None, 0, seq, x.dtype)
    return out.reshape(batch, seq, d)
```

```python
import functools
import math
from typing import NamedTuple

import numpy as np
import jax
import jax.numpy as jnp
from jax import lax
from jax.experimental import pallas as pl
from jax.experimental.pallas import tpu as pltpu

F32 = jnp.float32
BF16 = jnp.bfloat16

EPS = 1e-6
ROPE_THETA = 10000.0
GRID_W = 64
N_SUB = 3
NOPE_DIM = 128
ROPE_DIM = 64
V_DIM = 128
DIFF_DIM = 128
HY_EMB = 33
HY_TARGET = 1e-2
HY_SHORT_DECAY_PCT = 0.3
HY_LONG_DECAY_PCT = 1.5

LANES = 128
MXU_DIM_V7X = 256
VMEM_BYTES_V7X = 64 * 1024 * 1024
VMEM_COMPILER_RESERVE = 8 * 1024 * 1024
MLA_HEAD_PAD = MXU_DIM_V7X

TILE_M = 1024
TILE_N = 512
TILE_K = 4096
TILE_M_LONG_K = TILE_M // 2
TILE_K_LONG = 2 * TILE_K
ATTN_TQ = 256
ATTN_CHUNK = 256


def _params(semantics, block_bytes, temp_bytes=0):
    want = 2 * block_bytes + temp_bytes
    limit = min(VMEM_BYTES_V7X - VMEM_COMPILER_RESERVE, max(want, 32 * 1024 * 1024))
    return pltpu.CompilerParams(dimension_semantics=semantics, vmem_limit_bytes=int(limit))


def _nbytes(shape, dtype):
    return math.prod(shape) * jnp.dtype(dtype).itemsize


def _rms(x, g):
    return x * lax.rsqrt(jnp.mean(x * x, axis=-1, keepdims=True) + EPS) * g


def _matmul(a, w, *, tm, tn, tk, epilogue, out_shape, out_specs, extras=(), w_index=(), w_cols=None,
            resident_a=False, name):
    m, kdim = a.shape
    assert w.ndim == 2 + len(w_index) and w.shape[-2] == kdim
    col0, n = (0, w.shape[-1]) if w_cols is None else w_cols
    tm, tn, tk = min(tm, m), min(tn, n), min(tk, kdim)
    assert m % tm == 0 and n % tn == 0 and kdim % tk == 0 and col0 % tn == 0, (name, a.shape, w.shape, tm, tn, tk)
    j0 = col0 // tn
    nk = kdim // tk
    n_ex, n_out = len(extras), len(out_shape)

    def body(*refs):
        a_ref, w_ref = refs[0], refs[1]
        ex = refs[2:2 + n_ex]
        outs = refs[2 + n_ex:2 + n_ex + n_out]
        prod = jnp.dot(a_ref[...], w_ref[...].astype(BF16), preferred_element_type=F32)
        if nk == 1:
            epilogue(prod, ex, outs)
            return
        acc_ref = refs[-1]
        k = pl.program_id(2)

        @pl.when(k == 0)
        def _():
            acc_ref[...] = prod

        if nk > 2:
            @pl.when((k > 0) & (k < nk - 1))
            def _():
                acc_ref[...] += prod

        @pl.when(k == nk - 1)
        def _():
            epilogue(acc_ref[...] + prod, ex, outs)

    in_specs = [pl.BlockSpec((tm, tk), lambda i, j, k: (i, k), pipeline_mode=pl.Buffered(1) if resident_a else None),
                pl.BlockSpec((None,) * len(w_index) + (tk, tn), lambda i, j, k: (*w_index, k, j + j0))]
    in_specs += [spec for _, spec in extras]
    block_bytes = _nbytes((tm, tk), a.dtype) + _nbytes((tk, tn), w.dtype)
    for arr, spec in extras:
        block_bytes += _nbytes([d for d in spec.block_shape if d is not None], arr.dtype)
    for sds, spec in zip(out_shape, out_specs):
        block_bytes += _nbytes([d for d in spec.block_shape if d is not None], sds.dtype)
    acc_bytes = _nbytes((tm, tn), F32)
    return pl.pallas_call(
        body,
        grid=(m // tm, n // tn, nk),
        in_specs=in_specs,
        out_specs=list(out_specs),
        out_shape=list(out_shape),
        scratch_shapes=[pltpu.VMEM((tm, tn), F32)] if nk > 1 else [],
        compiler_params=_params(("parallel", "parallel", "arbitrary"), block_bytes, 8 * acc_bytes),
        name=name,
    )(a, w, *[arr for arr, _ in extras])


def _mm_plain(a, w, out_dtype, *, tm=TILE_M, tn=TILE_N, tk=TILE_K, w_cols=None, name):
    m, n = a.shape[0], (w.shape[1] if w_cols is None else w_cols[1])
    tm, tn = min(tm, m), min(tn, n)

    def epilogue(acc, ex, outs):
        outs[0][...] = acc.astype(out_dtype)

    return _matmul(a, w, tm=tm, tn=tn, tk=tk, epilogue=epilogue, w_cols=w_cols,
                   out_shape=[jax.ShapeDtypeStruct((m, n), out_dtype)],
                   out_specs=[pl.BlockSpec((tm, tn), lambda i, j, k: (i, j))], name=name)[0]


def _mm_residual(a, w, res, mods, sub, coef, rows_per_group, *, tm=TILE_M, tn=TILE_N, tk=TILE_K, w_index=(),
                 resident_a=False, name):
    m, n = a.shape[0], w.shape[-1]
    tm, tn = min(tm, m, rows_per_group), min(tn, n)
    tiles_per_group = rows_per_group // tm
    gate_row = 3 * sub + 2

    def epilogue(acc, ex, outs):
        res_ref, mod_ref = ex
        gate = mod_ref[gate_row:gate_row + 1, :]
        outs[0][...] = res_ref[...] + (coef * gate) * acc

    extras = [(res, pl.BlockSpec((tm, tn), lambda i, j, k: (i, j))),
              (mods, pl.BlockSpec((None, 3 * N_SUB, tn), lambda i, j, k: (i // tiles_per_group, 0, j)))]
    return _matmul(a, w, tm=tm, tn=tn, tk=tk, epilogue=epilogue, extras=extras, w_index=w_index,
                   resident_a=resident_a, out_shape=[jax.ShapeDtypeStruct((m, n), F32)],
                   out_specs=[pl.BlockSpec((tm, tn), lambda i, j, k: (i, j))], name=name)[0]


class RopeLayout(NamedTuple):
    group: int
    lo: int
    n_freq: int
    split: bool


MLA_ROPE_LAYOUT = RopeLayout(group=MLA_HEAD_PAD, lo=NOPE_DIM, n_freq=ROPE_DIM // 4, split=False)
DIFF_ROPE_LAYOUT = RopeLayout(group=DIFF_DIM, lo=0, n_freq=DIFF_DIM // 4, split=True)


def _rope_tables(seq, layout):
    n_freq = layout.n_freq
    pos = jnp.arange(seq, dtype=jnp.int32)
    row_pos = (pos // GRID_W).astype(F32)
    col_pos = (pos % GRID_W).astype(F32)
    inv = ROPE_THETA ** (-jnp.arange(n_freq, dtype=F32) / n_freq)
    ang = jnp.concatenate([row_pos[:, None] * inv, col_pos[:, None] * inv], axis=-1)
    cos, sin = jnp.cos(ang), jnp.sin(ang)
    if layout.split:
        assert 4 * n_freq == LANES
        return jnp.concatenate([cos, cos], axis=-1), jnp.concatenate([-sin, sin], axis=-1)
    cos = jnp.repeat(cos, 2, axis=-1)
    sin = jnp.repeat(sin, 2, axis=-1) * jnp.tile(jnp.array([-1.0, 1.0], F32), 2 * n_freq)
    rest = LANES - 4 * n_freq
    return (jnp.concatenate([cos, jnp.ones((seq, rest), F32)], axis=-1),
            jnp.concatenate([sin, jnp.zeros((seq, rest), F32)], axis=-1))


def _rope_apply(x, cos_ref, sin_ref, layout):
    assert layout.lo % LANES == 0 and layout.group % LANES == 0 and x.shape[-1] % layout.group == 0
    cos, sin = cos_ref[...], sin_ref[...]
    slabs = []
    for c0 in range(0, x.shape[-1], LANES):
        slab = x[:, c0:c0 + LANES]
        if c0 % layout.group == layout.lo:
            if layout.split:
                partner = pltpu.roll(slab, LANES // 2, 1)
            else:
                even = (lax.broadcasted_iota(jnp.int32, slab.shape, 1) & 1) == 0
                partner = jnp.where(even, pltpu.roll(slab, LANES - 1, 1), pltpu.roll(slab, 1, 1))
            slab = slab * cos + partner * sin
        slabs.append(slab)
    return jnp.concatenate(slabs, axis=1) if len(slabs) > 1 else slabs[0]


def _cast_deinterleave_columns(w, n_perm):
    kdim, n = w.shape
    perm = np.concatenate([np.arange(0, LANES, 2), np.arange(1, LANES, 2)])
    p = np.zeros((LANES, LANES), np.float32)
    p[perm, np.arange(LANES)] = 1.0
    tm, tn = min(TILE_M, kdim), min(TILE_N, n)
    assert n_perm % tn == 0
    perm_blocks = n_perm // tn

    def body(w_ref, p_ref, o_ref):
        @pl.when(pl.program_id(1) < perm_blocks)
        def _():
            pm = p_ref[...]
            for c0 in range(0, tn, LANES):
                o_ref[:, c0:c0 + LANES] = jnp.dot(w_ref[:, c0:c0 + LANES].astype(BF16), pm,
                                                  preferred_element_type=F32).astype(BF16)

        @pl.when(pl.program_id(1) >= perm_blocks)
        def _():
            o_ref[...] = w_ref[...].astype(BF16)

    return pl.pallas_call(
        body,
        grid=(kdim // tm, n // tn),
        in_specs=[pl.BlockSpec((tm, tn), lambda i, j: (i, j)), pl.BlockSpec((LANES, LANES), lambda i, j: (0, 0))],
        out_specs=pl.BlockSpec((tm, tn), lambda i, j: (i, j)),
        out_shape=jax.ShapeDtypeStruct((kdim, n), BF16),
        compiler_params=_params(("parallel", "parallel"), _nbytes((tm, tn), w.dtype) + _nbytes((tm, tn), BF16)),
        name="cast_deinterleave_rotary_columns",
    )(w, jnp.asarray(p, BF16))


def _mm_rope(a, w, tables, scale, seq, out_dtype, *, layout, tm=TILE_M, tn=TILE_N, tk=TILE_K, w_cols=None, name):
    m, n = a.shape[0], (w.shape[1] if w_cols is None else w_cols[1])
    tm, tn = min(tm, m, seq), min(tn, n)
    tiles_per_seq = seq // tm

    def epilogue(acc, ex, outs):
        y = acc if tables is None else _rope_apply(acc, ex[0], ex[1], layout)
        outs[0][...] = (y * scale).astype(out_dtype)

    extras = []
    if tables is not None:
        spec = pl.BlockSpec((tm, LANES), lambda i, j, k: (i % tiles_per_seq, 0))
        extras = [(tables[0], spec), (tables[1], spec)]
    return _matmul(a, w, tm=tm, tn=tn, tk=tk, epilogue=epilogue, extras=extras, w_cols=w_cols,
                   out_shape=[jax.ShapeDtypeStruct((m, n), out_dtype)],
                   out_specs=[pl.BlockSpec((tm, tn), lambda i, j, k: (i, j))], name=name)[0]


def _modulation(cvec, mod_w, mod_b, *, tn=512):
    depth, d, n = mod_w.shape
    rows = cvec.shape[0]
    tn = min(tn, n)

    def body(c_ref, w_ref, b_ref, o_ref):
        cv = c_ref[...]
        s = cv * (1.0 / (1.0 + jnp.exp(-cv)))
        o_ref[...] = jnp.dot(s, w_ref[...], preferred_element_type=F32) + b_ref[...]

    block_bytes = _nbytes((d, tn), F32) + _nbytes((rows, d), F32) + 2 * _nbytes((rows, tn), F32)
    return pl.pallas_call(
        body,
        grid=(depth, n // tn),
        in_specs=[pl.BlockSpec((rows, d), lambda l, j: (0, 0)),
                  pl.BlockSpec((None, d, tn), lambda l, j: (l, 0, j)),
                  pl.BlockSpec((None, 1, tn), lambda l, j: (l, 0, j))],
        out_specs=pl.BlockSpec((None, rows, tn), lambda l, j: (l, 0, j)),
        out_shape=jax.ShapeDtypeStruct((depth, rows, n), F32),
        compiler_params=_params(("parallel", "parallel"), block_bytes),
        name="adaln_modulation",
    )(cvec, mod_w, mod_b.reshape(depth, 1, n))


def _norm_mod(x, g, mods, sub, rows_per_group, out_dtype, *, tm=256):
    m, d = x.shape
    tm = min(tm, m, rows_per_group)
    tiles_per_group = rows_per_group // tm

    def body(*refs):
        if mods is None:
            x_ref, g_ref, o_ref = refs
        else:
            x_ref, g_ref, mod_ref, o_ref = refs
        y = _rms(x_ref[...], g_ref[...])
        if mods is not None:
            shift = mod_ref[3 * sub:3 * sub + 1, :]
            scale = mod_ref[3 * sub + 1:3 * sub + 2, :]
            y = y * (1.0 + scale) + shift
        o_ref[...] = y.astype(out_dtype)

    in_specs = [pl.BlockSpec((tm, d), lambda i: (i, 0)), pl.BlockSpec((1, d), lambda i: (0, 0))]
    args = [x, g.reshape(1, d)]
    if mods is not None:
        in_specs.append(pl.BlockSpec((None, 3 * N_SUB, d), lambda i: (i // tiles_per_group, 0, 0)))
        args.append(mods)
    block_bytes = _nbytes((tm, d), F32) + _nbytes((tm, d), out_dtype) + _nbytes((16, d), F32)
    return pl.pallas_call(
        body,
        grid=(m // tm,),
        in_specs=in_specs,
        out_specs=pl.BlockSpec((tm, d), lambda i: (i, 0)),
        out_shape=jax.ShapeDtypeStruct((m, d), out_dtype),
        compiler_params=_params(("parallel",), block_bytes, 2 * _nbytes((tm, d), F32)),
        name="rmsnorm_modulate",
    )(*args)


def _gate_up(h, w_gate, w_up, w_index, *, tm=2 * TILE_M, tn=TILE_N, tk=TILE_K):
    m, kdim = h.shape
    n = w_gate.shape[-1]
    tn = tn * jnp.dtype(BF16).itemsize // jnp.dtype(w_gate.dtype).itemsize
    tm, tn, tk = min(tm, m), min(tn, n), min(tk, kdim)
    assert m % tm == 0 and n % tn == 0 and kdim % tk == 0
    nk = kdim // tk

    def finish(g, u, o_ref):
        o_ref[...] = (g * (1.0 / (1.0 + jnp.exp(-g))) * u).astype(o_ref.dtype)

    def body(h_ref, wg_ref, wu_ref, o_ref, *acc):
        hh = h_ref[...]
        pg = jnp.dot(hh, wg_ref[...].astype(BF16), preferred_element_type=F32)
        pu = jnp.dot(hh, wu_ref[...].astype(BF16), preferred_element_type=F32)
        if nk == 1:
            finish(pg, pu, o_ref)
            return
        accg, accu = acc
        k = pl.program_id(2)

        @pl.when(k == 0)
        def _():
            accg[...] = pg
            accu[...] = pu

        if nk > 2:
            @pl.when((k > 0) & (k < nk - 1))
            def _():
                accg[...] += pg
                accu[...] += pu

        @pl.when(k == nk - 1)
        def _():
            finish(accg[...] + pg, accu[...] + pu, o_ref)

    block_bytes = _nbytes((tm, tk), BF16) + 2 * _nbytes((tk, tn), w_gate.dtype) + _nbytes((tm, tn), BF16)
    acc_bytes = _nbytes((tm, tn), F32) + _nbytes((tk, tn), BF16)
    w_spec = pl.BlockSpec((None,) * len(w_index) + (tk, tn), lambda i, j, k: (*w_index, k, j))
    return pl.pallas_call(
        body,
        grid=(m // tm, n // tn, nk),
        in_specs=[pl.BlockSpec((tm, tk), lambda i, j, k: (i, k), pipeline_mode=pl.Buffered(1)), w_spec, w_spec],
        out_specs=pl.BlockSpec((tm, tn), lambda i, j, k: (i, j)),
        out_shape=jax.ShapeDtypeStruct((m, n), BF16),
        scratch_shapes=[pltpu.VMEM((tm, tn), F32)] * 2 if nk > 1 else [],
        compiler_params=_params(("parallel", "parallel", "arbitrary"), block_bytes, 6 * acc_bytes),
        name="ffn_gate_up",
    )(h, w_gate, w_up)


def _half_ffn(x, mods, sub, rows_per_group, g, w_gate, w_up, w_down, w_index):
    h = _norm_mod(x, g, mods, sub, rows_per_group, BF16)
    u = _gate_up(h, w_gate, w_up, w_index)
    return _mm_residual(u, w_down, x, mods, sub, 0.5, rows_per_group, w_index=w_index,
                        tm=TILE_M, tk=TILE_K_LONG, resident_a=True, name="ffn_down_residual")


class ScoreSet(NamedTuple):
    q: object
    k_cols: slice
    write: object
    read: object


def _lane_fold(x, op):
    return functools.reduce(op, [x[:, c0:c0 + LANES] for c0 in range(0, x.shape[-1], LANES)])


def _attention_step(sets, k_refs, v_refs, chunk):
    state = [dict(mx=None, ls=None, acc=None, m_old=s.read[1][...]) for s in sets]
    off = 0
    for k_ref, v_ref in zip(k_refs, v_refs):
        total = k_ref.shape[0]
        for c0 in range(0, total, chunk):
            rows = min(chunk, total - c0)
            cols = slice(off + c0, off + c0 + rows)
            for s, st in zip(sets, state):
                if s.write is not None:
                    new = lax.dot_general(s.q, k_ref[c0:c0 + rows, s.k_cols], (((1,), (1,)), ((), ())),
                                          preferred_element_type=F32)
                    s.write[0][:, cols] = new
                    part = _lane_fold(new, jnp.maximum)
                    st["mx"] = part if st["mx"] is None else jnp.maximum(st["mx"], part)
            for s, st in zip(sets, state):
                m_old = jnp.concatenate([st["m_old"]] * (rows // LANES), axis=1) if rows > LANES else st["m_old"]
                e = jnp.exp(s.read[0][:, cols] - m_old)
                part = _lane_fold(e, jnp.add)
                st["ls"] = part if st["ls"] is None else st["ls"] + part
                pv = jnp.dot(e.astype(BF16), v_ref[c0:c0 + rows, :], preferred_element_type=F32)
                st["acc"] = pv if st["acc"] is None else st["acc"] + pv
        off += total
    out = []
    for s, st in zip(sets, state):
        if s.write is not None:
            s.write[1][...] = jnp.broadcast_to(jnp.max(st["mx"], axis=-1, keepdims=True), st["mx"].shape)
        out.append((st["acc"], jnp.sum(st["ls"], axis=-1, keepdims=True)))
    return out


def _skewed_steps(nq, score_bufs, emit):
    i = pl.program_id(2)
    buf_a, buf_b = score_bufs

    @pl.when(i == 0)
    def _():
        for pair in buf_b:
            for ref in pair:
                ref[...] = jnp.zeros_like(ref)

    even = lax.rem(i, 2) == 0

    @pl.when((i < nq) & even)
    def _():
        emit(buf_a, buf_b)

    @pl.when((i < nq) & jnp.logical_not(even))
    def _():
        emit(buf_b, buf_a)

    @pl.when(i == nq)
    def _():
        emit(None, buf_a if (nq - 1) % 2 == 0 else buf_b)


def _skewed_maps(nq):
    return (lambda bb, h, i: (bb, jnp.minimum(i, nq - 1), h)), (lambda bb, h, i: (bb, jnp.maximum(i - 1, 0), h))


def _kv_specs(arrays, width):
    return [pl.BlockSpec((None, a.shape[1], width), lambda bb, h, i: (bb, 0, h)) for a in arrays]


def _mla_attention(q, ks, vs, heads, *, tq=2 * ATTN_TQ):
    b, sq, _ = q.shape
    sk = sum(k.shape[1] for k in ks)
    tq = min(tq, sq)
    nkv = len(ks)

    nq = sq // tq
    q_map, o_map = _skewed_maps(nq)

    def body(q_ref, *refs):
        k_refs, v_refs = refs[:nkv], refs[nkv:2 * nkv]
        o_ref = refs[2 * nkv]
        s_a, m_a, s_b, m_b = refs[2 * nkv + 1:]

        def emit(write, read):
            sets = [ScoreSet(q_ref[...], slice(None), None if write is None else write[0], read[0])]
            (o, l), = _attention_step(sets, k_refs, v_refs, ATTN_CHUNK)
            o_ref[...] = (o / l).astype(o_ref.dtype)

        _skewed_steps(nq, ([(s_a, m_a)], [(s_b, m_b)]), emit)

    block_bytes = (_nbytes((tq, MLA_HEAD_PAD), BF16) + _nbytes((sk, MLA_HEAD_PAD), BF16)
                   + _nbytes((sk, V_DIM), BF16) + _nbytes((tq, V_DIM), BF16))
    score_bytes = _nbytes((tq, sk), F32)
    return pl.pallas_call(
        body,
        grid=(b, heads, nq + 1),
        in_specs=([pl.BlockSpec((None, tq, MLA_HEAD_PAD), q_map)]
                  + _kv_specs(ks, MLA_HEAD_PAD) + _kv_specs(vs, V_DIM)),
        out_specs=pl.BlockSpec((None, tq, V_DIM), o_map),
        out_shape=jax.ShapeDtypeStruct((b, sq, heads * V_DIM), BF16),
        scratch_shapes=[pltpu.VMEM((tq, sk), F32), pltpu.VMEM((tq, LANES), F32)] * 2,
        compiler_params=_params(("parallel", "parallel", "arbitrary"), block_bytes, 4 * score_bytes),
        name="mla_attention",
    )(q, *ks, *vs)


def _diff_attention(q, ks, vs, lam_p, subln_g, lambda_init, heads, *, tq=ATTN_TQ):
    b, sq, _ = q.shape
    sk = sum(k.shape[1] for k in ks)
    tq = min(tq, sq)
    hw = 2 * DIFF_DIM
    nkv = len(ks)
    nq = sq // tq
    q_map, o_map = _skewed_maps(nq)

    def body(lam_ref, g_ref, q_ref, *refs):
        k_refs, v_refs = refs[:nkv], refs[nkv:2 * nkv]
        o_ref = refs[2 * nkv]
        bufs = refs[2 * nkv + 1:]
        set_a = [(bufs[0], bufs[1]), (bufs[2], bufs[3])]
        set_b = [(bufs[4], bufs[5]), (bufs[6], bufs[7])]

        def emit(write, read):
            halves = [(q_ref[:, :DIFF_DIM], slice(0, DIFF_DIM)), (q_ref[:, DIFF_DIM:], slice(DIFF_DIM, hw))]
            sets = [ScoreSet(qh, cols, None if write is None else write[n], read[n])
                    for n, (qh, cols) in enumerate(halves)]
            (o1, l1), (o2, l2) = _attention_step(sets, k_refs, v_refs, ATTN_CHUNK)
            lp = lam_ref[...]
            lam = (jnp.exp(jnp.sum(lp[0:1] * lp[1:2], axis=-1, keepdims=True))
                   - jnp.exp(jnp.sum(lp[2:3] * lp[3:4], axis=-1, keepdims=True)) + lambda_init)
            o = o1 / l1 - (lam / l2) * o2
            o_ref[...] = (_rms(o, g_ref[...]) * (1.0 - lambda_init)).astype(o_ref.dtype)

        _skewed_steps(nq, (set_a, set_b), emit)

    block_bytes = 2 * _nbytes((tq, hw), BF16) + 2 * _nbytes((sk, hw), BF16)
    score_bytes = _nbytes((tq, sk), F32)
    return pl.pallas_call(
        body,
        grid=(b, heads, nq + 1),
        in_specs=([pl.BlockSpec((4, DIFF_DIM), lambda bb, h, i: (0, 0)),
                   pl.BlockSpec((1, hw), lambda bb, h, i: (0, 0)),
                   pl.BlockSpec((None, tq, hw), q_map)]
                  + _kv_specs(ks, hw) + _kv_specs(vs, hw)),
        out_specs=pl.BlockSpec((None, tq, hw), o_map),
        out_shape=jax.ShapeDtypeStruct((b, sq, heads * hw), BF16),
        scratch_shapes=[pltpu.VMEM((tq, sk), F32), pltpu.VMEM((tq, LANES), F32)] * 4,
        compiler_params=_params(("parallel", "parallel", "arbitrary"), block_bytes, 6 * score_bytes),
        name="diff_attention",
    )(lam_p.astype(F32), subln_g.reshape(1, hw).astype(F32), q, *ks, *vs)


def _hyena_filters(seq, w1, b1, w2, b2, w3, b3, freq, w_out, *, tl=256, tn=4096):
    hid = w1.shape[1]
    width2 = w_out.shape[1]
    width = width2 // 2
    bands = (HY_EMB - 1) // 2
    t = jnp.linspace(0.0, 1.0, seq, dtype=F32)[:, None]
    ang = ((2.0 * math.pi / seq) * jnp.arange(seq, dtype=F32)[:, None]
           * jnp.linspace(1e-4, bands - 1, bands, dtype=F32)[None, :])
    z = jnp.concatenate([t, jnp.cos(ang), -jnp.sin(ang), jnp.zeros((seq, LANES - HY_EMB), F32)], axis=-1)
    deltas = jnp.abs(jnp.linspace(math.log(HY_TARGET) / HY_LONG_DECAY_PCT,
                                  math.log(HY_TARGET) / HY_SHORT_DECAY_PCT, width, dtype=F32))
    deltas2 = jnp.concatenate([deltas, deltas])[None, :]

    def pad2(w):
        return jnp.pad(w.astype(F32), ((0, LANES - w.shape[0]), (0, LANES - w.shape[1])))

    def pad_row(v):
        return jnp.pad(v.astype(F32), (0, LANES - v.shape[0]))[None, :]

    w_out_p = jnp.pad(w_out.astype(F32), ((0, LANES - hid), (0, 0)))
    tl, tn = min(tl, seq), min(tn, width2)
    exact = lax.Precision.HIGHEST

    def body(z_ref, w1_ref, b1_ref, w2_ref, b2_ref, w3_ref, b3_ref, f_ref, wo_ref, d_ref, o_ref):
        zz, f = z_ref[...], f_ref[...]
        a = jnp.sin(f * (jnp.dot(zz, w1_ref[...], precision=exact, preferred_element_type=F32) + b1_ref[...]))
        a = jnp.sin(f * (jnp.dot(a, w2_ref[...], precision=exact, preferred_element_type=F32) + b2_ref[...]))
        a = jnp.sin(f * (jnp.dot(a, w3_ref[...], precision=exact, preferred_element_type=F32) + b3_ref[...]))
        h = jnp.dot(a, wo_ref[...], precision=exact, preferred_element_type=F32)
        h = h * jnp.exp(-zz[:, 0:1] * d_ref[...])
        row = pl.program_id(0) * tl + lax.broadcasted_iota(jnp.int32, h.shape, 0)
        col = pl.program_id(1) * tn + lax.broadcasted_iota(jnp.int32, h.shape, 1)
        o_ref[...] = jnp.where((row == 0) & (col >= width), 0.0, h).astype(o_ref.dtype)

    sq = pl.BlockSpec((LANES, LANES), lambda i, j: (0, 0))
    vec = pl.BlockSpec((1, LANES), lambda i, j: (0, 0))
    block_bytes = _nbytes((tl, LANES), F32) + _nbytes((LANES, tn), F32) + 2 * _nbytes((tl, tn), F32)
    return pl.pallas_call(
        body,
        grid=(seq // tl, width2 // tn),
        in_specs=[pl.BlockSpec((tl, LANES), lambda i, j: (i, 0)), sq, vec, sq, vec, sq, vec, vec,
                  pl.BlockSpec((LANES, tn), lambda i, j: (0, j)),
                  pl.BlockSpec((1, tn), lambda i, j: (0, j))],
        out_specs=pl.BlockSpec((tl, tn), lambda i, j: (i, j)),
        out_shape=jax.ShapeDtypeStruct((seq, width2), BF16),
        compiler_params=_params(("parallel", "parallel"), block_bytes, 4 * _nbytes((tl, tn), F32)),
        name="hyena_filters",
    )(z, pad2(w1), pad_row(b1), pad2(w2), pad_row(b2), pad2(w3), pad_row(b3), pad_row(freq), w_out_p, deltas2)


def _hyena_prep(zh, conv_w, conv_b, *, tc=128):
    b, seq, w3 = zh.shape
    width = w3 // 3
    tc = min(tc, width)
    nw = width // tc

    def body(z0_ref, z1_ref, z2_ref, w0_ref, w1_ref, w2_ref, b0_ref, b1_ref, b2_ref, x0_ref, vx_ref):
        row = lax.broadcasted_iota(jnp.int32, (seq, 1), 0)

        def conv(z_ref, w_ref, b_ref):
            z = z_ref[...]
            prev = jnp.where(row == 0, 0.0, pltpu.roll(z, 1, 0))
            nxt = jnp.where(row == seq - 1, 0.0, pltpu.roll(z, seq - 1, 0))
            w = w_ref[...]
            return prev * w[0:1] + z * w[1:2] + nxt * w[2:3] + b_ref[...]

        x0 = conv(z0_ref, w0_ref, b0_ref)
        x1 = conv(z1_ref, w1_ref, b1_ref)
        v = conv(z2_ref, w2_ref, b2_ref)
        x0_ref[...] = x0
        vx_ref[...] = (v * x1).astype(vx_ref.dtype)

    def zspec(part):
        return pl.BlockSpec((None, seq, tc), lambda bb, j: (bb, 0, part * nw + j))

    def wspec(rows, part):
        return pl.BlockSpec((rows, tc), lambda bb, j: (0, part * nw + j))

    out_spec = pl.BlockSpec((seq, tc), lambda bb, j: (0, bb * nw + j))
    block_bytes = 4 * _nbytes((seq, tc), F32) + _nbytes((seq, tc), BF16)
    return pl.pallas_call(
        body,
        grid=(b, nw),
        in_specs=[zspec(0), zspec(1), zspec(2), wspec(3, 0), wspec(3, 1), wspec(3, 2),
                  wspec(1, 0), wspec(1, 1), wspec(1, 2)],
        out_specs=[out_spec, out_spec],
        out_shape=[jax.ShapeDtypeStruct((seq, b * width), F32), jax.ShapeDtypeStruct((seq, b * width), BF16)],
        compiler_params=_params(("parallel", "parallel"), block_bytes, 8 * _nbytes((seq, tc), F32)),
        name="hyena_short_conv",
    )(zh, zh, zh, conv_w, conv_w, conv_w, conv_b[None, :], conv_b[None, :], conv_b[None, :])


def _dft_matrices(seq):
    n = 2 * seq
    assert n % 4 == 0
    kb = math.gcd(seq, 64)
    s = jnp.arange(seq, dtype=jnp.int32)

    def cos_sin(k):
        phase = _mod_nonneg(k[:, None] * s[None, :], n)
        return _cos_turns(phase, n), _cos_turns(_mod_nonneg(phase + (n - n // 4), n), n)

    c_hi, s_hi = cos_sin(jnp.arange(seq // kb, dtype=jnp.int32) * kb)
    c_lo, s_lo = cos_sin(jnp.arange(kb, dtype=jnp.int32))
    cos = (c_hi[:, None, :] * c_lo[None] - s_hi[:, None, :] * s_lo[None]).reshape(seq, seq)
    sin = (s_hi[:, None, :] * c_lo[None] + c_hi[:, None, :] * s_lo[None]).reshape(seq, seq)
    nyquist = jnp.where((s & 1) == 0, 1.0, -1.0).astype(F32)
    weight = jnp.where(s == 0, 1.0 / n, 2.0 / n).astype(F32)
    fwd = jnp.concatenate([cos, jnp.where(s[:, None] == 0, nyquist[None, :], sin)], axis=0)
    inv = jnp.concatenate([cos * weight[None, :],
                           jnp.where(s[None, :] == 0, nyquist[:, None], sin) * weight[None, :]], axis=1)
    return fwd.astype(BF16), inv.astype(BF16)


def _div_nonneg(x, d):
    return x >> (d.bit_length() - 1) if d & (d - 1) == 0 else x // d


def _mod_nonneg(x, d):
    return x & (d - 1) if d & (d - 1) == 0 else x % d


def _cos_turns(phase, n):
    quarter = n // 4
    quad = _div_nonneg(phase, quarter)
    rem = phase - quad * quarter
    odd = (quad & 1) == 1
    x = jnp.where(odd, quarter - rem, rem).astype(F32) * (2.0 * math.pi / n)
    x2 = x * x
    acc = jnp.full_like(x2, 1.0 / math.factorial(16))
    for order in range(14, -1, -2):
        acc = acc * (-x2) + 1.0 / math.factorial(order)
    return jnp.where((quad == 1) | (quad == 2), -acc, acc)


def _spectrum_multiply(spec, batch, width, *, tr=512, tc=512):
    n = spec.shape[0]
    half = n // 2
    tr, tc = min(tr, half), min(tc, width)
    nwc = width // tc
    s3 = spec.reshape(2, half, spec.shape[1])

    def body(u_ref, f_ref, b_ref, y_ref):
        ua, ub = u_ref[0], u_ref[1]
        fa, fb = f_ref[0], f_ref[1]
        ba, bb_ = b_ref[0], b_ref[1]
        ga = fa + ba
        gb = fb - bb_
        row = pl.program_id(0) * tr + lax.broadcasted_iota(jnp.int32, ua.shape, 0)
        dc = row == 0
        ya = ua * ga - jnp.where(dc, 0.0, ub * gb)
        yb = jnp.where(dc, ub * (fb + bb_), ua * gb + ub * ga)
        y_ref[0] = ya.astype(y_ref.dtype)
        y_ref[1] = yb.astype(y_ref.dtype)

    block_bytes = 3 * _nbytes((2, tr, tc), F32) + _nbytes((2, tr, tc), BF16)
    y = pl.pallas_call(
        body,
        grid=(half // tr, nwc, batch),
        in_specs=[pl.BlockSpec((2, tr, tc), lambda i, j, bb: (0, i, bb * nwc + j)),
                  pl.BlockSpec((2, tr, tc), lambda i, j, bb: (0, i, batch * nwc + j)),
                  pl.BlockSpec((2, tr, tc), lambda i, j, bb: (0, i, (batch + 1) * nwc + j))],
        out_specs=pl.BlockSpec((2, tr, tc), lambda i, j, bb: (0, i, bb * nwc + j)),
        out_shape=jax.ShapeDtypeStruct((2, half, batch * width), BF16),
        compiler_params=_params(("parallel", "parallel", "arbitrary"), block_bytes, 8 * _nbytes((tr, tc), F32)),
        name="hyena_spectrum_multiply",
    )(s3, s3, s3)
    return y.reshape(n, batch * width)


def _hyena_branch(zh, conv_w, conv_b, filt, hy_bias):
    b, seq, w3 = zh.shape
    width = w3 // 3
    x0, vx = _hyena_prep(zh, conv_w.astype(F32), conv_b.astype(F32))
    hfilt = _hyena_filters(seq, *filt)
    fwd, inv = _dft_matrices(seq)
    spec = _mm_plain(fwd, jnp.concatenate([vx, hfilt], axis=1), F32, name="hyena_dft_forward")
    y_hat = _spectrum_multiply(spec, b, width)

    tm, tn = min(TILE_M_LONG_K, seq), min(TILE_N, width)
    nw = width // tn

    def epilogue(acc, ex, outs):
        x0_ref, vx_ref, bias_ref = ex
        outs[0][...] = (x0_ref[...] * (acc + bias_ref[...] * vx_ref[...].astype(F32))).astype(BF16)

    extras = [(x0, pl.BlockSpec((tm, tn), lambda i, j, k: (i, j))),
              (vx, pl.BlockSpec((tm, tn), lambda i, j, k: (i, j))),
              (hy_bias.astype(F32)[None, :], pl.BlockSpec((1, tn), lambda i, j, k: (0, j % nw)))]
    return _matmul(inv, y_hat, tm=tm, tn=tn, tk=TILE_K_LONG, epilogue=epilogue, extras=extras,
                   out_shape=[jax.ShapeDtypeStruct((b, seq, width), BF16)],
                   out_specs=[pl.BlockSpec((None, tm, tn), lambda i, j, k: (j // nw, i, j % nw))],
                   name="hyena_dft_inverse")[0]


def _ab_latents(h, w1, q_norm_g, kv_norm_g, tables, seq, q_lora, kv_lora, *, tm=512, tk=1024):
    m = h.shape[0]
    n1 = w1.shape[1]
    tm = min(tm, m, seq)
    tiles_per_seq = seq // tm

    def epilogue(acc, ex, outs):
        outs[0][...] = _rms(acc[:, :q_lora], ex[0][...]).astype(BF16)
        outs[1][...] = _rms(acc[:, q_lora:q_lora + kv_lora], ex[1][...]).astype(BF16)
        kr = acc[:, q_lora + kv_lora:]
        if tables is not None:
            kr = _rope_apply(kr, ex[2], ex[3], MLA_ROPE_LAYOUT)
        outs[2][...] = kr

    extras = [(q_norm_g.astype(F32)[None, :], pl.BlockSpec((1, q_lora), lambda i, j, k: (0, 0))),
              (kv_norm_g.astype(F32)[None, :], pl.BlockSpec((1, kv_lora), lambda i, j, k: (0, 0)))]
    if tables is not None:
        spec = pl.BlockSpec((tm, LANES), lambda i, j, k: (i % tiles_per_seq, 0))
        extras += [(tables[0], spec), (tables[1], spec)]
    widths = (q_lora, kv_lora, MLA_HEAD_PAD)
    dtypes = (BF16, BF16, F32)
    return _matmul(h, w1, tm=tm, tn=n1, tk=tk, epilogue=epilogue, extras=extras,
                   out_shape=[jax.ShapeDtypeStruct((m, wd), dt) for wd, dt in zip(widths, dtypes)],
                   out_specs=[pl.BlockSpec((tm, wd), lambda i, j, k: (i, 0)) for wd in widths],
                   name="mla_latent_projection")


def _mla_keys(ckv, w_uk, kr, heads, *, tm=1024, tn=1024):
    m = ckv.shape[0]
    n = heads * MLA_HEAD_PAD
    tm, tn = min(tm, m), min(tn, n)
    reps = tn // MLA_HEAD_PAD

    def epilogue(acc, ex, outs):
        rot = ex[0][...]
        if reps > 1:
            rot = jnp.concatenate([rot] * reps, axis=1)
        outs[0][...] = (acc + rot).astype(BF16)

    extras = [(kr, pl.BlockSpec((tm, MLA_HEAD_PAD), lambda i, j, k: (i, 0)))]
    return _matmul(ckv, w_uk, tm=tm, tn=tn, tk=ckv.shape[1], epilogue=epilogue, extras=extras,
                   out_shape=[jax.ShapeDtypeStruct((m, n), BF16)],
                   out_specs=[pl.BlockSpec((tm, tn), lambda i, j, k: (i, j))], name="mla_key_up")[0]


def _ab_mixer(h, hc, batch, seq, seq_c, need_ctx, w_in, q_norm_g, kv_norm_g, w_uq, w_ukv,
              conv_w, conv_b, filt, hy_bias):
    d = w_in.shape[0]
    heads = d // (2 * V_DIM)
    q_lora, kv_lora = w_uq.shape[0], w_ukv.shape[0]
    kv_end = q_lora + kv_lora + ROPE_DIM
    mla_scale = (NOPE_DIM + ROPE_DIM) ** -0.5
    zero_pad = MLA_HEAD_PAD - NOPE_DIM - ROPE_DIM

    w1 = jnp.concatenate([w_in[:, :q_lora + kv_lora], jnp.zeros((d, NOPE_DIM), w_in.dtype),
                          w_in[:, q_lora + kv_lora:kv_end], jnp.zeros((d, zero_pad), w_in.dtype)],
                         axis=1).astype(BF16)
    w_hy = w_in[:, kv_end:].astype(BF16)
    uq = w_uq.reshape(q_lora, heads, NOPE_DIM + ROPE_DIM)
    w_uq_p = jnp.pad(uq, ((0, 0), (0, 0), (0, zero_pad))).reshape(q_lora, heads * MLA_HEAD_PAD).astype(BF16)
    ukv = w_ukv.reshape(kv_lora, heads, NOPE_DIM + V_DIM)
    w_uk_p = jnp.pad(ukv[..., :NOPE_DIM], ((0, 0), (0, 0), (0, MLA_HEAD_PAD - NOPE_DIM))
                     ).reshape(kv_lora, heads * MLA_HEAD_PAD).astype(BF16)
    w_uv = ukv[..., NOPE_DIM:].reshape(kv_lora, heads * V_DIM).astype(BF16)

    tables = _rope_tables(seq, MLA_ROPE_LAYOUT)
    rope_kw = dict(layout=MLA_ROPE_LAYOUT)

    cq, ckv, kr = _ab_latents(h, w1, q_norm_g, kv_norm_g, tables, seq, q_lora, kv_lora)
    cqc, ckvc, krc = _ab_latents(hc, w1, q_norm_g, kv_norm_g, None, seq_c, q_lora, kv_lora)
    q = _mm_rope(cq, w_uq_p, tables, mla_scale, seq, BF16, name="mla_query_up", **rope_kw)
    k = _mla_keys(ckv, w_uk_p, kr, heads)
    v = _mm_plain(ckv, w_uv, BF16, name="mla_value_up")
    kc = _mla_keys(ckvc, w_uk_p, krc, heads)
    vc = _mm_plain(ckvc, w_uv, BF16, name="mla_value_up")

    def split(t, s):
        return t.reshape(batch, s, t.shape[-1])

    att = _mla_attention(split(q, seq), [split(kc, seq_c), split(k, seq)], [split(vc, seq_c), split(v, seq)], heads)
    zh = _mm_plain(h, w_hy, F32, tn=512, name="hyena_in_projection")
    hy = _hyena_branch(split(zh, seq), conv_w, conv_b, filt, hy_bias)
    mix = jnp.concatenate([att, hy], axis=-1).reshape(batch * seq, -1)
    if not need_ctx:
        return mix, None
    qc = _mm_rope(cqc, w_uq_p, None, mla_scale, seq_c, BF16, name="mla_query_up", **rope_kw)
    attc = _mla_attention(split(qc, seq_c), [split(kc, seq_c)], [split(vc, seq_c)], heads)
    zhc = _mm_plain(hc, w_hy, F32, tn=512, name="hyena_in_projection")
    hyc = _hyena_branch(split(zhc, seq_c), conv_w, conv_b, filt, hy_bias)
    return mix, jnp.concatenate([attc, hyc], axis=-1).reshape(batch * seq_c, -1)


def _diff_mixer(h, hc, batch, seq, seq_c, need_ctx, w_in, lam_p, subln_g, lambda_init):
    d = w_in.shape[0]
    hd = w_in.shape[1] // 3
    heads = hd // (2 * DIFF_DIM)
    scale = DIFF_DIM ** -0.5
    w = _cast_deinterleave_columns(w_in, 2 * hd)
    q_cols, k_cols, v_cols = (0, hd), (hd, hd), (2 * hd, hd)
    tables = _rope_tables(seq, DIFF_ROPE_LAYOUT)
    rope_kw = dict(layout=DIFF_ROPE_LAYOUT)

    q = _mm_rope(h, w, tables, scale, seq, BF16, w_cols=q_cols, name="diff_query_projection", **rope_kw)
    k = _mm_rope(h, w, tables, 1.0, seq, BF16, w_cols=k_cols, name="diff_key_projection", **rope_kw)
    v = _mm_plain(h, w, BF16, w_cols=v_cols, name="diff_value_projection")
    kc = _mm_rope(hc, w, None, 1.0, seq_c, BF16, w_cols=k_cols, name="diff_key_projection", **rope_kw)
    vc = _mm_plain(hc, w, BF16, w_cols=v_cols, name="diff_value_projection")

    def split(t, s):
        return t.reshape(batch, s, t.shape[-1])

    o = _diff_attention(split(q, seq), [split(kc, seq_c), split(k, seq)], [split(vc, seq_c), split(v, seq)],
                        lam_p, subln_g, lambda_init, heads)
    mix = o.reshape(batch * seq, hd)
    if not need_ctx:
        return mix, None
    qc = _mm_rope(hc, w, None, scale, seq_c, BF16, w_cols=q_cols, name="diff_query_projection", **rope_kw)
    oc = _diff_attention(split(qc, seq_c), [split(kc, seq_c)], [split(vc, seq_c)], lam_p, subln_g, lambda_init, heads)
    return mix, oc.reshape(batch * seq_c, hd)


def kernel(x, c, ctx, c_ctx, mod_w, mod_b, norm_g, ffn_w_gate, ffn_w_up, ffn_w_down, ab_w_in, mla_q_norm_g, mla_kv_norm_g, mla_w_uq, mla_w_ukv, hy_conv_w, hy_conv_b, hy_w1, hy_b1, hy_w2, hy_b2, hy_w3, hy_b3, hy_freq, hy_w_out, hy_bias, ab_w_out, c_w_in, c_lambda, c_subln_g, c_w_out, final_norm_g):
    batch, seq, d = x.shape
    seq_c = ctx.shape[1]
    depth = mod_w.shape[0]
    rows_c = batch * seq_c
    xs = x.reshape(batch * seq, d).astype(F32)
    xc = ctx.reshape(rows_c, d).astype(F32)

    cvec = jnp.concatenate([c.astype(F32), c_ctx.astype(F32)[None, :]], axis=0)
    cvec = jnp.pad(cvec, ((0, -cvec.shape[0] % 8), (0, 0)))
    mods_all = _modulation(cvec, mod_w.astype(F32), mod_b.astype(F32))
    w_down_bf = ffn_w_down.astype(BF16)

    for layer in range(depth):
        need_ctx = layer < depth - 1
        mods = mods_all[layer, :batch].reshape(batch, 3 * N_SUB, d)
        modc = mods_all[layer, batch:batch + 1].reshape(1, 3 * N_SUB, d)
        g = norm_g[layer].astype(F32)

        def ffn(t, m, rows, sub, which):
            return _half_ffn(t, m, sub, rows, g[sub], ffn_w_gate, ffn_w_up, w_down_bf, (layer, which))

        xs = ffn(xs, mods, seq, 0, 0)
        xc = ffn(xc, modc, rows_c, 0, 0)
        h = _norm_mod(xs, g[1], mods, 1, seq, BF16)
        hc = _norm_mod(xc, g[1], modc, 1, rows_c, BF16)
        i = layer // 2
        if layer % 2 == 0:
            filt = (hy_w1[i], hy_b1[i], hy_w2[i], hy_b2[i], hy_w3[i], hy_b3[i], hy_freq[i], hy_w_out[i])
            mix, mixc = _ab_mixer(h, hc, batch, seq, seq_c, need_ctx, ab_w_in[i], mla_q_norm_g[i],
                                  mla_kv_norm_g[i], mla_w_uq[i], mla_w_ukv[i], hy_conv_w[i], hy_conv_b[i],
                                  filt, hy_bias[i])
            w_out = ab_w_out[i].astype(BF16)
        else:
            lambda_init = 0.8 - 0.6 * math.exp(-0.3 * layer)
            mix, mixc = _diff_mixer(h, hc, batch, seq, seq_c, need_ctx, c_w_in[i], c_lambda[i], c_subln_g[i],
                                    lambda_init)
            w_out = c_w_out[i].astype(BF16)
        xs = _mm_residual(mix, w_out, xs, mods, 1, 1.0, seq, name="mixer_out_residual")
        xs = ffn(xs, mods, seq, 2, 1)
        if need_ctx:
            xc = _mm_residual(mixc, w_out, xc, modc, 1, 1.0, rows_c, name="mixer_out_residual")
            xc = ffn(xc, modc, rows_c, 2, 1)

    out = _norm_mod(xs, final_norm_g.astype(F32), None, 0, seq, x.dtype)
    return out.reshape(batch, seq, d)
```

```python
import functools
import math
from typing import NamedTuple

import numpy as np
import jax
import jax.numpy as jnp
from jax import lax
from jax.experimental import pallas as pl
from jax.experimental.pallas import tpu as pltpu

F32 = jnp.float32
BF16 = jnp.bfloat16

EPS = 1e-6
ROPE_THETA = 10000.0
GRID_W = 64
N_SUB = 3
NOPE_DIM = 128
ROPE_DIM = 64
V_DIM = 128
DIFF_DIM = 128
HY_EMB = 33
HY_TARGET = 1e-2
HY_SHORT_DECAY_PCT = 0.3
HY_LONG_DECAY_PCT = 1.5

LANES = 128
MXU_DIM_V7X = 256
VMEM_BYTES_V7X = 64 * 1024 * 1024
VMEM_COMPILER_RESERVE = 8 * 1024 * 1024
MLA_HEAD_PAD = MXU_DIM_V7X

TILE_M = 1024
TILE_N = 512
TILE_K = 4096
TILE_M_LONG_K = TILE_M // 2
TILE_K_LONG = 2 * TILE_K
ATTN_TQ = 256
ATTN_CHUNK = 256


def _params(semantics, block_bytes, temp_bytes=0):
    want = 2 * block_bytes + temp_bytes
    limit = min(VMEM_BYTES_V7X - VMEM_COMPILER_RESERVE, max(want, 32 * 1024 * 1024))
    return pltpu.CompilerParams(dimension_semantics=semantics, vmem_limit_bytes=int(limit))


def _nbytes(shape, dtype):
    return math.prod(shape) * jnp.dtype(dtype).itemsize


def _rms(x, g):
    return x * lax.rsqrt(jnp.mean(x * x, axis=-1, keepdims=True) + EPS) * g


def _matmul(a, w, *, tm, tn, tk, epilogue, out_shape, out_specs, extras=(), w_index=(), w_cols=None,
            w_col_shift=None, resident_a=False, name):
    m, kdim = a.shape
    assert w.ndim == 2 + len(w_index) and w.shape[-2] == kdim
    col0, n = (0, w.shape[-1]) if w_cols is None else w_cols
    tm, tn, tk = min(tm, m), min(tn, n), min(tk, kdim)
    assert m % tm == 0 and n % tn == 0 and kdim % tk == 0 and col0 % tn == 0, (name, a.shape, w.shape, tm, tn, tk)
    j0 = col0 // tn
    nk = kdim // tk
    n_ex, n_out = len(extras), len(out_shape)

    def body(*refs):
        a_ref, w_ref = refs[0], refs[1]
        ex = refs[2:2 + n_ex]
        outs = refs[2 + n_ex:2 + n_ex + n_out]
        prod = jnp.dot(a_ref[...], w_ref[...].astype(BF16), preferred_element_type=F32)
        if nk == 1:
            epilogue(prod, ex, outs)
            return
        acc_ref = refs[-1]
        k = pl.program_id(2)

        @pl.when(k == 0)
        def _():
            acc_ref[...] = prod

        if nk > 2:
            @pl.when((k > 0) & (k < nk - 1))
            def _():
                acc_ref[...] += prod

        @pl.when(k == nk - 1)
        def _():
            epilogue(acc_ref[...] + prod, ex, outs)

    in_specs = [pl.BlockSpec((tm, tk), lambda i, j, k: (i, k), pipeline_mode=pl.Buffered(1) if resident_a else None),
                pl.BlockSpec((None,) * len(w_index) + (tk, tn),
                             lambda i, j, k: (*w_index, k, j + j0 + (0 if w_col_shift is None else w_col_shift(i))))]
    in_specs += [spec for _, spec in extras]
    block_bytes = _nbytes((tm, tk), a.dtype) + _nbytes((tk, tn), w.dtype)
    for arr, spec in extras:
        block_bytes += _nbytes([d for d in spec.block_shape if d is not None], arr.dtype)
    for sds, spec in zip(out_shape, out_specs):
        block_bytes += _nbytes([d for d in spec.block_shape if d is not None], sds.dtype)
    acc_bytes = _nbytes((tm, tn), F32)
    return pl.pallas_call(
        body,
        grid=(m // tm, n // tn, nk),
        in_specs=in_specs,
        out_specs=list(out_specs),
        out_shape=list(out_shape),
        scratch_shapes=[pltpu.VMEM((tm, tn), F32)] if nk > 1 else [],
        compiler_params=_params(("parallel", "parallel", "arbitrary"), block_bytes, 8 * acc_bytes),
        name=name,
    )(a, w, *[arr for arr, _ in extras])


def _mm_plain(a, w, out_dtype, *, tm=TILE_M, tn=TILE_N, tk=TILE_K, w_cols=None, name):
    m, n = a.shape[0], (w.shape[1] if w_cols is None else w_cols[1])
    tm, tn = min(tm, m), min(tn, n)

    def epilogue(acc, ex, outs):
        outs[0][...] = acc.astype(out_dtype)

    return _matmul(a, w, tm=tm, tn=tn, tk=tk, epilogue=epilogue, w_cols=w_cols,
                   out_shape=[jax.ShapeDtypeStruct((m, n), out_dtype)],
                   out_specs=[pl.BlockSpec((tm, tn), lambda i, j, k: (i, j))], name=name)[0]


def _mm_residual(a, w, res, mods, sub, coef, rows_per_group, *, tm=TILE_M, tn=TILE_N, tk=TILE_K, w_index=(),
                 resident_a=False, name):
    m, n = a.shape[0], w.shape[-1]
    tm, tn = min(tm, m, rows_per_group), min(tn, n)
    tiles_per_group = rows_per_group // tm
    gate_row = 3 * sub + 2

    def epilogue(acc, ex, outs):
        res_ref, mod_ref = ex
        gate = mod_ref[gate_row:gate_row + 1, :]
        outs[0][...] = res_ref[...] + (coef * gate) * acc

    extras = [(res, pl.BlockSpec((tm, tn), lambda i, j, k: (i, j))),
              (mods, pl.BlockSpec((None, 3 * N_SUB, tn), lambda i, j, k: (i // tiles_per_group, 0, j)))]
    return _matmul(a, w, tm=tm, tn=tn, tk=tk, epilogue=epilogue, extras=extras, w_index=w_index,
                   resident_a=resident_a, out_shape=[jax.ShapeDtypeStruct((m, n), F32)],
                   out_specs=[pl.BlockSpec((tm, tn), lambda i, j, k: (i, j))], name=name)[0]


class RopeLayout(NamedTuple):
    group: int
    lo: int
    n_freq: int
    split: bool


MLA_ROPE_LAYOUT = RopeLayout(group=MLA_HEAD_PAD, lo=NOPE_DIM, n_freq=ROPE_DIM // 4, split=False)
DIFF_ROPE_LAYOUT = RopeLayout(group=DIFF_DIM, lo=0, n_freq=DIFF_DIM // 4, split=True)


def _rope_tables(seq, layout):
    n_freq = layout.n_freq
    pos = jnp.arange(seq, dtype=jnp.int32)
    row_pos = (pos // GRID_W).astype(F32)
    col_pos = (pos % GRID_W).astype(F32)
    inv = ROPE_THETA ** (-jnp.arange(n_freq, dtype=F32) / n_freq)
    ang = jnp.concatenate([row_pos[:, None] * inv, col_pos[:, None] * inv], axis=-1)
    cos, sin = jnp.cos(ang), jnp.sin(ang)
    if layout.split:
        assert 4 * n_freq == LANES
        return jnp.concatenate([cos, cos], axis=-1), jnp.concatenate([-sin, sin], axis=-1)
    cos = jnp.repeat(cos, 2, axis=-1)
    sin = jnp.repeat(sin, 2, axis=-1) * jnp.tile(jnp.array([-1.0, 1.0], F32), 2 * n_freq)
    rest = LANES - 4 * n_freq
    return (jnp.concatenate([cos, jnp.ones((seq, rest), F32)], axis=-1),
            jnp.concatenate([sin, jnp.zeros((seq, rest), F32)], axis=-1))


def _rope_apply(x, cos_ref, sin_ref, layout):
    assert layout.lo % LANES == 0 and layout.group % LANES == 0 and x.shape[-1] % layout.group == 0
    cos, sin = cos_ref[...], sin_ref[...]
    slabs = []
    for c0 in range(0, x.shape[-1], LANES):
        slab = x[:, c0:c0 + LANES]
        if c0 % layout.group == layout.lo:
            if layout.split:
                partner = pltpu.roll(slab, LANES // 2, 1)
            else:
                even = (lax.broadcasted_iota(jnp.int32, slab.shape, 1) & 1) == 0
                partner = jnp.where(even, pltpu.roll(slab, LANES - 1, 1), pltpu.roll(slab, 1, 1))
            slab = slab * cos + partner * sin
        slabs.append(slab)
    return jnp.concatenate(slabs, axis=1) if len(slabs) > 1 else slabs[0]


def _cast_deinterleave_columns(w, n_perm):
    kdim, n = w.shape
    perm = np.concatenate([np.arange(0, LANES, 2), np.arange(1, LANES, 2)])
    p = np.zeros((LANES, LANES), np.float32)
    p[perm, np.arange(LANES)] = 1.0
    tm, tn = min(TILE_M, kdim), min(TILE_N, n)
    assert n_perm % tn == 0
    perm_blocks = n_perm // tn

    def body(w_ref, p_ref, o_ref):
        @pl.when(pl.program_id(1) < perm_blocks)
        def _():
            pm = p_ref[...]
            for c0 in range(0, tn, LANES):
                o_ref[:, c0:c0 + LANES] = jnp.dot(w_ref[:, c0:c0 + LANES].astype(BF16), pm,
                                                  preferred_element_type=F32).astype(BF16)

        @pl.when(pl.program_id(1) >= perm_blocks)
        def _():
            o_ref[...] = w_ref[...].astype(BF16)

    return pl.pallas_call(
        body,
        grid=(kdim // tm, n // tn),
        in_specs=[pl.BlockSpec((tm, tn), lambda i, j: (i, j)), pl.BlockSpec((LANES, LANES), lambda i, j: (0, 0))],
        out_specs=pl.BlockSpec((tm, tn), lambda i, j: (i, j)),
        out_shape=jax.ShapeDtypeStruct((kdim, n), BF16),
        compiler_params=_params(("parallel", "parallel"), _nbytes((tm, tn), w.dtype) + _nbytes((tm, tn), BF16)),
        name="cast_deinterleave_rotary_columns",
    )(w, jnp.asarray(p, BF16))


def _mm_rope(a, w, tables, scale, seq, out_dtype, *, layout, tm=TILE_M, tn=TILE_N, tk=TILE_K, w_cols=None, name):
    m, n = a.shape[0], (w.shape[1] if w_cols is None else w_cols[1])
    tm, tn = min(tm, m, seq), min(tn, n)
    tiles_per_seq = seq // tm

    def epilogue(acc, ex, outs):
        y = acc if tables is None else _rope_apply(acc, ex[0], ex[1], layout)
        outs[0][...] = (y * scale).astype(out_dtype)

    extras = []
    if tables is not None:
        spec = pl.BlockSpec((tm, LANES), lambda i, j, k: (i % tiles_per_seq, 0))
        extras = [(tables[0], spec), (tables[1], spec)]
    return _matmul(a, w, tm=tm, tn=tn, tk=tk, epilogue=epilogue, extras=extras, w_cols=w_cols,
                   out_shape=[jax.ShapeDtypeStruct((m, n), out_dtype)],
                   out_specs=[pl.BlockSpec((tm, tn), lambda i, j, k: (i, j))], name=name)[0]


def _modulation(cvec, mod_w, mod_b, *, tn=512):
    depth, d, n = mod_w.shape
    rows = cvec.shape[0]
    tn = min(tn, n)

    def body(c_ref, w_ref, b_ref, o_ref):
        cv = c_ref[...]
        s = cv * (1.0 / (1.0 + jnp.exp(-cv)))
        o_ref[...] = jnp.dot(s, w_ref[...], preferred_element_type=F32) + b_ref[...]

    block_bytes = _nbytes((d, tn), F32) + _nbytes((rows, d), F32) + 2 * _nbytes((rows, tn), F32)
    return pl.pallas_call(
        body,
        grid=(depth, n // tn),
        in_specs=[pl.BlockSpec((rows, d), lambda l, j: (0, 0)),
                  pl.BlockSpec((None, d, tn), lambda l, j: (l, 0, j)),
                  pl.BlockSpec((None, 1, tn), lambda l, j: (l, 0, j))],
        out_specs=pl.BlockSpec((None, rows, tn), lambda l, j: (l, 0, j)),
        out_shape=jax.ShapeDtypeStruct((depth, rows, n), F32),
        compiler_params=_params(("parallel", "parallel"), block_bytes),
        name="adaln_modulation",
    )(cvec, mod_w, mod_b.reshape(depth, 1, n))


def _norm_mod(x, g, mods, sub, rows_per_group, out_dtype, *, tm=256):
    m, d = x.shape
    tm = min(tm, m, rows_per_group)
    tiles_per_group = rows_per_group // tm

    def body(*refs):
        if mods is None:
            x_ref, g_ref, o_ref = refs
        else:
            x_ref, g_ref, mod_ref, o_ref = refs
        y = _rms(x_ref[...], g_ref[...])
        if mods is not None:
            shift = mod_ref[3 * sub:3 * sub + 1, :]
            scale = mod_ref[3 * sub + 1:3 * sub + 2, :]
            y = y * (1.0 + scale) + shift
        o_ref[...] = y.astype(out_dtype)

    in_specs = [pl.BlockSpec((tm, d), lambda i: (i, 0)), pl.BlockSpec((1, d), lambda i: (0, 0))]
    args = [x, g.reshape(1, d)]
    if mods is not None:
        in_specs.append(pl.BlockSpec((None, 3 * N_SUB, d), lambda i: (i // tiles_per_group, 0, 0)))
        args.append(mods)
    block_bytes = _nbytes((tm, d), F32) + _nbytes((tm, d), out_dtype) + _nbytes((16, d), F32)
    return pl.pallas_call(
        body,
        grid=(m // tm,),
        in_specs=in_specs,
        out_specs=pl.BlockSpec((tm, d), lambda i: (i, 0)),
        out_shape=jax.ShapeDtypeStruct((m, d), out_dtype),
        compiler_params=_params(("parallel",), block_bytes, 2 * _nbytes((tm, d), F32)),
        name="rmsnorm_modulate",
    )(*args)


def _gate_up(h, w_gate, w_up, w_index, *, tm=2 * TILE_M, tn=TILE_N, tk=TILE_K):
    m, kdim = h.shape
    n = w_gate.shape[-1]
    tn = tn * jnp.dtype(BF16).itemsize // jnp.dtype(w_gate.dtype).itemsize
    tm, tn, tk = min(tm, m), min(tn, n), min(tk, kdim)
    assert m % tm == 0 and n % tn == 0 and kdim % tk == 0
    nk = kdim // tk

    def finish(g, u, o_ref):
        o_ref[...] = (g * (1.0 / (1.0 + jnp.exp(-g))) * u).astype(o_ref.dtype)

    def body(h_ref, wg_ref, wu_ref, o_ref, *acc):
        hh = h_ref[...]
        pg = jnp.dot(hh, wg_ref[...].astype(BF16), preferred_element_type=F32)
        pu = jnp.dot(hh, wu_ref[...].astype(BF16), preferred_element_type=F32)
        if nk == 1:
            finish(pg, pu, o_ref)
            return
        accg, accu = acc
        k = pl.program_id(2)

        @pl.when(k == 0)
        def _():
            accg[...] = pg
            accu[...] = pu

        if nk > 2:
            @pl.when((k > 0) & (k < nk - 1))
            def _():
                accg[...] += pg
                accu[...] += pu

        @pl.when(k == nk - 1)
        def _():
            finish(accg[...] + pg, accu[...] + pu, o_ref)

    block_bytes = _nbytes((tm, tk), BF16) + 2 * _nbytes((tk, tn), w_gate.dtype) + _nbytes((tm, tn), BF16)
    acc_bytes = _nbytes((tm, tn), F32) + _nbytes((tk, tn), BF16)
    w_spec = pl.BlockSpec((None,) * len(w_index) + (tk, tn), lambda i, j, k: (*w_index, k, j))
    return pl.pallas_call(
        body,
        grid=(m // tm, n // tn, nk),
        in_specs=[pl.BlockSpec((tm, tk), lambda i, j, k: (i, k), pipeline_mode=pl.Buffered(1)), w_spec, w_spec],
        out_specs=pl.BlockSpec((tm, tn), lambda i, j, k: (i, j)),
        out_shape=jax.ShapeDtypeStruct((m, n), BF16),
        scratch_shapes=[pltpu.VMEM((tm, tn), F32)] * 2 if nk > 1 else [],
        compiler_params=_params(("parallel", "parallel", "arbitrary"), block_bytes, 6 * acc_bytes),
        name="ffn_gate_up",
    )(h, w_gate, w_up)


def _half_ffn(x, mods, sub, rows_per_group, g, w_gate, w_up, w_down, w_index):
    h = _norm_mod(x, g, mods, sub, rows_per_group, BF16)
    u = _gate_up(h, w_gate, w_up, w_index)
    tn = TILE_N * jnp.dtype(BF16).itemsize // jnp.dtype(w_down.dtype).itemsize
    return _mm_residual(u, w_down, x, mods, sub, 0.5, rows_per_group, w_index=w_index,
                        tm=TILE_M, tn=tn, tk=TILE_K_LONG, resident_a=True, name="ffn_down_residual")


class ScoreSet(NamedTuple):
    q: object
    k_cols: slice
    write: object
    read: object


def _lane_fold(x, op):
    return functools.reduce(op, [x[:, c0:c0 + LANES] for c0 in range(0, x.shape[-1], LANES)])


def _attention_step(sets, k_refs, v_refs, chunk):
    state = [dict(mx=None, ls=None, acc=None, m_old=s.read[1][...]) for s in sets]
    off = 0
    for k_ref, v_ref in zip(k_refs, v_refs):
        total = k_ref.shape[0]
        for c0 in range(0, total, chunk):
            rows = min(chunk, total - c0)
            cols = slice(off + c0, off + c0 + rows)
            for s, st in zip(sets, state):
                if s.write is not None:
                    new = lax.dot_general(s.q, k_ref[c0:c0 + rows, s.k_cols], (((1,), (1,)), ((), ())),
                                          preferred_element_type=F32)
                    s.write[0][:, cols] = new
                    part = _lane_fold(new, jnp.maximum)
                    st["mx"] = part if st["mx"] is None else jnp.maximum(st["mx"], part)
            for s, st in zip(sets, state):
                m_old = jnp.concatenate([st["m_old"]] * (rows // LANES), axis=1) if rows > LANES else st["m_old"]
                e = jnp.exp(s.read[0][:, cols] - m_old)
                part = _lane_fold(e, jnp.add)
                st["ls"] = part if st["ls"] is None else st["ls"] + part
                pv = jnp.dot(e.astype(BF16), v_ref[c0:c0 + rows, :], preferred_element_type=F32)
                st["acc"] = pv if st["acc"] is None else st["acc"] + pv
        off += total
    out = []
    for s, st in zip(sets, state):
        if s.write is not None:
            s.write[1][...] = jnp.broadcast_to(jnp.max(st["mx"], axis=-1, keepdims=True), st["mx"].shape)
        out.append((st["acc"], jnp.sum(st["ls"], axis=-1, keepdims=True)))
    return out


def _skewed_steps(nq, score_bufs, emit):
    i = pl.program_id(2)
    buf_a, buf_b = score_bufs

    @pl.when(i == 0)
    def _():
        for pair in buf_b:
            for ref in pair:
                ref[...] = jnp.zeros_like(ref)

    even = lax.rem(i, 2) == 0

    @pl.when((i < nq) & even)
    def _():
        emit(buf_a, buf_b)

    @pl.when((i < nq) & jnp.logical_not(even))
    def _():
        emit(buf_b, buf_a)

    @pl.when(i == nq)
    def _():
        emit(None, buf_a if (nq - 1) % 2 == 0 else buf_b)


def _skewed_maps(nq):
    return (lambda bb, h, i: (bb, jnp.minimum(i, nq - 1), h)), (lambda bb, h, i: (bb, jnp.maximum(i - 1, 0), h))


def _kv_specs(arrays, width):
    return [pl.BlockSpec((None, a.shape[1], width), lambda bb, h, i: (bb, 0, h)) for a in arrays]


def _mla_attention(q, ks, vs, heads, *, tq=2 * ATTN_TQ):
    b, sq, _ = q.shape
    sk = sum(k.shape[1] for k in ks)
    tq = min(tq, sq)
    nkv = len(ks)

    nq = sq // tq
    q_map, o_map = _skewed_maps(nq)

    def body(q_ref, *refs):
        k_refs, v_refs = refs[:nkv], refs[nkv:2 * nkv]
        o_ref = refs[2 * nkv]
        s_a, m_a, s_b, m_b = refs[2 * nkv + 1:]

        def emit(write, read):
            sets = [ScoreSet(q_ref[...], slice(None), None if write is None else write[0], read[0])]
            (o, l), = _attention_step(sets, k_refs, v_refs, ATTN_CHUNK)
            o_ref[...] = (o / l).astype(o_ref.dtype)

        _skewed_steps(nq, ([(s_a, m_a)], [(s_b, m_b)]), emit)

    block_bytes = (_nbytes((tq, MLA_HEAD_PAD), BF16) + _nbytes((sk, MLA_HEAD_PAD), BF16)
                   + _nbytes((sk, V_DIM), BF16) + _nbytes((tq, V_DIM), BF16))
    score_bytes = _nbytes((tq, sk), F32)
    return pl.pallas_call(
        body,
        grid=(b, heads, nq + 1),
        in_specs=([pl.BlockSpec((None, tq, MLA_HEAD_PAD), q_map)]
                  + _kv_specs(ks, MLA_HEAD_PAD) + _kv_specs(vs, V_DIM)),
        out_specs=pl.BlockSpec((None, tq, V_DIM), o_map),
        out_shape=jax.ShapeDtypeStruct((b, sq, heads * V_DIM), BF16),
        scratch_shapes=[pltpu.VMEM((tq, sk), F32), pltpu.VMEM((tq, LANES), F32)] * 2,
        compiler_params=_params(("parallel", "parallel", "arbitrary"), block_bytes, 4 * score_bytes),
        name="mla_attention",
    )(q, *ks, *vs)


def _diff_attention(q, ks, vs, lam_p, subln_g, lambda_init, heads, *, tq=ATTN_TQ):
    b, sq, _ = q.shape
    sk = sum(k.shape[1] for k in ks)
    tq = min(tq, sq)
    hw = 2 * DIFF_DIM
    nkv = len(ks)
    nq = sq // tq
    q_map, o_map = _skewed_maps(nq)

    def body(lam_ref, g_ref, q_ref, *refs):
        k_refs, v_refs = refs[:nkv], refs[nkv:2 * nkv]
        o_ref = refs[2 * nkv]
        bufs = refs[2 * nkv + 1:]
        set_a = [(bufs[0], bufs[1]), (bufs[2], bufs[3])]
        set_b = [(bufs[4], bufs[5]), (bufs[6], bufs[7])]

        def emit(write, read):
            halves = [(q_ref[:, :DIFF_DIM], slice(0, DIFF_DIM)), (q_ref[:, DIFF_DIM:], slice(DIFF_DIM, hw))]
            sets = [ScoreSet(qh, cols, None if write is None else write[n], read[n])
                    for n, (qh, cols) in enumerate(halves)]
            (o1, l1), (o2, l2) = _attention_step(sets, k_refs, v_refs, ATTN_CHUNK)
            lp = lam_ref[...]
            lam = (jnp.exp(jnp.sum(lp[0:1] * lp[1:2], axis=-1, keepdims=True))
                   - jnp.exp(jnp.sum(lp[2:3] * lp[3:4], axis=-1, keepdims=True)) + lambda_init)
            o = o1 / l1 - (lam / l2) * o2
            o_ref[...] = (_rms(o, g_ref[...]) * (1.0 - lambda_init)).astype(o_ref.dtype)

        _skewed_steps(nq, (set_a, set_b), emit)

    block_bytes = 2 * _nbytes((tq, hw), BF16) + 2 * _nbytes((sk, hw), BF16)
    score_bytes = _nbytes((tq, sk), F32)
    return pl.pallas_call(
        body,
        grid=(b, heads, nq + 1),
        in_specs=([pl.BlockSpec((4, DIFF_DIM), lambda bb, h, i: (0, 0)),
                   pl.BlockSpec((1, hw), lambda bb, h, i: (0, 0)),
                   pl.BlockSpec((None, tq, hw), q_map)]
                  + _kv_specs(ks, hw) + _kv_specs(vs, hw)),
        out_specs=pl.BlockSpec((None, tq, hw), o_map),
        out_shape=jax.ShapeDtypeStruct((b, sq, heads * hw), BF16),
        scratch_shapes=[pltpu.VMEM((tq, sk), F32), pltpu.VMEM((tq, LANES), F32)] * 4,
        compiler_params=_params(("parallel", "parallel", "arbitrary"), block_bytes, 6 * score_bytes),
        name="diff_attention",
    )(lam_p.astype(F32), subln_g.reshape(1, hw).astype(F32), q, *ks, *vs)


def _hyena_filters(seq, w1, b1, w2, b2, w3, b3, freq, w_out, *, tl=256):
    hid = w1.shape[1]
    width2 = w_out.shape[1]
    width = width2 // 2
    bands = (HY_EMB - 1) // 2
    t = jnp.linspace(0.0, 1.0, seq, dtype=F32)[:, None]
    ang = ((2.0 * math.pi / seq) * jnp.arange(seq, dtype=F32)[:, None]
           * jnp.linspace(1e-4, bands - 1, bands, dtype=F32)[None, :])
    z = jnp.concatenate([t, jnp.cos(ang), -jnp.sin(ang), jnp.zeros((seq, LANES - HY_EMB), F32)], axis=-1)
    deltas = jnp.abs(jnp.linspace(math.log(HY_TARGET) / HY_LONG_DECAY_PCT,
                                  math.log(HY_TARGET) / HY_SHORT_DECAY_PCT, width, dtype=F32))
    deltas2 = jnp.concatenate([deltas, deltas])[None, :]

    def pad2(w):
        return jnp.pad(w.astype(F32), ((0, LANES - w.shape[0]), (0, LANES - w.shape[1])))

    def pad_row(v):
        return jnp.pad(v.astype(F32), (0, LANES - v.shape[0]))[None, :]

    w_out_p = jnp.pad(w_out.astype(F32), ((0, LANES - hid), (0, 0)))
    tl, tn = min(tl, seq), width2
    exact = lax.Precision.HIGHEST

    def body(z_ref, w1_ref, b1_ref, w2_ref, b2_ref, w3_ref, b3_ref, f_ref, wo_ref, d_ref, o_ref, nyq_ref):
        zz, f = z_ref[...], f_ref[...]
        a = jnp.sin(f * (jnp.dot(zz, w1_ref[...], precision=exact, preferred_element_type=F32) + b1_ref[...]))
        a = jnp.sin(f * (jnp.dot(a, w2_ref[...], precision=exact, preferred_element_type=F32) + b2_ref[...]))
        a = jnp.sin(f * (jnp.dot(a, w3_ref[...], precision=exact, preferred_element_type=F32) + b3_ref[...]))
        h = jnp.dot(a, wo_ref[...], precision=exact, preferred_element_type=F32)
        h = h * jnp.exp(-zz[:, 0:1] * d_ref[...])
        row = pl.program_id(0) * tl + lax.broadcasted_iota(jnp.int32, (tl, width), 0)
        h_fwd = h[:, :width]
        h_bwd = jnp.where(row == 0, 0.0, h[:, width:])
        h_sum = h_fwd + h_bwd
        o_ref[:, :width] = h_sum.astype(o_ref.dtype)
        o_ref[:, width:] = (h_fwd - h_bwd).astype(o_ref.dtype)
        alternating = jnp.sum(jnp.where((row & 1) == 0, h_sum, -h_sum), axis=0, keepdims=True)

        @pl.when(pl.program_id(0) == 0)
        def _():
            nyq_ref[...] = alternating

        @pl.when(pl.program_id(0) > 0)
        def _():
            nyq_ref[...] += alternating

    sq = pl.BlockSpec((LANES, LANES), lambda i: (0, 0))
    vec = pl.BlockSpec((1, LANES), lambda i: (0, 0))
    block_bytes = _nbytes((tl, LANES), F32) + _nbytes((LANES, tn), F32) + 2 * _nbytes((tl, tn), F32)
    return pl.pallas_call(
        body,
        grid=(seq // tl,),
        in_specs=[pl.BlockSpec((tl, LANES), lambda i: (i, 0)), sq, vec, sq, vec, sq, vec, vec,
                  pl.BlockSpec((LANES, tn), lambda i: (0, 0)),
                  pl.BlockSpec((1, tn), lambda i: (0, 0))],
        out_specs=[pl.BlockSpec((tl, tn), lambda i: (i, 0)), pl.BlockSpec((1, width), lambda i: (0, 0))],
        out_shape=[jax.ShapeDtypeStruct((seq, width2), BF16), jax.ShapeDtypeStruct((1, width), F32)],
        compiler_params=_params(("arbitrary",), block_bytes, 4 * _nbytes((tl, tn), F32)),
        name="hyena_filters",
    )(z, pad2(w1), pad_row(b1), pad2(w2), pad_row(b2), pad2(w3), pad_row(b3), pad_row(freq), w_out_p, deltas2)


def _hyena_prep(zh, conv_w, conv_b, *, tc=128):
    b, seq, w3 = zh.shape
    width = w3 // 3
    tc = min(tc, width)
    nw = width // tc

    def body(z0_ref, z1_ref, z2_ref, w0_ref, w1_ref, w2_ref, b0_ref, b1_ref, b2_ref, x0_ref, vx_ref):
        row = lax.broadcasted_iota(jnp.int32, (seq, 1), 0)

        def conv(z_ref, w_ref, b_ref):
            z = z_ref[...]
            prev = jnp.where(row == 0, 0.0, pltpu.roll(z, 1, 0))
            nxt = jnp.where(row == seq - 1, 0.0, pltpu.roll(z, seq - 1, 0))
            w = w_ref[...]
            return prev * w[0:1] + z * w[1:2] + nxt * w[2:3] + b_ref[...]

        x0 = conv(z0_ref, w0_ref, b0_ref)
        x1 = conv(z1_ref, w1_ref, b1_ref)
        v = conv(z2_ref, w2_ref, b2_ref)
        x0_ref[...] = x0
        vx_ref[...] = (v * x1).astype(vx_ref.dtype)

    def zspec(part):
        return pl.BlockSpec((None, seq, tc), lambda bb, j: (bb, 0, part * nw + j))

    def wspec(rows, part):
        return pl.BlockSpec((rows, tc), lambda bb, j: (0, part * nw + j))

    out_spec = pl.BlockSpec((seq, tc), lambda bb, j: (0, bb * nw + j))
    block_bytes = 4 * _nbytes((seq, tc), F32) + _nbytes((seq, tc), BF16)
    return pl.pallas_call(
        body,
        grid=(b, nw),
        in_specs=[zspec(0), zspec(1), zspec(2), wspec(3, 0), wspec(3, 1), wspec(3, 2),
                  wspec(1, 0), wspec(1, 1), wspec(1, 2)],
        out_specs=[out_spec, out_spec],
        out_shape=[jax.ShapeDtypeStruct((seq, b * width), F32), jax.ShapeDtypeStruct((seq, b * width), BF16)],
        compiler_params=_params(("parallel", "parallel"), block_bytes, 8 * _nbytes((seq, tc), F32)),
        name="hyena_short_conv",
    )(zh, zh, zh, conv_w, conv_w, conv_w, conv_b[None, :], conv_b[None, :], conv_b[None, :])


def _dft_matrices(seq):
    n = 2 * seq
    assert n % 4 == 0
    kb = math.gcd(seq, 64)
    s = jnp.arange(seq, dtype=jnp.int32)

    def cos_sin(k):
        phase = _mod_nonneg(k[:, None] * s[None, :], n)
        return _cos_turns(phase, n), _cos_turns(_mod_nonneg(phase + (n - n // 4), n), n)

    c_hi, s_hi = cos_sin(jnp.arange(seq // kb, dtype=jnp.int32) * kb)
    c_lo, s_lo = cos_sin(jnp.arange(kb, dtype=jnp.int32))
    cos = (c_hi[:, None, :] * c_lo[None] - s_hi[:, None, :] * s_lo[None]).reshape(seq, seq)
    sin = (s_hi[:, None, :] * c_lo[None] + c_hi[:, None, :] * s_lo[None]).reshape(seq, seq)
    nyquist = jnp.where((s & 1) == 0, 1.0, -1.0).astype(F32)
    weight = jnp.where(s == 0, 1.0 / n, 2.0 / n).astype(F32)
    fwd = jnp.concatenate([cos, jnp.where(s[:, None] == 0, nyquist[None, :], sin)], axis=0)
    inv = jnp.concatenate([cos * weight[None, :],
                           jnp.where(s[None, :] == 0, nyquist[:, None], sin) * weight[None, :]], axis=1)
    return fwd.astype(BF16), inv.astype(BF16)


def _div_nonneg(x, d):
    return x >> (d.bit_length() - 1) if d & (d - 1) == 0 else x // d


def _mod_nonneg(x, d):
    return x & (d - 1) if d & (d - 1) == 0 else x % d


def _cos_turns(phase, n):
    quarter = n // 4
    quad = _div_nonneg(phase, quarter)
    rem = phase - quad * quarter
    odd = (quad & 1) == 1
    x = jnp.where(odd, quarter - rem, rem).astype(F32) * (2.0 * math.pi / n)
    x2 = x * x
    acc = jnp.full_like(x2, 1.0 / math.factorial(16))
    for order in range(14, -1, -2):
        acc = acc * (-x2) + 1.0 / math.factorial(order)
    return jnp.where((quad == 1) | (quad == 2), -acc, acc)


def _spectrum_multiply(spec_u, spec_g, nyquist_g, batch, width, *, tr=512, tc=512):
    n = spec_u.shape[0]
    half = n // 2
    tr, tc = min(tr, half), min(tc, width)
    nwc = width // tc

    def body(u_ref, g_ref, nyq_ref, y_ref):
        ua, ub = u_ref[0], u_ref[1]
        ga, gb = g_ref[0], g_ref[1]
        row = pl.program_id(0) * tr + lax.broadcasted_iota(jnp.int32, ua.shape, 0)
        dc = row == 0
        ya = ua * ga - jnp.where(dc, 0.0, ub * gb)
        yb = jnp.where(dc, ub * nyq_ref[...], ua * gb + ub * ga)
        y_ref[0] = ya.astype(y_ref.dtype)
        y_ref[1] = yb.astype(y_ref.dtype)

    block_bytes = 2 * _nbytes((2, tr, tc), F32) + _nbytes((2, tr, tc), BF16)
    y = pl.pallas_call(
        body,
        grid=(half // tr, nwc, batch),
        in_specs=[pl.BlockSpec((2, tr, tc), lambda i, j, bb: (0, i, bb * nwc + j)),
                  pl.BlockSpec((2, tr, tc), lambda i, j, bb: (0, i, j)),
                  pl.BlockSpec((1, tc), lambda i, j, bb: (0, j))],
        out_specs=pl.BlockSpec((2, tr, tc), lambda i, j, bb: (0, i, bb * nwc + j)),
        out_shape=jax.ShapeDtypeStruct((2, half, batch * width), BF16),
        compiler_params=_params(("parallel", "parallel", "arbitrary"), block_bytes, 8 * _nbytes((tr, tc), F32)),
        name="hyena_spectrum_multiply",
    )(spec_u.reshape(2, half, batch * width), spec_g.reshape(2, half, width), nyquist_g)
    return y.reshape(n, batch * width)


def _hyena_branch(zh, conv_w, conv_b, filt, hy_bias):
    b, seq, w3 = zh.shape
    width = w3 // 3
    x0, vx = _hyena_prep(zh, conv_w.astype(F32), conv_b.astype(F32))
    h_sum_dif, nyquist_g = _hyena_filters(seq, *filt)
    fwd, inv = _dft_matrices(seq)
    spec_u = _mm_plain(fwd, vx, F32, name="hyena_dft_forward")
    tm_g, tn_g = min(TILE_M, seq), min(TILE_N, width)

    def store(acc, ex, outs):
        outs[0][...] = acc

    spec_g = _matmul(fwd, h_sum_dif, tm=tm_g, tn=tn_g, tk=TILE_K, epilogue=store, w_cols=(0, width),
                     w_col_shift=lambda i: (i // (seq // tm_g)) * (width // tn_g),
                     out_shape=[jax.ShapeDtypeStruct((2 * seq, width), F32)],
                     out_specs=[pl.BlockSpec((tm_g, tn_g), lambda i, j, k: (i, j))], name="hyena_dft_filters")[0]
    y_hat = _spectrum_multiply(spec_u, spec_g, nyquist_g, b, width)

    tm, tn = min(TILE_M_LONG_K, seq), min(TILE_N, width)
    nw = width // tn

    def epilogue(acc, ex, outs):
        x0_ref, vx_ref, bias_ref = ex
        outs[0][...] = (x0_ref[...] * (acc + bias_ref[...] * vx_ref[...].astype(F32))).astype(BF16)

    extras = [(x0, pl.BlockSpec((tm, tn), lambda i, j, k: (i, j))),
              (vx, pl.BlockSpec((tm, tn), lambda i, j, k: (i, j))),
              (hy_bias.astype(F32)[None, :], pl.BlockSpec((1, tn), lambda i, j, k: (0, j % nw)))]
    return _matmul(inv, y_hat, tm=tm, tn=tn, tk=TILE_K_LONG, epilogue=epilogue, extras=extras,
                   out_shape=[jax.ShapeDtypeStruct((b, seq, width), BF16)],
                   out_specs=[pl.BlockSpec((None, tm, tn), lambda i, j, k: (j // nw, i, j % nw))],
                   name="hyena_dft_inverse")[0]


def _ab_latents(h, w1, q_norm_g, kv_norm_g, tables, seq, q_lora, kv_lora, *, tm=512, tk=1024):
    m = h.shape[0]
    n1 = w1.shape[1]
    tm = min(tm, m, seq)
    tiles_per_seq = seq // tm

    def epilogue(acc, ex, outs):
        outs[0][...] = _rms(acc[:, :q_lora], ex[0][...]).astype(BF16)
        outs[1][...] = _rms(acc[:, q_lora:q_lora + kv_lora], ex[1][...]).astype(BF16)
        kr = acc[:, q_lora + kv_lora:]
        if tables is not None:
            kr = _rope_apply(kr, ex[2], ex[3], MLA_ROPE_LAYOUT)
        outs[2][...] = kr

    extras = [(q_norm_g.astype(F32)[None, :], pl.BlockSpec((1, q_lora), lambda i, j, k: (0, 0))),
              (kv_norm_g.astype(F32)[None, :], pl.BlockSpec((1, kv_lora), lambda i, j, k: (0, 0)))]
    if tables is not None:
        spec = pl.BlockSpec((tm, LANES), lambda i, j, k: (i % tiles_per_seq, 0))
        extras += [(tables[0], spec), (tables[1], spec)]
    widths = (q_lora, kv_lora, MLA_HEAD_PAD)
    dtypes = (BF16, BF16, F32)
    return _matmul(h, w1, tm=tm, tn=n1, tk=tk, epilogue=epilogue, extras=extras,
                   out_shape=[jax.ShapeDtypeStruct((m, wd), dt) for wd, dt in zip(widths, dtypes)],
                   out_specs=[pl.BlockSpec((tm, wd), lambda i, j, k: (i, 0)) for wd in widths],
                   name="mla_latent_projection")


def _mla_keys(ckv, w_uk, kr, heads, *, tm=1024, tn=1024):
    m = ckv.shape[0]
    n = heads * MLA_HEAD_PAD
    tm, tn = min(tm, m), min(tn, n)
    reps = tn // MLA_HEAD_PAD

    def epilogue(acc, ex, outs):
        rot = ex[0][...]
        if reps > 1:
            rot = jnp.concatenate([rot] * reps, axis=1)
        outs[0][...] = (acc + rot).astype(BF16)

    extras = [(kr, pl.BlockSpec((tm, MLA_HEAD_PAD), lambda i, j, k: (i, 0)))]
    return _matmul(ckv, w_uk, tm=tm, tn=tn, tk=ckv.shape[1], epilogue=epilogue, extras=extras,
                   out_shape=[jax.ShapeDtypeStruct((m, n), BF16)],
                   out_specs=[pl.BlockSpec((tm, tn), lambda i, j, k: (i, j))], name="mla_key_up")[0]


def _ab_mixer(h, hc, batch, seq, seq_c, need_ctx, w_in, q_norm_g, kv_norm_g, w_uq, w_ukv,
              conv_w, conv_b, filt, hy_bias):
    d = w_in.shape[0]
    heads = d // (2 * V_DIM)
    q_lora, kv_lora = w_uq.shape[0], w_ukv.shape[0]
    kv_end = q_lora + kv_lora + ROPE_DIM
    mla_scale = (NOPE_DIM + ROPE_DIM) ** -0.5
    zero_pad = MLA_HEAD_PAD - NOPE_DIM - ROPE_DIM

    w1 = jnp.concatenate([w_in[:, :q_lora + kv_lora], jnp.zeros((d, NOPE_DIM), w_in.dtype),
                          w_in[:, q_lora + kv_lora:kv_end], jnp.zeros((d, zero_pad), w_in.dtype)],
                         axis=1).astype(BF16)
    w_hy = w_in[:, kv_end:].astype(BF16)
    uq = w_uq.reshape(q_lora, heads, NOPE_DIM + ROPE_DIM)
    w_uq_p = jnp.pad(uq, ((0, 0), (0, 0), (0, zero_pad))).reshape(q_lora, heads * MLA_HEAD_PAD).astype(BF16)
    ukv = w_ukv.reshape(kv_lora, heads, NOPE_DIM + V_DIM)
    w_uk_p = jnp.pad(ukv[..., :NOPE_DIM], ((0, 0), (0, 0), (0, MLA_HEAD_PAD - NOPE_DIM))
                     ).reshape(kv_lora, heads * MLA_HEAD_PAD).astype(BF16)
    w_uv = ukv[..., NOPE_DIM:].reshape(kv_lora, heads * V_DIM).astype(BF16)

    tables = _rope_tables(seq, MLA_ROPE_LAYOUT)
    rope_kw = dict(layout=MLA_ROPE_LAYOUT)

    cq, ckv, kr = _ab_latents(h, w1, q_norm_g, kv_norm_g, tables, seq, q_lora, kv_lora)
    cqc, ckvc, krc = _ab_latents(hc, w1, q_norm_g, kv_norm_g, None, seq_c, q_lora, kv_lora)
    q = _mm_rope(cq, w_uq_p, tables, mla_scale, seq, BF16, name="mla_query_up", **rope_kw)
    k = _mla_keys(ckv, w_uk_p, kr, heads)
    v = _mm_plain(ckv, w_uv, BF16, name="mla_value_up")
    kc = _mla_keys(ckvc, w_uk_p, krc, heads)
    vc = _mm_plain(ckvc, w_uv, BF16, name="mla_value_up")

    def split(t, s):
        return t.reshape(batch, s, t.shape[-1])

    att = _mla_attention(split(q, seq), [split(kc, seq_c), split(k, seq)], [split(vc, seq_c), split(v, seq)], heads)
    zh = _mm_plain(h, w_hy, F32, tn=512, name="hyena_in_projection")
    hy = _hyena_branch(split(zh, seq), conv_w, conv_b, filt, hy_bias)
    mix = jnp.concatenate([att, hy], axis=-1).reshape(batch * seq, -1)
    if not need_ctx:
        return mix, None
    qc = _mm_rope(cqc, w_uq_p, None, mla_scale, seq_c, BF16, name="mla_query_up", **rope_kw)
    attc = _mla_attention(split(qc, seq_c), [split(kc, seq_c)], [split(vc, seq_c)], heads)
    zhc = _mm_plain(hc, w_hy, F32, tn=512, name="hyena_in_projection")
    hyc = _hyena_branch(split(zhc, seq_c), conv_w, conv_b, filt, hy_bias)
    return mix, jnp.concatenate([attc, hyc], axis=-1).reshape(batch * seq_c, -1)


def _diff_mixer(h, hc, batch, seq, seq_c, need_ctx, w_in, lam_p, subln_g, lambda_init):
    d = w_in.shape[0]
    hd = w_in.shape[1] // 3
    heads = hd // (2 * DIFF_DIM)
    scale = DIFF_DIM ** -0.5
    w = _cast_deinterleave_columns(w_in, 2 * hd)
    q_cols, k_cols, v_cols = (0, hd), (hd, hd), (2 * hd, hd)
    tables = _rope_tables(seq, DIFF_ROPE_LAYOUT)
    rope_kw = dict(layout=DIFF_ROPE_LAYOUT)

    q = _mm_rope(h, w, tables, scale, seq, BF16, w_cols=q_cols, name="diff_query_projection", **rope_kw)
    k = _mm_rope(h, w, tables, 1.0, seq, BF16, w_cols=k_cols, name="diff_key_projection", **rope_kw)
    v = _mm_plain(h, w, BF16, w_cols=v_cols, name="diff_value_projection")
    kc = _mm_rope(hc, w, None, 1.0, seq_c, BF16, w_cols=k_cols, name="diff_key_projection", **rope_kw)
    vc = _mm_plain(hc, w, BF16, w_cols=v_cols, name="diff_value_projection")

    def split(t, s):
        return t.reshape(batch, s, t.shape[-1])

    o = _diff_attention(split(q, seq), [split(kc, seq_c), split(k, seq)], [split(vc, seq_c), split(v, seq)],
                        lam_p, subln_g, lambda_init, heads)
    mix = o.reshape(batch * seq, hd)
    if not need_ctx:
        return mix, None
    qc = _mm_rope(hc, w, None, scale, seq_c, BF16, w_cols=q_cols, name="diff_query_projection", **rope_kw)
    oc = _diff_attention(split(qc, seq_c), [split(kc, seq_c)], [split(vc, seq_c)], lam_p, subln_g, lambda_init, heads)
    return mix, oc.reshape(batch * seq_c, hd)


def kernel(x, c, ctx, c_ctx, mod_w, mod_b, norm_g, ffn_w_gate, ffn_w_up, ffn_w_down, ab_w_in, mla_q_norm_g, mla_kv_norm_g, mla_w_uq, mla_w_ukv, hy_conv_w, hy_conv_b, hy_w1, hy_b1, hy_w2, hy_b2, hy_w3, hy_b3, hy_freq, hy_w_out, hy_bias, ab_w_out, c_w_in, c_lambda, c_subln_g, c_w_out, final_norm_g):
    batch, seq, d = x.shape
    seq_c = ctx.shape[1]
    depth = mod_w.shape[0]
    rows_c = batch * seq_c
    xs = x.reshape(batch * seq, d).astype(F32)
    xc = ctx.reshape(rows_c, d).astype(F32)

    cvec = jnp.concatenate([c.astype(F32), c_ctx.astype(F32)[None, :]], axis=0)
    cvec = jnp.pad(cvec, ((0, -cvec.shape[0] % 8), (0, 0)))
    mods_all = _modulation(cvec, mod_w.astype(F32), mod_b.astype(F32))

    for layer in range(depth):
        need_ctx = layer < depth - 1
        mods = mods_all[layer, :batch].reshape(batch, 3 * N_SUB, d)
        modc = mods_all[layer, batch:batch + 1].reshape(1, 3 * N_SUB, d)
        g = norm_g[layer].astype(F32)

        def ffn(t, m, rows, sub, which):
            return _half_ffn(t, m, sub, rows, g[sub], ffn_w_gate, ffn_w_up, ffn_w_down, (layer, which))

        xs = ffn(xs, mods, seq, 0, 0)
        xc = ffn(xc, modc, rows_c, 0, 0)
        h = _norm_mod(xs, g[1], mods, 1, seq, BF16)
        hc = _norm_mod(xc, g[1], modc, 1, rows_c, BF16)
        i = layer // 2
        if layer % 2 == 0:
            filt = (hy_w1[i], hy_b1[i], hy_w2[i], hy_b2[i], hy_w3[i], hy_b3[i], hy_freq[i], hy_w_out[i])
            mix, mixc = _ab_mixer(h, hc, batch, seq, seq_c, need_ctx, ab_w_in[i], mla_q_norm_g[i],
                                  mla_kv_norm_g[i], mla_w_uq[i], mla_w_ukv[i], hy_conv_w[i], hy_conv_b[i],
                                  filt, hy_bias[i])
            w_out = ab_w_out[i].astype(BF16)
        else:
            lambda_init = 0.8 - 0.6 * math.exp(-0.3 * layer)
            mix, mixc = _diff_mixer(h, hc, batch, seq, seq_c, need_ctx, c_w_in[i], c_lambda[i], c_subln_g[i],
                                    lambda_init)
            w_out = c_w_out[i].astype(BF16)
        xs = _mm_residual(mix, w_out, xs, mods, 1, 1.0, seq, name="mixer_out_residual")
        xs = ffn(xs, mods, seq, 2, 1)
        if need_ctx:
            xc = _mm_residual(mixc, w_out, xc, modc, 1, 1.0, rows_c, name="mixer_out_residual")
            xc = ffn(xc, modc, rows_c, 2, 1)

    out = _norm_mod(xs, final_norm_g.astype(F32), None, 0, seq, x.dtype)
    return out.reshape(batch, seq, d)
```

```python
import functools
import math
from typing import NamedTuple

import numpy as np
import jax
import jax.numpy as jnp
from jax import lax
from jax.experimental import pallas as pl
from jax.experimental.pallas import tpu as pltpu

F32 = jnp.float32
BF16 = jnp.bfloat16

EPS = 1e-6
ROPE_THETA = 10000.0
GRID_W = 64
N_SUB = 3
NOPE_DIM = 128
ROPE_DIM = 64
V_DIM = 128
DIFF_DIM = 128
HY_EMB = 33
HY_TARGET = 1e-2
HY_SHORT_DECAY_PCT = 0.3
HY_LONG_DECAY_PCT = 1.5

LANES = 128
MXU_DIM_V7X = 256
VMEM_BYTES_V7X = 64 * 1024 * 1024
VMEM_COMPILER_RESERVE = 8 * 1024 * 1024
MLA_HEAD_PAD = MXU_DIM_V7X

TILE_M = 1024
TILE_N = 512
TILE_K = 4096
TILE_K_LONG = 2 * TILE_K
ATTN_TQ = 256
ATTN_CHUNK = 256


def _params(semantics, block_bytes, temp_bytes=0):
    want = 2 * block_bytes + temp_bytes
    limit = min(VMEM_BYTES_V7X - VMEM_COMPILER_RESERVE, max(want, 32 * 1024 * 1024))
    return pltpu.CompilerParams(dimension_semantics=semantics, vmem_limit_bytes=int(limit))


def _nbytes(shape, dtype):
    return math.prod(shape) * jnp.dtype(dtype).itemsize


def _rms(x, g):
    return x * lax.rsqrt(jnp.mean(x * x, axis=-1, keepdims=True) + EPS) * g


def _matmul(a, w, *, tm, tn, tk, epilogue, out_shape, out_specs, extras=(), w_index=(), w_cols=None,
            w_col_shift=None, resident_a=False, name):
    m, kdim = a.shape
    assert w.ndim == 2 + len(w_index) and w.shape[-2] == kdim
    col0, n = (0, w.shape[-1]) if w_cols is None else w_cols
    tm, tn, tk = min(tm, m), min(tn, n), min(tk, kdim)
    assert m % tm == 0 and n % tn == 0 and kdim % tk == 0 and col0 % tn == 0, (name, a.shape, w.shape, tm, tn, tk)
    j0 = col0 // tn
    nk = kdim // tk
    n_ex, n_out = len(extras), len(out_shape)

    def body(*refs):
        a_ref, w_ref = refs[0], refs[1]
        ex = refs[2:2 + n_ex]
        outs = refs[2 + n_ex:2 + n_ex + n_out]
        prod = jnp.dot(a_ref[...], w_ref[...].astype(BF16), preferred_element_type=F32)
        if nk == 1:
            epilogue(prod, ex, outs)
            return
        acc_ref = refs[-1]
        k = pl.program_id(2)

        @pl.when(k == 0)
        def _():
            acc_ref[...] = prod

        if nk > 2:
            @pl.when((k > 0) & (k < nk - 1))
            def _():
                acc_ref[...] += prod

        @pl.when(k == nk - 1)
        def _():
            epilogue(acc_ref[...] + prod, ex, outs)

    in_specs = [pl.BlockSpec((tm, tk), lambda i, j, k: (i, k), pipeline_mode=pl.Buffered(1) if resident_a else None),
                pl.BlockSpec((None,) * len(w_index) + (tk, tn),
                             lambda i, j, k: (*w_index, k, j + j0 + (0 if w_col_shift is None else w_col_shift(i))))]
    in_specs += [spec for _, spec in extras]
    block_bytes = _nbytes((tm, tk), a.dtype) + _nbytes((tk, tn), w.dtype)
    for arr, spec in extras:
        block_bytes += _nbytes([d for d in spec.block_shape if d is not None], arr.dtype)
    for sds, spec in zip(out_shape, out_specs):
        block_bytes += _nbytes([d for d in spec.block_shape if d is not None], sds.dtype)
    acc_bytes = _nbytes((tm, tn), F32)
    return pl.pallas_call(
        body,
        grid=(m // tm, n // tn, nk),
        in_specs=in_specs,
        out_specs=list(out_specs),
        out_shape=list(out_shape),
        scratch_shapes=[pltpu.VMEM((tm, tn), F32)] if nk > 1 else [],
        compiler_params=_params(("parallel", "parallel", "arbitrary"), block_bytes, 8 * acc_bytes),
        name=name,
    )(a, w, *[arr for arr, _ in extras])


def _mm_plain(a, w, out_dtype, *, tm=TILE_M, tn=TILE_N, tk=TILE_K, w_cols=None, name):
    m, n = a.shape[0], (w.shape[1] if w_cols is None else w_cols[1])
    tm, tn = min(tm, m), min(tn, n)

    def epilogue(acc, ex, outs):
        outs[0][...] = acc.astype(out_dtype)

    return _matmul(a, w, tm=tm, tn=tn, tk=tk, epilogue=epilogue, w_cols=w_cols,
                   out_shape=[jax.ShapeDtypeStruct((m, n), out_dtype)],
                   out_specs=[pl.BlockSpec((tm, tn), lambda i, j, k: (i, j))], name=name)[0]


def _mm_residual(a, w, res, mods, sub, coef, rows_per_group, *, tm=TILE_M, tn=TILE_N, tk=TILE_K, w_index=(),
                 resident_a=False, name):
    m, n = a.shape[0], w.shape[-1]
    tm, tn = min(tm, m, rows_per_group), min(tn, n)
    tiles_per_group = rows_per_group // tm
    gate_row = 3 * sub + 2

    def epilogue(acc, ex, outs):
        res_ref, mod_ref = ex
        gate = mod_ref[gate_row:gate_row + 1, :]
        outs[0][...] = res_ref[...] + (coef * gate) * acc

    extras = [(res, pl.BlockSpec((tm, tn), lambda i, j, k: (i, j))),
              (mods, pl.BlockSpec((None, 3 * N_SUB, tn), lambda i, j, k: (i // tiles_per_group, 0, j)))]
    return _matmul(a, w, tm=tm, tn=tn, tk=tk, epilogue=epilogue, extras=extras, w_index=w_index,
                   resident_a=resident_a, out_shape=[jax.ShapeDtypeStruct((m, n), F32)],
                   out_specs=[pl.BlockSpec((tm, tn), lambda i, j, k: (i, j))], name=name)[0]


class RopeLayout(NamedTuple):
    group: int
    lo: int
    n_freq: int
    split: bool


MLA_ROPE_LAYOUT = RopeLayout(group=MLA_HEAD_PAD, lo=NOPE_DIM, n_freq=ROPE_DIM // 4, split=False)
DIFF_ROPE_LAYOUT = RopeLayout(group=DIFF_DIM, lo=0, n_freq=DIFF_DIM // 4, split=True)


def _rope_tables(seq, layout):
    n_freq = layout.n_freq
    pos = jnp.arange(seq, dtype=jnp.int32)
    row_pos = (pos // GRID_W).astype(F32)
    col_pos = (pos % GRID_W).astype(F32)
    inv = ROPE_THETA ** (-jnp.arange(n_freq, dtype=F32) / n_freq)
    ang = jnp.concatenate([row_pos[:, None] * inv, col_pos[:, None] * inv], axis=-1)
    cos, sin = jnp.cos(ang), jnp.sin(ang)
    if layout.split:
        assert 4 * n_freq == LANES
        return jnp.concatenate([cos, cos], axis=-1), jnp.concatenate([-sin, sin], axis=-1)
    cos = jnp.repeat(cos, 2, axis=-1)
    sin = jnp.repeat(sin, 2, axis=-1) * jnp.tile(jnp.array([-1.0, 1.0], F32), 2 * n_freq)
    rest = LANES - 4 * n_freq
    return (jnp.concatenate([cos, jnp.ones((seq, rest), F32)], axis=-1),
            jnp.concatenate([sin, jnp.zeros((seq, rest), F32)], axis=-1))


def _rope_apply(x, cos_ref, sin_ref, layout):
    assert layout.lo % LANES == 0 and layout.group % LANES == 0 and x.shape[-1] % layout.group == 0
    cos, sin = cos_ref[...], sin_ref[...]
    slabs = []
    for c0 in range(0, x.shape[-1], LANES):
        slab = x[:, c0:c0 + LANES]
        if c0 % layout.group == layout.lo:
            if layout.split:
                partner = pltpu.roll(slab, LANES // 2, 1)
            else:
                even = (lax.broadcasted_iota(jnp.int32, slab.shape, 1) & 1) == 0
                partner = jnp.where(even, pltpu.roll(slab, LANES - 1, 1), pltpu.roll(slab, 1, 1))
            slab = slab * cos + partner * sin
        slabs.append(slab)
    return jnp.concatenate(slabs, axis=1) if len(slabs) > 1 else slabs[0]


def _cast_deinterleave_columns(w, n_perm):
    kdim, n = w.shape
    perm = np.concatenate([np.arange(0, LANES, 2), np.arange(1, LANES, 2)])
    p = np.zeros((LANES, LANES), np.float32)
    p[perm, np.arange(LANES)] = 1.0
    tm, tn = min(TILE_M, kdim), min(TILE_N, n)
    assert n_perm % tn == 0
    perm_blocks = n_perm // tn

    def body(w_ref, p_ref, o_ref):
        @pl.when(pl.program_id(1) < perm_blocks)
        def _():
            pm = p_ref[...]
            for c0 in range(0, tn, LANES):
                o_ref[:, c0:c0 + LANES] = jnp.dot(w_ref[:, c0:c0 + LANES].astype(BF16), pm,
                                                  preferred_element_type=F32).astype(BF16)

        @pl.when(pl.program_id(1) >= perm_blocks)
        def _():
            o_ref[...] = w_ref[...].astype(BF16)

    return pl.pallas_call(
        body,
        grid=(kdim // tm, n // tn),
        in_specs=[pl.BlockSpec((tm, tn), lambda i, j: (i, j)), pl.BlockSpec((LANES, LANES), lambda i, j: (0, 0))],
        out_specs=pl.BlockSpec((tm, tn), lambda i, j: (i, j)),
        out_shape=jax.ShapeDtypeStruct((kdim, n), BF16),
        compiler_params=_params(("parallel", "parallel"), _nbytes((tm, tn), w.dtype) + _nbytes((tm, tn), BF16)),
        name="cast_deinterleave_rotary_columns",
    )(w, jnp.asarray(p, BF16))


def _mm_rope(a, w, tables, scale, seq, out_dtype, *, layout, tm=TILE_M, tn=TILE_N, tk=TILE_K, w_cols=None, name):
    m, n = a.shape[0], (w.shape[1] if w_cols is None else w_cols[1])
    tm, tn = min(tm, m, seq), min(tn, n)
    tiles_per_seq = seq // tm

    def epilogue(acc, ex, outs):
        y = acc if tables is None else _rope_apply(acc, ex[0], ex[1], layout)
        outs[0][...] = (y * scale).astype(out_dtype)

    extras = []
    if tables is not None:
        spec = pl.BlockSpec((tm, LANES), lambda i, j, k: (i % tiles_per_seq, 0))
        extras = [(tables[0], spec), (tables[1], spec)]
    return _matmul(a, w, tm=tm, tn=tn, tk=tk, epilogue=epilogue, extras=extras, w_cols=w_cols,
                   out_shape=[jax.ShapeDtypeStruct((m, n), out_dtype)],
                   out_specs=[pl.BlockSpec((tm, tn), lambda i, j, k: (i, j))], name=name)[0]


def _modulation(cvec, mod_w, mod_b, *, tn=512):
    depth, d, n = mod_w.shape
    rows = cvec.shape[0]
    tn = min(tn, n)

    def body(c_ref, w_ref, b_ref, o_ref):
        cv = c_ref[...]
        s = cv * (1.0 / (1.0 + jnp.exp(-cv)))
        o_ref[...] = jnp.dot(s, w_ref[...], preferred_element_type=F32) + b_ref[...]

    block_bytes = _nbytes((d, tn), F32) + _nbytes((rows, d), F32) + 2 * _nbytes((rows, tn), F32)
    return pl.pallas_call(
        body,
        grid=(depth, n // tn),
        in_specs=[pl.BlockSpec((rows, d), lambda l, j: (0, 0)),
                  pl.BlockSpec((None, d, tn), lambda l, j: (l, 0, j)),
                  pl.BlockSpec((None, 1, tn), lambda l, j: (l, 0, j))],
        out_specs=pl.BlockSpec((None, rows, tn), lambda l, j: (l, 0, j)),
        out_shape=jax.ShapeDtypeStruct((depth, rows, n), F32),
        compiler_params=_params(("parallel", "parallel"), block_bytes),
        name="adaln_modulation",
    )(cvec, mod_w, mod_b.reshape(depth, 1, n))


def _norm_mod(x, g, mods, sub, rows_per_group, out_dtype, *, tm=256):
    m, d = x.shape
    tm = min(tm, m, rows_per_group)
    tiles_per_group = rows_per_group // tm

    def body(*refs):
        if mods is None:
            x_ref, g_ref, o_ref = refs
        else:
            x_ref, g_ref, mod_ref, o_ref = refs
        y = _rms(x_ref[...], g_ref[...])
        if mods is not None:
            shift = mod_ref[3 * sub:3 * sub + 1, :]
            scale = mod_ref[3 * sub + 1:3 * sub + 2, :]
            y = y * (1.0 + scale) + shift
        o_ref[...] = y.astype(out_dtype)

    in_specs = [pl.BlockSpec((tm, d), lambda i: (i, 0)), pl.BlockSpec((1, d), lambda i: (0, 0))]
    args = [x, g.reshape(1, d)]
    if mods is not None:
        in_specs.append(pl.BlockSpec((None, 3 * N_SUB, d), lambda i: (i // tiles_per_group, 0, 0)))
        args.append(mods)
    block_bytes = _nbytes((tm, d), F32) + _nbytes((tm, d), out_dtype) + _nbytes((16, d), F32)
    return pl.pallas_call(
        body,
        grid=(m // tm,),
        in_specs=in_specs,
        out_specs=pl.BlockSpec((tm, d), lambda i: (i, 0)),
        out_shape=jax.ShapeDtypeStruct((m, d), out_dtype),
        compiler_params=_params(("parallel",), block_bytes, 2 * _nbytes((tm, d), F32)),
        name="rmsnorm_modulate",
    )(*args)


def _gate_up(h, w_gate, w_up, w_index, *, tm=2 * TILE_M, tn=TILE_N, tk=TILE_K):
    m, kdim = h.shape
    n = w_gate.shape[-1]
    tn = tn * jnp.dtype(BF16).itemsize // jnp.dtype(w_gate.dtype).itemsize
    tm, tn, tk = min(tm, m), min(tn, n), min(tk, kdim)
    assert m % tm == 0 and n % tn == 0 and kdim % tk == 0
    nk = kdim // tk

    def finish(g, u, o_ref):
        o_ref[...] = (g * (1.0 / (1.0 + jnp.exp(-g))) * u).astype(o_ref.dtype)

    def body(h_ref, wg_ref, wu_ref, o_ref, *acc):
        hh = h_ref[...]
        pg = jnp.dot(hh, wg_ref[...].astype(BF16), preferred_element_type=F32)
        pu = jnp.dot(hh, wu_ref[...].astype(BF16), preferred_element_type=F32)
        if nk == 1:
            finish(pg, pu, o_ref)
            return
        accg, accu = acc
        k = pl.program_id(2)

        @pl.when(k == 0)
        def _():
            accg[...] = pg
            accu[...] = pu

        if nk > 2:
            @pl.when((k > 0) & (k < nk - 1))
            def _():
                accg[...] += pg
                accu[...] += pu

        @pl.when(k == nk - 1)
        def _():
            finish(accg[...] + pg, accu[...] + pu, o_ref)

    block_bytes = _nbytes((tm, tk), BF16) + 2 * _nbytes((tk, tn), w_gate.dtype) + _nbytes((tm, tn), BF16)
    acc_bytes = _nbytes((tm, tn), F32) + _nbytes((tk, tn), BF16)
    w_spec = pl.BlockSpec((None,) * len(w_index) + (tk, tn), lambda i, j, k: (*w_index, k, j))
    return pl.pallas_call(
        body,
        grid=(m // tm, n // tn, nk),
        in_specs=[pl.BlockSpec((tm, tk), lambda i, j, k: (i, k), pipeline_mode=pl.Buffered(1)), w_spec, w_spec],
        out_specs=pl.BlockSpec((tm, tn), lambda i, j, k: (i, j)),
        out_shape=jax.ShapeDtypeStruct((m, n), BF16),
        scratch_shapes=[pltpu.VMEM((tm, tn), F32)] * 2 if nk > 1 else [],
        compiler_params=_params(("parallel", "parallel", "arbitrary"), block_bytes, 6 * acc_bytes),
        name="ffn_gate_up",
    )(h, w_gate, w_up)


def _half_ffn(x, mods, sub, rows_per_group, g, w_gate, w_up, w_down, w_index):
    h = _norm_mod(x, g, mods, sub, rows_per_group, BF16)
    u = _gate_up(h, w_gate, w_up, w_index)
    tn = TILE_N * jnp.dtype(BF16).itemsize // jnp.dtype(w_down.dtype).itemsize
    return _mm_residual(u, w_down, x, mods, sub, 0.5, rows_per_group, w_index=w_index,
                        tm=TILE_M, tn=tn, tk=TILE_K_LONG, resident_a=True, name="ffn_down_residual")


class ScoreSet(NamedTuple):
    q: object
    k_cols: slice
    write: object
    read: object


def _lane_fold(x, op):
    return functools.reduce(op, [x[:, c0:c0 + LANES] for c0 in range(0, x.shape[-1], LANES)])


def _attention_step(sets, k_refs, v_refs, chunk):
    state = [dict(mx=None, ls=None, acc=None, m_old=s.read[1][...]) for s in sets]
    off = 0
    for k_ref, v_ref in zip(k_refs, v_refs):
        total = k_ref.shape[0]
        for c0 in range(0, total, chunk):
            rows = min(chunk, total - c0)
            cols = slice(off + c0, off + c0 + rows)
            for s, st in zip(sets, state):
                if s.write is not None:
                    new = lax.dot_general(s.q, k_ref[c0:c0 + rows, s.k_cols], (((1,), (1,)), ((), ())),
                                          preferred_element_type=F32)
                    s.write[0][:, cols] = new
                    part = _lane_fold(new, jnp.maximum)
                    st["mx"] = part if st["mx"] is None else jnp.maximum(st["mx"], part)
            for s, st in zip(sets, state):
                m_old = jnp.concatenate([st["m_old"]] * (rows // LANES), axis=1) if rows > LANES else st["m_old"]
                e = jnp.exp(s.read[0][:, cols] - m_old)
                part = _lane_fold(e, jnp.add)
                st["ls"] = part if st["ls"] is None else st["ls"] + part
                pv = jnp.dot(e.astype(BF16), v_ref[c0:c0 + rows, :], preferred_element_type=F32)
                st["acc"] = pv if st["acc"] is None else st["acc"] + pv
        off += total
    out = []
    for s, st in zip(sets, state):
        if s.write is not None:
            s.write[1][...] = jnp.broadcast_to(jnp.max(st["mx"], axis=-1, keepdims=True), st["mx"].shape)
        out.append((st["acc"], jnp.sum(st["ls"], axis=-1, keepdims=True)))
    return out


def _skewed_steps(nq, score_bufs, emit):
    i = pl.program_id(2)
    buf_a, buf_b = score_bufs

    @pl.when(i == 0)
    def _():
        for pair in buf_b:
            for ref in pair:
                ref[...] = jnp.zeros_like(ref)

    even = lax.rem(i, 2) == 0

    @pl.when((i < nq) & even)
    def _():
        emit(buf_a, buf_b)

    @pl.when((i < nq) & jnp.logical_not(even))
    def _():
        emit(buf_b, buf_a)

    @pl.when(i == nq)
    def _():
        emit(None, buf_a if (nq - 1) % 2 == 0 else buf_b)


def _skewed_maps(nq):
    return (lambda bb, h, i: (bb, jnp.minimum(i, nq - 1), h)), (lambda bb, h, i: (bb, jnp.maximum(i - 1, 0), h))


def _kv_specs(arrays, width):
    return [pl.BlockSpec((None, a.shape[1], width), lambda bb, h, i: (bb, 0, h)) for a in arrays]


def _mla_attention(q, ks, vs, heads, *, tq=2 * ATTN_TQ):
    b, sq, _ = q.shape
    sk = sum(k.shape[1] for k in ks)
    tq = min(tq, sq)
    nkv = len(ks)

    nq = sq // tq
    q_map, o_map = _skewed_maps(nq)

    def body(q_ref, *refs):
        k_refs, v_refs = refs[:nkv], refs[nkv:2 * nkv]
        o_ref = refs[2 * nkv]
        s_a, m_a, s_b, m_b = refs[2 * nkv + 1:]

        def emit(write, read):
            sets = [ScoreSet(q_ref[...], slice(None), None if write is None else write[0], read[0])]
            (o, l), = _attention_step(sets, k_refs, v_refs, ATTN_CHUNK)
            o_ref[...] = (o / l).astype(o_ref.dtype)

        _skewed_steps(nq, ([(s_a, m_a)], [(s_b, m_b)]), emit)

    block_bytes = (_nbytes((tq, MLA_HEAD_PAD), BF16) + _nbytes((sk, MLA_HEAD_PAD), BF16)
                   + _nbytes((sk, V_DIM), BF16) + _nbytes((tq, V_DIM), BF16))
    score_bytes = _nbytes((tq, sk), F32)
    return pl.pallas_call(
        body,
        grid=(b, heads, nq + 1),
        in_specs=([pl.BlockSpec((None, tq, MLA_HEAD_PAD), q_map)]
                  + _kv_specs(ks, MLA_HEAD_PAD) + _kv_specs(vs, V_DIM)),
        out_specs=pl.BlockSpec((None, tq, V_DIM), o_map),
        out_shape=jax.ShapeDtypeStruct((b, sq, heads * V_DIM), BF16),
        scratch_shapes=[pltpu.VMEM((tq, sk), F32), pltpu.VMEM((tq, LANES), F32)] * 2,
        compiler_params=_params(("parallel", "parallel", "arbitrary"), block_bytes, 4 * score_bytes),
        name="mla_attention",
    )(q, *ks, *vs)


def _diff_attention(q, ks, vs, lam_p, subln_g, lambda_init, heads, *, tq=ATTN_TQ):
    b, sq, _ = q.shape
    sk = sum(k.shape[1] for k in ks)
    tq = min(tq, sq)
    hw = 2 * DIFF_DIM
    nkv = len(ks)
    nq = sq // tq
    q_map, o_map = _skewed_maps(nq)

    def body(lam_ref, g_ref, q_ref, *refs):
        k_refs, v_refs = refs[:nkv], refs[nkv:2 * nkv]
        o_ref = refs[2 * nkv]
        bufs = refs[2 * nkv + 1:]
        set_a = [(bufs[0], bufs[1]), (bufs[2], bufs[3])]
        set_b = [(bufs[4], bufs[5]), (bufs[6], bufs[7])]

        def emit(write, read):
            halves = [(q_ref[:, :DIFF_DIM], slice(0, DIFF_DIM)), (q_ref[:, DIFF_DIM:], slice(DIFF_DIM, hw))]
            sets = [ScoreSet(qh, cols, None if write is None else write[n], read[n])
                    for n, (qh, cols) in enumerate(halves)]
            (o1, l1), (o2, l2) = _attention_step(sets, k_refs, v_refs, ATTN_CHUNK)
            lp = lam_ref[...]
            lam = (jnp.exp(jnp.sum(lp[0:1] * lp[1:2], axis=-1, keepdims=True))
                   - jnp.exp(jnp.sum(lp[2:3] * lp[3:4], axis=-1, keepdims=True)) + lambda_init)
            o = o1 / l1 - (lam / l2) * o2
            o_ref[...] = (_rms(o, g_ref[...]) * (1.0 - lambda_init)).astype(o_ref.dtype)

        _skewed_steps(nq, (set_a, set_b), emit)

    block_bytes = 2 * _nbytes((tq, hw), BF16) + 2 * _nbytes((sk, hw), BF16)
    score_bytes = _nbytes((tq, sk), F32)
    return pl.pallas_call(
        body,
        grid=(b, heads, nq + 1),
        in_specs=([pl.BlockSpec((4, DIFF_DIM), lambda bb, h, i: (0, 0)),
                   pl.BlockSpec((1, hw), lambda bb, h, i: (0, 0)),
                   pl.BlockSpec((None, tq, hw), q_map)]
                  + _kv_specs(ks, hw) + _kv_specs(vs, hw)),
        out_specs=pl.BlockSpec((None, tq, hw), o_map),
        out_shape=jax.ShapeDtypeStruct((b, sq, heads * hw), BF16),
        scratch_shapes=[pltpu.VMEM((tq, sk), F32), pltpu.VMEM((tq, LANES), F32)] * 4,
        compiler_params=_params(("parallel", "parallel", "arbitrary"), block_bytes, 6 * score_bytes),
        name="diff_attention",
    )(lam_p.astype(F32), subln_g.reshape(1, hw).astype(F32), q, *ks, *vs)


def _hyena_filters(seq, w1, b1, w2, b2, w3, b3, freq, w_out, *, tl=256):
    hid = w1.shape[1]
    width2 = w_out.shape[1]
    width = width2 // 2
    bands = (HY_EMB - 1) // 2
    t = jnp.linspace(0.0, 1.0, seq, dtype=F32)[:, None]
    ang = ((2.0 * math.pi / seq) * jnp.arange(seq, dtype=F32)[:, None]
           * jnp.linspace(1e-4, bands - 1, bands, dtype=F32)[None, :])
    z = jnp.concatenate([t, jnp.cos(ang), -jnp.sin(ang), jnp.zeros((seq, LANES - HY_EMB), F32)], axis=-1)
    z = jnp.concatenate([z[0::2], z[1::2]], axis=0)
    deltas =jnp.abs(jnp.linspace(math.log(HY_TARGET) / HY_LONG_DECAY_PCT,
                                  math.log(HY_TARGET) / HY_SHORT_DECAY_PCT, width, dtype=F32))
    deltas2 = jnp.concatenate([deltas, deltas])[None, :]

    def pad2(w):
        return jnp.pad(w.astype(F32), ((0, LANES - w.shape[0]), (0, LANES - w.shape[1])))

    def pad_row(v):
        return jnp.pad(v.astype(F32), (0, LANES - v.shape[0]))[None, :]

    w_out_p = jnp.pad(w_out.astype(F32), ((0, LANES - hid), (0, 0)))
    tl, tn = min(tl, seq), width2
    exact = lax.Precision.HIGHEST

    def body(z_ref, w1_ref, b1_ref, w2_ref, b2_ref, w3_ref, b3_ref, f_ref, wo_ref, d_ref, o_ref, mid_ref):
        zz, f = z_ref[...], f_ref[...]
        a = jnp.sin(f * (jnp.dot(zz, w1_ref[...], precision=exact, preferred_element_type=F32) + b1_ref[...]))
        a = jnp.sin(f * (jnp.dot(a, w2_ref[...], precision=exact, preferred_element_type=F32) + b2_ref[...]))
        a = jnp.sin(f * (jnp.dot(a, w3_ref[...], precision=exact, preferred_element_type=F32) + b3_ref[...]))
        h = jnp.dot(a, wo_ref[...], precision=exact, preferred_element_type=F32)
        h = h * jnp.exp(-zz[:, 0:1] * d_ref[...])
        row = pl.program_id(0) * tl + lax.broadcasted_iota(jnp.int32, (tl, width), 0)
        h_fwd = h[:, :width]
        h_bwd = jnp.where(row == 0, 0.0, h[:, width:])
        h_sum = h_fwd + h_bwd
        o_ref[:, :width] = h_sum.astype(o_ref.dtype)
        o_ref[:, width:] = (h_fwd - h_bwd).astype(o_ref.dtype)
        signed = jnp.where(row >= seq // 2, 0.0, jnp.where((row & 1) == 0, h_sum, -h_sum))
        alternating = jnp.sum(signed, axis=0, keepdims=True)

        @pl.when(pl.program_id(0) == 0)
        def _():
            mid_ref[...] = alternating

        @pl.when(pl.program_id(0) > 0)
        def _():
            mid_ref[...] += alternating

    sq = pl.BlockSpec((LANES, LANES), lambda i: (0, 0))
    vec = pl.BlockSpec((1, LANES), lambda i: (0, 0))
    block_bytes = _nbytes((tl, LANES), F32) + _nbytes((LANES, tn), F32) + 2 * _nbytes((tl, tn), F32)
    return pl.pallas_call(
        body,
        grid=(seq // tl,),
        in_specs=[pl.BlockSpec((tl, LANES), lambda i: (i, 0)), sq, vec, sq, vec, sq, vec, vec,
                  pl.BlockSpec((LANES, tn), lambda i: (0, 0)),
                  pl.BlockSpec((1, tn), lambda i: (0, 0))],
        out_specs=[pl.BlockSpec((tl, tn), lambda i: (i, 0)), pl.BlockSpec((1, width), lambda i: (0, 0))],
        out_shape=[jax.ShapeDtypeStruct((seq, width2), BF16), jax.ShapeDtypeStruct((1, width), F32)],
        compiler_params=_params(("arbitrary",), block_bytes, 4 * _nbytes((tl, tn), F32)),
        name="hyena_filters",
    )(z, pad2(w1), pad_row(b1), pad2(w2), pad_row(b2), pad2(w3), pad_row(b3), pad_row(freq), w_out_p, deltas2)


def _hyena_prep(zh, conv_w, conv_b, *, tc=128):
    b, seq, w3 = zh.shape
    width = w3 // 3
    half = seq // 2
    tc = min(tc, width)
    nw = width // tc
    z2 = zh.reshape(b, half, 2 * w3)

    def body(e0_ref, e1_ref, e2_ref, o0_ref, o1_ref, o2_ref, w0_ref, w1_ref, w2_ref, b0_ref, b1_ref, b2_ref,
             x0_ref, vx_ref):
        row = lax.broadcasted_iota(jnp.int32, (half, 1), 0)

        def conv(e_ref, o_ref, w_ref, b_ref):
            ev, od = e_ref[...], o_ref[...]
            w, bias = w_ref[...], b_ref[...]
            before_even = jnp.where(row == 0, 0.0, pltpu.roll(od, 1, 0))
            after_odd = jnp.where(row == half - 1, 0.0, pltpu.roll(ev, half - 1, 0))
            return (before_even * w[0:1] + ev * w[1:2] + od * w[2:3] + bias,
                    ev * w[0:1] + od * w[1:2] + after_odd * w[2:3] + bias)

        x0 = conv(e0_ref, o0_ref, w0_ref, b0_ref)
        x1 = conv(e1_ref, o1_ref, w1_ref, b1_ref)
        v = conv(e2_ref, o2_ref, w2_ref, b2_ref)
        for parity in range(2):
            x0_ref[parity] = x0[parity]
            vx_ref[parity] = (v[parity] * x1[parity]).astype(vx_ref.dtype)

    def zspec(parity, part):
        return pl.BlockSpec((None, half, tc), lambda bb, j: (bb, 0, (3 * parity + part) * nw + j))

    def wspec(rows, part):
        return pl.BlockSpec((rows, tc), lambda bb, j: (0, part * nw + j))

    out_spec = pl.BlockSpec((2, half, tc), lambda bb, j: (0, 0, bb * nw + j))
    block_bytes = 4 * _nbytes((seq, tc), F32) + _nbytes((seq, tc), BF16)
    return pl.pallas_call(
        body,
        grid=(b, nw),
        in_specs=[zspec(0, 0), zspec(0, 1), zspec(0, 2), zspec(1, 0), zspec(1, 1), zspec(1, 2),
                  wspec(3, 0), wspec(3, 1), wspec(3, 2), wspec(1, 0), wspec(1, 1), wspec(1, 2)],
        out_specs=[out_spec, out_spec],
        out_shape=[jax.ShapeDtypeStruct((2, half, b * width), F32),
                   jax.ShapeDtypeStruct((2, half, b * width), BF16)],
        compiler_params=_params(("parallel", "parallel"), block_bytes, 8 * _nbytes((seq, tc), F32)),
        name="hyena_short_conv",
    )(z2, z2, z2, z2, z2, z2, conv_w, conv_w, conv_w, conv_b[None, :], conv_b[None, :], conv_b[None, :])


def _dft_matrices(seq):
    n = 2 * seq
    half = seq // 2
    assert seq % 2 == 0 and n % 4 == 0
    idx = jnp.arange(half, dtype=jnp.int32)
    alt = jnp.where((idx & 1) == 0, 1.0, -1.0).astype(F32)

    def cos_sin(k, s):
        phase = _mod_nonneg(k * s, n)
        return _cos_turns(phase, n), _cos_turns(_mod_nonneg(phase + (n - n // 4), n), n)

    def forward(sample):
        c, s = cos_sin(idx[:, None], sample[None, :])
        return jnp.concatenate([c, jnp.where(idx[:, None] == 0, alt[None, :], s)], axis=0).astype(BF16)

    def inverse(sample):
        c, s = cos_sin(idx[None, :], sample[:, None])
        return jnp.concatenate([c, jnp.where(idx[None, :] == 0, alt[:, None], s)], axis=1).astype(BF16)

    return forward(2 * idx), forward(2 * idx + 1), inverse(2 * idx), inverse(2 * idx + 1)


def _div_nonneg(x, d):
    return x >> (d.bit_length() - 1) if d & (d - 1) == 0 else x // d


def _mod_nonneg(x, d):
    return x & (d - 1) if d & (d - 1) == 0 else x % d


def _cos_turns(phase, n):
    quarter = n // 4
    quad = _div_nonneg(phase, quarter)
    rem = phase - quad * quarter
    odd = (quad & 1) == 1
    x = jnp.where(odd, quarter - rem, rem).astype(F32) * (2.0 * math.pi / n)
    x2 = x * x
    acc = jnp.full_like(x2, 1.0 / math.factorial(16))
    for order in range(14, -1, -2):
        acc = acc * (-x2) + 1.0 / math.factorial(order)
    return jnp.where((quad == 1) | (quad == 2), -acc, acc)


def _spectrum_multiply(eu, ou, eg, og, mid_g, batch, width, *, tr=256, tc=512):
    seq = eu.shape[0]
    half = seq // 2
    n = 2 * seq
    tr, tc = min(tr, half), min(tc, width)
    nwc = width // tc

    def body(eu_ref, ou_ref, eg_ref, og_ref, mid_ref, pe_ref, po_ref):
        row = pl.program_id(0) * tr + lax.broadcasted_iota(jnp.int32, (tr, tc), 0)
        first = row == 0

        def pair(e_ref, o_ref):
            ec, es, oc, os_ = e_ref[0], e_ref[1], o_ref[0], o_ref[1]
            return (ec + oc, jnp.where(first, 0.0, es + os_)), (ec - oc, jnp.where(first, 0.0, os_ - es))

        def product(u, g):
            return u[0] * g[0] - u[1] * g[1], u[0] * g[1] + u[1] * g[0]

        (u_lo, u_hi), (g_lo, g_hi) = pair(eu_ref, ou_ref), pair(eg_ref, og_ref)
        ya_lo, yb_lo = product(u_lo, g_lo)
        ya_hi, yb_hi = product(u_hi, g_hi)
        ya_mid, yb_mid = product((eu_ref[1], ou_ref[1]), (mid_ref[...], og_ref[1]))
        w_cos = jnp.where(first, 1.0 / n, 2.0 / n)
        w_sin = 2.0 / n
        pe_ref[0] = (w_cos * (ya_lo + ya_hi)).astype(pe_ref.dtype)
        po_ref[0] = (w_cos * (ya_lo - ya_hi)).astype(po_ref.dtype)
        pe_ref[1] = (w_sin * jnp.where(first, ya_mid, yb_lo - yb_hi)).astype(pe_ref.dtype)
        po_ref[1] = (w_sin * jnp.where(first, yb_mid, yb_lo + yb_hi)).astype(po_ref.dtype)

    u_spec = pl.BlockSpec((2, tr, tc), lambda i, j, bb: (0, i, bb * nwc + j))
    g_spec = pl.BlockSpec((2, tr, tc), lambda i, j, bb: (0, i, j))
    block_bytes = 4 * _nbytes((2, tr, tc), F32) + 2 * _nbytes((2, tr, tc), BF16)
    out_sds = jax.ShapeDtypeStruct((2, half, batch * width), BF16)
    pe, po = pl.pallas_call(
        body,
        grid=(half // tr, nwc, batch),
        in_specs=[u_spec, u_spec, g_spec, g_spec, pl.BlockSpec((1, tc), lambda i, j, bb: (0, j))],
        out_specs=[u_spec, u_spec],
        out_shape=[out_sds, out_sds],
        compiler_params=_params(("parallel", "parallel", "arbitrary"), block_bytes, 24 * _nbytes((tr, tc), F32)),
        name="hyena_spectrum_multiply",
    )(eu.reshape(2, half, batch * width), ou.reshape(2, half, batch * width),
      eg.reshape(2, half, width), og.reshape(2, half, width), mid_g)
    return pe.reshape(seq, batch * width), po.reshape(seq, batch * width)


def _hyena_branch(zh, conv_w, conv_b, filt, hy_bias):
    b, seq, w3 = zh.shape
    width = w3 // 3
    half = seq // 2
    x0, vx = _hyena_prep(zh, conv_w.astype(F32), conv_b.astype(F32))
    h_sum_dif, mid_g = _hyena_filters(seq, *filt)
    h_sum_dif = h_sum_dif.reshape(2, half, 2 * width)
    mats = _dft_matrices(seq)
    forward_mats, inverse_mats = mats[:2], mats[2:]
    tm, tn = min(TILE_M, half), min(TILE_N, width)
    nw = width // tn

    def store(acc, ex, outs):
        outs[0][...] = acc

    def forward(mat, samples, parity, cols, col_shift, name):
        return _matmul(mat, samples, tm=tm, tn=tn, tk=TILE_K, epilogue=store, w_index=(parity,),
                       w_cols=(0, cols), w_col_shift=col_shift,
                       out_shape=[jax.ShapeDtypeStruct((seq, cols), F32)],
                       out_specs=[pl.BlockSpec((tm, tn), lambda i, j, k: (i, j))], name=name)[0]

    def filter_cols(i):
        return (i // (half // tm)) * nw

    data = [forward(m, vx, p, b * width, None, "hyena_dft_forward") for p, m in enumerate(forward_mats)]
    filt_spec = [forward(m, h_sum_dif, p, width, filter_cols, "hyena_dft_filters")
                 for p, m in enumerate(forward_mats)]
    spectra = _spectrum_multiply(data[0], data[1], filt_spec[0], filt_spec[1], mid_g, b, width)

    def epilogue(acc, ex, outs):
        x0_ref, vx_ref, bias_ref = ex
        outs[0][...] = (x0_ref[...] * (acc + bias_ref[...] * vx_ref[...].astype(F32))).astype(BF16)

    def inverse(mat, spectrum, parity):
        extras = [(x0, pl.BlockSpec((None, tm, tn), lambda i, j, k: (parity, i, j))),
                  (vx, pl.BlockSpec((None, tm, tn), lambda i, j, k: (parity, i, j))),
                  (hy_bias.astype(F32)[None, :], pl.BlockSpec((1, tn), lambda i, j, k: (0, j % nw)))]
        return _matmul(mat, spectrum, tm=tm, tn=tn, tk=TILE_K, epilogue=epilogue, extras=extras,
                       out_shape=[jax.ShapeDtypeStruct((b, half, width), BF16)],
                       out_specs=[pl.BlockSpec((None, tm, tn), lambda i, j, k: (j // nw, i, j % nw))],
                       name="hyena_dft_inverse")[0]

    hy_even, hy_odd = (inverse(m, s, p) for p, (m, s) in enumerate(zip(inverse_mats, spectra)))
    return jnp.concatenate([hy_even, hy_odd], axis=-1).reshape(b, seq, width)


def _ab_latents(h, w1, q_norm_g, kv_norm_g, tables, seq, q_lora, kv_lora, *, tm=512, tk=1024):
    m = h.shape[0]
    n1 = w1.shape[1]
    tm = min(tm, m, seq)
    tiles_per_seq = seq // tm

    def epilogue(acc, ex, outs):
        outs[0][...] = _rms(acc[:, :q_lora], ex[0][...]).astype(BF16)
        outs[1][...] = _rms(acc[:, q_lora:q_lora + kv_lora], ex[1][...]).astype(BF16)
        kr = acc[:, q_lora + kv_lora:]
        if tables is not None:
            kr = _rope_apply(kr, ex[2], ex[3], MLA_ROPE_LAYOUT)
        outs[2][...] = kr

    extras = [(q_norm_g.astype(F32)[None, :], pl.BlockSpec((1, q_lora), lambda i, j, k: (0, 0))),
              (kv_norm_g.astype(F32)[None, :], pl.BlockSpec((1, kv_lora), lambda i, j, k: (0, 0)))]
    if tables is not None:
        spec = pl.BlockSpec((tm, LANES), lambda i, j, k: (i % tiles_per_seq, 0))
        extras += [(tables[0], spec), (tables[1], spec)]
    widths = (q_lora, kv_lora, MLA_HEAD_PAD)
    dtypes = (BF16, BF16, F32)
    return _matmul(h, w1, tm=tm, tn=n1, tk=tk, epilogue=epilogue, extras=extras,
                   out_shape=[jax.ShapeDtypeStruct((m, wd), dt) for wd, dt in zip(widths, dtypes)],
                   out_specs=[pl.BlockSpec((tm, wd), lambda i, j, k: (i, 0)) for wd in widths],
                   name="mla_latent_projection")


def _mla_keys(ckv, w_uk, kr, heads, *, tm=1024, tn=1024):
    m = ckv.shape[0]
    n = heads * MLA_HEAD_PAD
    tm, tn = min(tm, m), min(tn, n)
    reps = tn // MLA_HEAD_PAD

    def epilogue(acc, ex, outs):
        rot = ex[0][...]
        if reps > 1:
            rot = jnp.concatenate([rot] * reps, axis=1)
        outs[0][...] = (acc + rot).astype(BF16)

    extras = [(kr, pl.BlockSpec((tm, MLA_HEAD_PAD), lambda i, j, k: (i, 0)))]
    return _matmul(ckv, w_uk, tm=tm, tn=tn, tk=ckv.shape[1], epilogue=epilogue, extras=extras,
                   out_shape=[jax.ShapeDtypeStruct((m, n), BF16)],
                   out_specs=[pl.BlockSpec((tm, tn), lambda i, j, k: (i, j))], name="mla_key_up")[0]


def _ab_mixer(h, hc, batch, seq, seq_c, need_ctx, w_in, q_norm_g, kv_norm_g, w_uq, w_ukv,
              conv_w, conv_b, filt, hy_bias):
    d = w_in.shape[0]
    heads = d // (2 * V_DIM)
    q_lora, kv_lora = w_uq.shape[0], w_ukv.shape[0]
    kv_end = q_lora + kv_lora + ROPE_DIM
    mla_scale = (NOPE_DIM + ROPE_DIM) ** -0.5
    zero_pad = MLA_HEAD_PAD - NOPE_DIM - ROPE_DIM

    w1 = jnp.concatenate([w_in[:, :q_lora + kv_lora], jnp.zeros((d, NOPE_DIM), w_in.dtype),
                          w_in[:, q_lora + kv_lora:kv_end], jnp.zeros((d, zero_pad), w_in.dtype)],
                         axis=1).astype(BF16)
    w_hy = w_in[:, kv_end:].astype(BF16)
    uq = w_uq.reshape(q_lora, heads, NOPE_DIM + ROPE_DIM)
    w_uq_p = jnp.pad(uq, ((0, 0), (0, 0), (0, zero_pad))).reshape(q_lora, heads * MLA_HEAD_PAD).astype(BF16)
    ukv = w_ukv.reshape(kv_lora, heads, NOPE_DIM + V_DIM)
    w_uk_p = jnp.pad(ukv[..., :NOPE_DIM], ((0, 0), (0, 0), (0, MLA_HEAD_PAD - NOPE_DIM))
                     ).reshape(kv_lora, heads * MLA_HEAD_PAD).astype(BF16)
    w_uv = ukv[..., NOPE_DIM:].reshape(kv_lora, heads * V_DIM).astype(BF16)

    tables = _rope_tables(seq, MLA_ROPE_LAYOUT)
    rope_kw = dict(layout=MLA_ROPE_LAYOUT)

    cq, ckv, kr = _ab_latents(h, w1, q_norm_g, kv_norm_g, tables, seq, q_lora, kv_lora)
    cqc, ckvc, krc = _ab_latents(hc, w1, q_norm_g, kv_norm_g, None, seq_c, q_lora, kv_lora)
    q = _mm_rope(cq, w_uq_p, tables, mla_scale, seq, BF16, name="mla_query_up", **rope_kw)
    k = _mla_keys(ckv, w_uk_p, kr, heads)
    v = _mm_plain(ckv, w_uv, BF16, name="mla_value_up")
    kc = _mla_keys(ckvc, w_uk_p, krc, heads)
    vc = _mm_plain(ckvc, w_uv, BF16, name="mla_value_up")

    def split(t, s):
        return t.reshape(batch, s, t.shape[-1])

    att = _mla_attention(split(q, seq), [split(kc, seq_c), split(k, seq)], [split(vc, seq_c), split(v, seq)], heads)
    zh = _mm_plain(h, w_hy, F32, tn=512, name="hyena_in_projection")
    hy = _hyena_branch(split(zh, seq), conv_w, conv_b, filt, hy_bias)
    mix = jnp.concatenate([att, hy], axis=-1).reshape(batch * seq, -1)
    if not need_ctx:
        return mix, None
    qc = _mm_rope(cqc, w_uq_p, None, mla_scale, seq_c, BF16, name="mla_query_up", **rope_kw)
    attc = _mla_attention(split(qc, seq_c), [split(kc, seq_c)], [split(vc, seq_c)], heads)
    zhc = _mm_plain(hc, w_hy, F32, tn=512, name="hyena_in_projection")
    hyc = _hyena_branch(split(zhc, seq_c), conv_w, conv_b, filt, hy_bias)
    return mix, jnp.concatenate([attc, hyc], axis=-1).reshape(batch * seq_c, -1)


def _diff_mixer(h, hc, batch, seq, seq_c, need_ctx, w_in, lam_p, subln_g, lambda_init):
    d = w_in.shape[0]
    hd = w_in.shape[1] // 3
    heads = hd // (2 * DIFF_DIM)
    scale = DIFF_DIM ** -0.5
    w = _cast_deinterleave_columns(w_in, 2 * hd)
    q_cols, k_cols, v_cols = (0, hd), (hd, hd), (2 * hd, hd)
    tables = _rope_tables(seq, DIFF_ROPE_LAYOUT)
    rope_kw = dict(layout=DIFF_ROPE_LAYOUT)

    q = _mm_rope(h, w, tables, scale, seq, BF16, w_cols=q_cols, name="diff_query_projection", **rope_kw)
    k = _mm_rope(h, w, tables, 1.0, seq, BF16, w_cols=k_cols, name="diff_key_projection", **rope_kw)
    v = _mm_plain(h, w, BF16, w_cols=v_cols, name="diff_value_projection")
    kc = _mm_rope(hc, w, None, 1.0, seq_c, BF16, w_cols=k_cols, name="diff_key_projection", **rope_kw)
    vc = _mm_plain(hc, w, BF16, w_cols=v_cols, name="diff_value_projection")

    def split(t, s):
        return t.reshape(batch, s, t.shape[-1])

    o = _diff_attention(split(q, seq), [split(kc, seq_c), split(k, seq)], [split(vc, seq_c), split(v, seq)],
                        lam_p, subln_g, lambda_init, heads)
    mix = o.reshape(batch * seq, hd)
    if not need_ctx:
        return mix, None
    qc = _mm_rope(hc, w, None, scale, seq_c, BF16, w_cols=q_cols, name="diff_query_projection", **rope_kw)
    oc = _diff_attention(split(qc, seq_c), [split(kc, seq_c)], [split(vc, seq_c)], lam_p, subln_g, lambda_init, heads)
    return mix, oc.reshape(batch * seq_c, hd)


def kernel(x, c, ctx, c_ctx, mod_w, mod_b, norm_g, ffn_w_gate, ffn_w_up, ffn_w_down, ab_w_in, mla_q_norm_g, mla_kv_norm_g, mla_w_uq, mla_w_ukv, hy_conv_w, hy_conv_b, hy_w1, hy_b1, hy_w2, hy_b2, hy_w3, hy_b3, hy_freq, hy_w_out, hy_bias, ab_w_out, c_w_in, c_lambda, c_subln_g, c_w_out, final_norm_g):
    batch, seq, d = x.shape
    seq_c = ctx.shape[1]
    depth = mod_w.shape[0]
    rows_c = batch * seq_c
    xs = x.reshape(batch * seq, d).astype(F32)
    xc = ctx.reshape(rows_c, d).astype(F32)

    cvec = jnp.concatenate([c.astype(F32), c_ctx.astype(F32)[None, :]], axis=0)
    cvec = jnp.pad(cvec, ((0, -cvec.shape[0] % 8), (0, 0)))
    mods_all = _modulation(cvec, mod_w.astype(F32), mod_b.astype(F32))

    for layer in range(depth):
        need_ctx = layer < depth - 1
        mods = mods_all[layer, :batch].reshape(batch, 3 * N_SUB, d)
        modc = mods_all[layer, batch:batch + 1].reshape(1, 3 * N_SUB, d)
        g = norm_g[layer].astype(F32)

        def ffn(t, m, rows, sub, which):
            return _half_ffn(t, m, sub, rows, g[sub], ffn_w_gate, ffn_w_up, ffn_w_down, (layer, which))

        xs = ffn(xs, mods, seq, 0, 0)
        xc = ffn(xc, modc, rows_c, 0, 0)
        h = _norm_mod(xs, g[1], mods, 1, seq, BF16)
        hc = _norm_mod(xc, g[1], modc, 1, rows_c, BF16)
        i = layer // 2
        if layer % 2 == 0:
            filt = (hy_w1[i], hy_b1[i], hy_w2[i], hy_b2[i], hy_w3[i], hy_b3[i], hy_freq[i], hy_w_out[i])
            mix, mixc = _ab_mixer(h, hc, batch, seq, seq_c, need_ctx, ab_w_in[i], mla_q_norm_g[i],
                                  mla_kv_norm_g[i], mla_w_uq[i], mla_w_ukv[i], hy_conv_w[i], hy_conv_b[i],
                                  filt, hy_bias[i])
            w_out = ab_w_out[i].astype(BF16)
        else:
            lambda_init = 0.8 - 0.6 * math.exp(-0.3 * layer)
            mix, mixc = _diff_mixer(h, hc, batch, seq, seq_c, need_ctx, c_w_in[i], c_lambda[i], c_subln_g[i],
                                    lambda_init)
            w_out = c_w_out[i].astype(BF16)
        xs = _mm_residual(mix, w_out, xs, mods, 1, 1.0, seq, name="mixer_out_residual")
        xs = ffn(xs, mods, seq, 2, 1)
        if need_ctx:
            xc = _mm_residual(mixc, w_out, xc, modc, 1, 1.0, rows_c, name="mixer_out_residual")
            xc = ffn(xc, modc, rows_c, 2, 1)

    out = _norm_mod(xs, final_norm_g.astype(F32), None, 0, seq, x.dtype)
    return out.reshape(batch, seq, d)
```

```python
import functools
import math
from typing import NamedTuple

import numpy as np
import jax
import jax.numpy as jnp
from jax import lax
from jax.experimental import pallas as pl
from jax.experimental.pallas import tpu as pltpu

F32 = jnp.float32
BF16 = jnp.bfloat16

EPS = 1e-6
ROPE_THETA = 10000.0
GRID_W = 64
N_SUB = 3
NOPE_DIM = 128
ROPE_DIM = 64
V_DIM = 128
DIFF_DIM = 128
HY_EMB = 33
HY_TARGET = 1e-2
HY_SHORT_DECAY_PCT = 0.3
HY_LONG_DECAY_PCT = 1.5

LANES = 128
MXU_DIM_V7X = 256
VMEM_BYTES_V7X = 64 * 1024 * 1024
VMEM_COMPILER_RESERVE = 8 * 1024 * 1024
MLA_HEAD_PAD = MXU_DIM_V7X

TILE_M = 1024
TILE_N = 512
TILE_K = 4096
TILE_K_LONG = 2 * TILE_K
ATTN_TQ = 256
ATTN_CHUNK = 256


def _params(semantics, block_bytes, temp_bytes=0):
    want = 2 * block_bytes + temp_bytes
    limit = min(VMEM_BYTES_V7X - VMEM_COMPILER_RESERVE, max(want, 32 * 1024 * 1024))
    return pltpu.CompilerParams(dimension_semantics=semantics, vmem_limit_bytes=int(limit))


def _nbytes(shape, dtype):
    return math.prod(shape) * jnp.dtype(dtype).itemsize


def _rms(x, g):
    return x * lax.rsqrt(jnp.mean(x * x, axis=-1, keepdims=True) + EPS) * g


def _matmul(a, w, *, tm, tn, tk, epilogue, out_shape, out_specs, extras=(), w_index=(), w_cols=None,
            w_col_shift=None, resident_a=False, name):
    m, kdim = a.shape
    assert w.ndim == 2 + len(w_index) and w.shape[-2] == kdim
    col0, n = (0, w.shape[-1]) if w_cols is None else w_cols
    tm, tn, tk = min(tm, m), min(tn, n), min(tk, kdim)
    assert m % tm == 0 and n % tn == 0 and kdim % tk == 0 and col0 % tn == 0, (name, a.shape, w.shape, tm, tn, tk)
    j0 = col0 // tn
    nk = kdim // tk
    n_ex, n_out = len(extras), len(out_shape)

    def body(*refs):
        a_ref, w_ref = refs[0], refs[1]
        ex = refs[2:2 + n_ex]
        outs = refs[2 + n_ex:2 + n_ex + n_out]
        prod = jnp.dot(a_ref[...], w_ref[...].astype(BF16), preferred_element_type=F32)
        if nk == 1:
            epilogue(prod, ex, outs)
            return
        acc_ref = refs[-1]
        k = pl.program_id(2)

        @pl.when(k == 0)
        def _():
            acc_ref[...] = prod

        if nk > 2:
            @pl.when((k > 0) & (k < nk - 1))
            def _():
                acc_ref[...] += prod

        @pl.when(k == nk - 1)
        def _():
            epilogue(acc_ref[...] + prod, ex, outs)

    in_specs = [pl.BlockSpec((tm, tk), lambda i, j, k: (i, k), pipeline_mode=pl.Buffered(1) if resident_a else None),
                pl.BlockSpec((None,) * len(w_index) + (tk, tn),
                             lambda i, j, k: (*w_index, k, j + j0 + (0 if w_col_shift is None else w_col_shift(i))))]
    in_specs += [spec for _, spec in extras]
    block_bytes = _nbytes((tm, tk), a.dtype) + _nbytes((tk, tn), w.dtype)
    for arr, spec in extras:
        block_bytes += _nbytes([d for d in spec.block_shape if d is not None], arr.dtype)
    for sds, spec in zip(out_shape, out_specs):
        block_bytes += _nbytes([d for d in spec.block_shape if d is not None], sds.dtype)
    acc_bytes = _nbytes((tm, tn), F32)
    return pl.pallas_call(
        body,
        grid=(m // tm, n // tn, nk),
        in_specs=in_specs,
        out_specs=list(out_specs),
        out_shape=list(out_shape),
        scratch_shapes=[pltpu.VMEM((tm, tn), F32)] if nk > 1 else [],
        compiler_params=_params(("parallel", "parallel", "arbitrary"), block_bytes, 8 * acc_bytes),
        name=name,
    )(a, w, *[arr for arr, _ in extras])


def _mm_plain(a, w, out_dtype, *, tm=TILE_M, tn=TILE_N, tk=TILE_K, w_cols=None, name):
    m, n = a.shape[0], (w.shape[1] if w_cols is None else w_cols[1])
    tm, tn = min(tm, m), min(tn, n)

    def epilogue(acc, ex, outs):
        outs[0][...] = acc.astype(out_dtype)

    return _matmul(a, w, tm=tm, tn=tn, tk=tk, epilogue=epilogue, w_cols=w_cols,
                   out_shape=[jax.ShapeDtypeStruct((m, n), out_dtype)],
                   out_specs=[pl.BlockSpec((tm, tn), lambda i, j, k: (i, j))], name=name)[0]


def _mm_residual(a, w, res, mods, sub, coef, rows_per_group, *, tm=TILE_M, tn=TILE_N, tk=TILE_K, w_index=(),
                 resident_a=False, name):
    m, n = a.shape[0], w.shape[-1]
    tm, tn = min(tm, m, rows_per_group), min(tn, n)
    tiles_per_group = rows_per_group // tm
    gate_row = 3 * sub + 2

    def epilogue(acc, ex, outs):
        res_ref, mod_ref = ex
        gate = mod_ref[gate_row:gate_row + 1, :]
        outs[0][...] = res_ref[...] + (coef * gate) * acc

    extras = [(res, pl.BlockSpec((tm, tn), lambda i, j, k: (i, j))),
              (mods, pl.BlockSpec((None, 3 * N_SUB, tn), lambda i, j, k: (i // tiles_per_group, 0, j)))]
    return _matmul(a, w, tm=tm, tn=tn, tk=tk, epilogue=epilogue, extras=extras, w_index=w_index,
                   resident_a=resident_a, out_shape=[jax.ShapeDtypeStruct((m, n), F32)],
                   out_specs=[pl.BlockSpec((tm, tn), lambda i, j, k: (i, j))], name=name)[0]


class RopeLayout(NamedTuple):
    group: int
    lo: int
    n_freq: int
    split: bool


MLA_ROPE_LAYOUT = RopeLayout(group=MLA_HEAD_PAD, lo=NOPE_DIM, n_freq=ROPE_DIM // 4, split=False)
DIFF_ROPE_LAYOUT = RopeLayout(group=DIFF_DIM, lo=0, n_freq=DIFF_DIM // 4, split=True)


def _rope_tables(seq, layout):
    n_freq = layout.n_freq
    pos = jnp.arange(seq, dtype=jnp.int32)
    row_pos = (pos // GRID_W).astype(F32)
    col_pos = (pos % GRID_W).astype(F32)
    inv = ROPE_THETA ** (-jnp.arange(n_freq, dtype=F32) / n_freq)
    ang = jnp.concatenate([row_pos[:, None] * inv, col_pos[:, None] * inv], axis=-1)
    cos, sin = jnp.cos(ang), jnp.sin(ang)
    if layout.split:
        assert 4 * n_freq == LANES
        return jnp.concatenate([cos, cos], axis=-1), jnp.concatenate([-sin, sin], axis=-1)
    cos = jnp.repeat(cos, 2, axis=-1)
    sin = jnp.repeat(sin, 2, axis=-1) * jnp.tile(jnp.array([-1.0, 1.0], F32), 2 * n_freq)
    rest = LANES - 4 * n_freq
    return (jnp.concatenate([cos, jnp.ones((seq, rest), F32)], axis=-1),
            jnp.concatenate([sin, jnp.zeros((seq, rest), F32)], axis=-1))


def _rope_apply(x, cos_ref, sin_ref, layout):
    assert layout.lo % LANES == 0 and layout.group % LANES == 0 and x.shape[-1] % layout.group == 0
    cos, sin = cos_ref[...], sin_ref[...]
    slabs = []
    for c0 in range(0, x.shape[-1], LANES):
        slab = x[:, c0:c0 + LANES]
        if c0 % layout.group == layout.lo:
            if layout.split:
                partner = pltpu.roll(slab, LANES // 2, 1)
            else:
                even = (lax.broadcasted_iota(jnp.int32, slab.shape, 1) & 1) == 0
                partner = jnp.where(even, pltpu.roll(slab, LANES - 1, 1), pltpu.roll(slab, 1, 1))
            slab = slab * cos + partner * sin
        slabs.append(slab)
    return jnp.concatenate(slabs, axis=1) if len(slabs) > 1 else slabs[0]


def _cast_deinterleave_columns(w, n_perm):
    kdim, n = w.shape
    perm = np.concatenate([np.arange(0, LANES, 2), np.arange(1, LANES, 2)])
    p = np.zeros((LANES, LANES), np.float32)
    p[perm, np.arange(LANES)] = 1.0
    tm, tn = min(TILE_M, kdim), min(TILE_N, n)
    assert n_perm % tn == 0
    perm_blocks = n_perm // tn

    def body(w_ref, p_ref, o_ref):
        @pl.when(pl.program_id(1) < perm_blocks)
        def _():
            pm = p_ref[...]
            for c0 in range(0, tn, LANES):
                o_ref[:, c0:c0 + LANES] = jnp.dot(w_ref[:, c0:c0 + LANES].astype(BF16), pm,
                                                  preferred_element_type=F32).astype(BF16)

        @pl.when(pl.program_id(1) >= perm_blocks)
        def _():
            o_ref[...] = w_ref[...].astype(BF16)

    return pl.pallas_call(
        body,
        grid=(kdim // tm, n // tn),
        in_specs=[pl.BlockSpec((tm, tn), lambda i, j: (i, j)), pl.BlockSpec((LANES, LANES), lambda i, j: (0, 0))],
        out_specs=pl.BlockSpec((tm, tn), lambda i, j: (i, j)),
        out_shape=jax.ShapeDtypeStruct((kdim, n), BF16),
        compiler_params=_params(("parallel", "parallel"), _nbytes((tm, tn), w.dtype) + _nbytes((tm, tn), BF16)),
        name="cast_deinterleave_rotary_columns",
    )(w, jnp.asarray(p, BF16))


def _mm_rope(a, w, tables, scale, seq, out_dtype, *, layout, tm=TILE_M, tn=TILE_N, tk=TILE_K, w_cols=None, name):
    m, n = a.shape[0], (w.shape[1] if w_cols is None else w_cols[1])
    tm, tn = min(tm, m, seq), min(tn, n)
    tiles_per_seq = seq // tm

    def epilogue(acc, ex, outs):
        y = acc if tables is None else _rope_apply(acc, ex[0], ex[1], layout)
        outs[0][...] = (y * scale).astype(out_dtype)

    extras = []
    if tables is not None:
        spec = pl.BlockSpec((tm, LANES), lambda i, j, k: (i % tiles_per_seq, 0))
        extras = [(tables[0], spec), (tables[1], spec)]
    return _matmul(a, w, tm=tm, tn=tn, tk=tk, epilogue=epilogue, extras=extras, w_cols=w_cols,
                   out_shape=[jax.ShapeDtypeStruct((m, n), out_dtype)],
                   out_specs=[pl.BlockSpec((tm, tn), lambda i, j, k: (i, j))], name=name)[0]


def _modulation(cvec, mod_w, mod_b, *, tn=512):
    depth, d, n = mod_w.shape
    rows = cvec.shape[0]
    tn = min(tn, n)

    def body(c_ref, w_ref, b_ref, o_ref):
        cv = c_ref[...]
        s = cv * (1.0 / (1.0 + jnp.exp(-cv)))
        o_ref[...] = jnp.dot(s, w_ref[...], preferred_element_type=F32) + b_ref[...]

    block_bytes = _nbytes((d, tn), F32) + _nbytes((rows, d), F32) + 2 * _nbytes((rows, tn), F32)
    return pl.pallas_call(
        body,
        grid=(depth, n // tn),
        in_specs=[pl.BlockSpec((rows, d), lambda l, j: (0, 0)),
                  pl.BlockSpec((None, d, tn), lambda l, j: (l, 0, j)),
                  pl.BlockSpec((None, 1, tn), lambda l, j: (l, 0, j))],
        out_specs=pl.BlockSpec((None, rows, tn), lambda l, j: (l, 0, j)),
        out_shape=jax.ShapeDtypeStruct((depth, rows, n), F32),
        compiler_params=_params(("parallel", "parallel"), block_bytes),
        name="adaln_modulation",
    )(cvec, mod_w, mod_b.reshape(depth, 1, n))


def _norm_mod(x, g, mods, sub, rows_per_group, out_dtype, *, tm=256):
    m, d = x.shape
    tm = min(tm, m, rows_per_group)
    tiles_per_group = rows_per_group // tm

    def body(*refs):
        if mods is None:
            x_ref, g_ref, o_ref = refs
        else:
            x_ref, g_ref, mod_ref, o_ref = refs
        y = _rms(x_ref[...], g_ref[...])
        if mods is not None:
            shift = mod_ref[3 * sub:3 * sub + 1, :]
            scale = mod_ref[3 * sub + 1:3 * sub + 2, :]
            y = y * (1.0 + scale) + shift
        o_ref[...] = y.astype(out_dtype)

    in_specs = [pl.BlockSpec((tm, d), lambda i: (i, 0)), pl.BlockSpec((1, d), lambda i: (0, 0))]
    args = [x, g.reshape(1, d)]
    if mods is not None:
        in_specs.append(pl.BlockSpec((None, 3 * N_SUB, d), lambda i: (i // tiles_per_group, 0, 0)))
        args.append(mods)
    block_bytes = _nbytes((tm, d), F32) + _nbytes((tm, d), out_dtype) + _nbytes((16, d), F32)
    return pl.pallas_call(
        body,
        grid=(m // tm,),
        in_specs=in_specs,
        out_specs=pl.BlockSpec((tm, d), lambda i: (i, 0)),
        out_shape=jax.ShapeDtypeStruct((m, d), out_dtype),
        compiler_params=_params(("parallel",), block_bytes, 2 * _nbytes((tm, d), F32)),
        name="rmsnorm_modulate",
    )(*args)


def _gate_up(h, w_gate, w_up, w_index, *, tm=2 * TILE_M, tn=TILE_N, tk=TILE_K):
    m, kdim = h.shape
    n = w_gate.shape[-1]
    tn = tn * jnp.dtype(BF16).itemsize // jnp.dtype(w_gate.dtype).itemsize
    tm, tn, tk = min(tm, m), min(tn, n), min(tk, kdim)
    assert m % tm == 0 and n % tn == 0 and kdim % tk == 0
    nk = kdim // tk

    def finish(g, u, o_ref):
        o_ref[...] = (g * (1.0 / (1.0 + jnp.exp(-g))) * u).astype(o_ref.dtype)

    def body(h_ref, wg_ref, wu_ref, o_ref, *acc):
        hh = h_ref[...]
        pg = jnp.dot(hh, wg_ref[...].astype(BF16), preferred_element_type=F32)
        pu = jnp.dot(hh, wu_ref[...].astype(BF16), preferred_element_type=F32)
        if nk == 1:
            finish(pg, pu, o_ref)
            return
        accg, accu = acc
        k = pl.program_id(2)

        @pl.when(k == 0)
        def _():
            accg[...] = pg
            accu[...] = pu

        if nk > 2:
            @pl.when((k > 0) & (k < nk - 1))
            def _():
                accg[...] += pg
                accu[...] += pu

        @pl.when(k == nk - 1)
        def _():
            finish(accg[...] + pg, accu[...] + pu, o_ref)

    block_bytes = _nbytes((tm, tk), BF16) + 2 * _nbytes((tk, tn), w_gate.dtype) + _nbytes((tm, tn), BF16)
    acc_bytes = _nbytes((tm, tn), F32) + _nbytes((tk, tn), BF16)
    w_spec = pl.BlockSpec((None,) * len(w_index) + (tk, tn), lambda i, j, k: (*w_index, k, j))
    return pl.pallas_call(
        body,
        grid=(m // tm, n // tn, nk),
        in_specs=[pl.BlockSpec((tm, tk), lambda i, j, k: (i, k), pipeline_mode=pl.Buffered(1)), w_spec, w_spec],
        out_specs=pl.BlockSpec((tm, tn), lambda i, j, k: (i, j)),
        out_shape=jax.ShapeDtypeStruct((m, n), BF16),
        scratch_shapes=[pltpu.VMEM((tm, tn), F32)] * 2 if nk > 1 else [],
        compiler_params=_params(("parallel", "parallel", "arbitrary"), block_bytes, 6 * acc_bytes),
        name="ffn_gate_up",
    )(h, w_gate, w_up)


def _half_ffn(x, mods, sub, rows_per_group, g, w_gate, w_up, w_down, w_index):
    h = _norm_mod(x, g, mods, sub, rows_per_group, BF16)
    u = _gate_up(h, w_gate, w_up, w_index)
    tn = TILE_N * jnp.dtype(BF16).itemsize // jnp.dtype(w_down.dtype).itemsize
    return _mm_residual(u, w_down, x, mods, sub, 0.5, rows_per_group, w_index=w_index,
                        tm=TILE_M, tn=tn, tk=TILE_K_LONG, resident_a=True, name="ffn_down_residual")


class ScoreSet(NamedTuple):
    q: object
    k_cols: slice
    write: object
    read: object


def _lane_fold(x, op):
    return functools.reduce(op, [x[:, c0:c0 + LANES] for c0 in range(0, x.shape[-1], LANES)])


def _attention_step(sets, k_refs, v_refs, chunk):
    state = [dict(mx=None, ls=None, acc=None, m_old=s.read[1][...]) for s in sets]
    off = 0
    for k_ref, v_ref in zip(k_refs, v_refs):
        total = k_ref.shape[0]
        for c0 in range(0, total, chunk):
            rows = min(chunk, total - c0)
            cols = slice(off + c0, off + c0 + rows)
            for s, st in zip(sets, state):
                if s.write is not None:
                    new = lax.dot_general(s.q, k_ref[c0:c0 + rows, s.k_cols], (((1,), (1,)), ((), ())),
                                          preferred_element_type=F32)
                    s.write[0][:, cols] = new
                    part = _lane_fold(new, jnp.maximum)
                    st["mx"] = part if st["mx"] is None else jnp.maximum(st["mx"], part)
            for s, st in zip(sets, state):
                m_old = jnp.concatenate([st["m_old"]] * (rows // LANES), axis=1) if rows > LANES else st["m_old"]
                e = jnp.exp(s.read[0][:, cols] - m_old)
                part = _lane_fold(e, jnp.add)
                st["ls"] = part if st["ls"] is None else st["ls"] + part
                pv = jnp.dot(e.astype(BF16), v_ref[c0:c0 + rows, :], preferred_element_type=F32)
                st["acc"] = pv if st["acc"] is None else st["acc"] + pv
        off += total
    out = []
    for s, st in zip(sets, state):
        if s.write is not None:
            s.write[1][...] = jnp.broadcast_to(jnp.max(st["mx"], axis=-1, keepdims=True), st["mx"].shape)
        out.append((st["acc"], jnp.sum(st["ls"], axis=-1, keepdims=True)))
    return out


def _skewed_steps(nq, score_bufs, emit):
    i = pl.program_id(2)
    buf_a, buf_b = score_bufs

    @pl.when(i == 0)
    def _():
        for pair in buf_b:
            for ref in pair:
                ref[...] = jnp.zeros_like(ref)

    even = lax.rem(i, 2) == 0

    @pl.when((i < nq) & even)
    def _():
        emit(buf_a, buf_b)

    @pl.when((i < nq) & jnp.logical_not(even))
    def _():
        emit(buf_b, buf_a)

    @pl.when(i == nq)
    def _():
        emit(None, buf_a if (nq - 1) % 2 == 0 else buf_b)


def _skewed_maps(nq):
    return (lambda bb, h, i: (bb, jnp.minimum(i, nq - 1), h)), (lambda bb, h, i: (bb, jnp.maximum(i - 1, 0), h))


def _kv_specs(arrays, width):
    return [pl.BlockSpec((None, a.shape[1], width), lambda bb, h, i: (bb, 0, h)) for a in arrays]


def _mla_attention(q, ks, vs, heads, *, tq=2 * ATTN_TQ):
    b, sq, _ = q.shape
    sk = sum(k.shape[1] for k in ks)
    tq = min(tq, sq)
    nkv = len(ks)

    nq = sq // tq
    q_map, o_map = _skewed_maps(nq)

    def body(q_ref, *refs):
        k_refs, v_refs = refs[:nkv], refs[nkv:2 * nkv]
        o_ref = refs[2 * nkv]
        s_a, m_a, s_b, m_b = refs[2 * nkv + 1:]

        def emit(write, read):
            sets = [ScoreSet(q_ref[...], slice(None), None if write is None else write[0], read[0])]
            (o, l), = _attention_step(sets, k_refs, v_refs, ATTN_CHUNK)
            o_ref[...] = (o / l).astype(o_ref.dtype)

        _skewed_steps(nq, ([(s_a, m_a)], [(s_b, m_b)]), emit)

    block_bytes = (_nbytes((tq, MLA_HEAD_PAD), BF16) + _nbytes((sk, MLA_HEAD_PAD), BF16)
                   + _nbytes((sk, V_DIM), BF16) + _nbytes((tq, V_DIM), BF16))
    score_bytes = _nbytes((tq, sk), F32)
    return pl.pallas_call(
        body,
        grid=(b, heads, nq + 1),
        in_specs=([pl.BlockSpec((None, tq, MLA_HEAD_PAD), q_map)]
                  + _kv_specs(ks, MLA_HEAD_PAD) + _kv_specs(vs, V_DIM)),
        out_specs=pl.BlockSpec((None, tq, V_DIM), o_map),
        out_shape=jax.ShapeDtypeStruct((b, sq, heads * V_DIM), BF16),
        scratch_shapes=[pltpu.VMEM((tq, sk), F32), pltpu.VMEM((tq, LANES), F32)] * 2,
        compiler_params=_params(("parallel", "parallel", "arbitrary"), block_bytes, 4 * score_bytes),
        name="mla_attention",
    )(q, *ks, *vs)


def _diff_attention(q, ks, vs, lam_p, subln_g, lambda_init, heads, *, tq=ATTN_TQ):
    b, sq, _ = q.shape
    sk = sum(k.shape[1] for k in ks)
    tq = min(tq, sq)
    hw = 2 * DIFF_DIM
    nkv = len(ks)
    nq = sq // tq
    q_map, o_map = _skewed_maps(nq)

    def body(lam_ref, g_ref, q_ref, *refs):
        k_refs, v_refs = refs[:nkv], refs[nkv:2 * nkv]
        o_ref = refs[2 * nkv]
        bufs = refs[2 * nkv + 1:]
        set_a = [(bufs[0], bufs[1]), (bufs[2], bufs[3])]
        set_b = [(bufs[4], bufs[5]), (bufs[6], bufs[7])]

        def emit(write, read):
            halves = [(q_ref[:, :DIFF_DIM], slice(0, DIFF_DIM)), (q_ref[:, DIFF_DIM:], slice(DIFF_DIM, hw))]
            sets = [ScoreSet(qh, cols, None if write is None else write[n], read[n])
                    for n, (qh, cols) in enumerate(halves)]
            (o1, l1), (o2, l2) = _attention_step(sets, k_refs, v_refs, ATTN_CHUNK)
            lp = lam_ref[...]
            lam = (jnp.exp(jnp.sum(lp[0:1] * lp[1:2], axis=-1, keepdims=True))
                   - jnp.exp(jnp.sum(lp[2:3] * lp[3:4], axis=-1, keepdims=True)) + lambda_init)
            o = o1 / l1 - (lam / l2) * o2
            o_ref[...] = (_rms(o, g_ref[...]) * (1.0 - lambda_init)).astype(o_ref.dtype)

        _skewed_steps(nq, (set_a, set_b), emit)

    block_bytes = 2 * _nbytes((tq, hw), BF16) + 2 * _nbytes((sk, hw), BF16)
    score_bytes = _nbytes((tq, sk), F32)
    return pl.pallas_call(
        body,
        grid=(b, heads, nq + 1),
        in_specs=([pl.BlockSpec((4, DIFF_DIM), lambda bb, h, i: (0, 0)),
                   pl.BlockSpec((1, hw), lambda bb, h, i: (0, 0)),
                   pl.BlockSpec((None, tq, hw), q_map)]
                  + _kv_specs(ks, hw) + _kv_specs(vs, hw)),
        out_specs=pl.BlockSpec((None, tq, hw), o_map),
        out_shape=jax.ShapeDtypeStruct((b, sq, heads * hw), BF16),
        scratch_shapes=[pltpu.VMEM((tq, sk), F32), pltpu.VMEM((tq, LANES), F32)] * 4,
        compiler_params=_params(("parallel", "parallel", "arbitrary"), block_bytes, 6 * score_bytes),
        name="diff_attention",
    )(lam_p.astype(F32), subln_g.reshape(1, hw).astype(F32), q, *ks, *vs)


def _hyena_filters(seq, w1, b1, w2, b2, w3, b3, freq, w_out, *, tl=256):
    hid = w1.shape[1]
    width2 = w_out.shape[1]
    width = width2 // 2
    bands = (HY_EMB - 1) // 2
    t = jnp.linspace(0.0, 1.0, seq, dtype=F32)[:, None]
    ang = ((2.0 * math.pi / seq) * jnp.arange(seq, dtype=F32)[:, None]
           * jnp.linspace(1e-4, bands - 1, bands, dtype=F32)[None, :])
    z = jnp.concatenate([t, jnp.cos(ang), -jnp.sin(ang), jnp.zeros((seq, LANES - HY_EMB), F32)], axis=-1)
    z = jnp.concatenate([z[0::2], z[1::2]], axis=0)
    deltas =jnp.abs(jnp.linspace(math.log(HY_TARGET) / HY_LONG_DECAY_PCT,
                                  math.log(HY_TARGET) / HY_SHORT_DECAY_PCT, width, dtype=F32))
    deltas2 = jnp.concatenate([deltas, deltas])[None, :]

    def pad2(w):
        return jnp.pad(w.astype(F32), ((0, LANES - w.shape[0]), (0, LANES - w.shape[1])))

    def pad_row(v):
        return jnp.pad(v.astype(F32), (0, LANES - v.shape[0]))[None, :]

    w_out_p = jnp.pad(w_out.astype(F32), ((0, LANES - hid), (0, 0)))
    tl, tn = min(tl, seq), width2
    exact = lax.Precision.HIGHEST

    def body(z_ref, w1_ref, b1_ref, w2_ref, b2_ref, w3_ref, b3_ref, f_ref, wo_ref, d_ref, o_ref, mid_ref):
        zz, f = z_ref[...], f_ref[...]
        a = jnp.sin(f * (jnp.dot(zz, w1_ref[...], precision=exact, preferred_element_type=F32) + b1_ref[...]))
        a = jnp.sin(f * (jnp.dot(a, w2_ref[...], precision=exact, preferred_element_type=F32) + b2_ref[...]))
        a = jnp.sin(f * (jnp.dot(a, w3_ref[...], precision=exact, preferred_element_type=F32) + b3_ref[...]))
        h = jnp.dot(a, wo_ref[...], precision=exact, preferred_element_type=F32)
        h = h * jnp.exp(-zz[:, 0:1] * d_ref[...])
        row = pl.program_id(0) * tl + lax.broadcasted_iota(jnp.int32, (tl, width), 0)
        h_fwd = h[:, :width]
        h_bwd = jnp.where(row == 0, 0.0, h[:, width:])
        h_sum = h_fwd + h_bwd
        o_ref[:, :width] = h_sum.astype(o_ref.dtype)
        o_ref[:, width:] = (h_fwd - h_bwd).astype(o_ref.dtype)
        signed = jnp.where(row >= seq // 2, 0.0, jnp.where((row & 1) == 0, h_sum, -h_sum))
        alternating = jnp.sum(signed, axis=0, keepdims=True)

        @pl.when(pl.program_id(0) == 0)
        def _():
            mid_ref[...] = alternating

        @pl.when(pl.program_id(0) > 0)
        def _():
            mid_ref[...] += alternating

    sq = pl.BlockSpec((LANES, LANES), lambda i: (0, 0))
    vec = pl.BlockSpec((1, LANES), lambda i: (0, 0))
    block_bytes = _nbytes((tl, LANES), F32) + _nbytes((LANES, tn), F32) + 2 * _nbytes((tl, tn), F32)
    return pl.pallas_call(
        body,
        grid=(seq // tl,),
        in_specs=[pl.BlockSpec((tl, LANES), lambda i: (i, 0)), sq, vec, sq, vec, sq, vec, vec,
                  pl.BlockSpec((LANES, tn), lambda i: (0, 0)),
                  pl.BlockSpec((1, tn), lambda i: (0, 0))],
        out_specs=[pl.BlockSpec((tl, tn), lambda i: (i, 0)), pl.BlockSpec((1, width), lambda i: (0, 0))],
        out_shape=[jax.ShapeDtypeStruct((seq, width2), BF16), jax.ShapeDtypeStruct((1, width), F32)],
        compiler_params=_params(("arbitrary",), block_bytes, 4 * _nbytes((tl, tn), F32)),
        name="hyena_filters",
    )(z, pad2(w1), pad_row(b1), pad2(w2), pad_row(b2), pad2(w3), pad_row(b3), pad_row(freq), w_out_p, deltas2)


def _hyena_prep(zh, conv_w, conv_b, *, tc=128):
    b, seq, w3 = zh.shape
    width = w3 // 3
    half = seq // 2
    tc = min(tc, width)
    nw = width // tc

    def body(z0_ref, z1_ref, z2_ref, w0_ref, w1_ref, w2_ref, b0_ref, b1_ref, b2_ref, x0_ref, vx_ref):
        row = lax.broadcasted_iota(jnp.int32, (half, 1), 0)

        def conv(z_ref, w_ref, b_ref):
            ev = z_ref[pl.ds(0, half, stride=2), :]
            od = z_ref[pl.ds(1, half, stride=2), :]
            w, bias = w_ref[...], b_ref[...]
            before_even = jnp.where(row == 0, 0.0, pltpu.roll(od, 1, 0))
            after_odd = jnp.where(row == half - 1, 0.0, pltpu.roll(ev, half - 1, 0))
            return (before_even * w[0:1] + ev * w[1:2] + od * w[2:3] + bias,
                    ev * w[0:1] + od * w[1:2] + after_odd * w[2:3] + bias)

        x0 = conv(z0_ref, w0_ref, b0_ref)
        x1 = conv(z1_ref, w1_ref, b1_ref)
        v = conv(z2_ref, w2_ref, b2_ref)
        for parity in range(2):
            x0_ref[parity] = x0[parity]
            vx_ref[parity] = (v[parity] * x1[parity]).astype(vx_ref.dtype)

    def zspec(part):
        return pl.BlockSpec((None, seq, tc), lambda bb, j: (bb, 0, part * nw + j))

    def wspec(rows, part):
        return pl.BlockSpec((rows, tc), lambda bb, j: (0, part * nw + j))

    out_spec = pl.BlockSpec((2, half, tc), lambda bb, j: (0, 0, bb * nw + j))
    block_bytes = 4 * _nbytes((seq, tc), F32) + _nbytes((seq, tc), BF16)
    return pl.pallas_call(
        body,
        grid=(b, nw),
        in_specs=[zspec(0), zspec(1), zspec(2),
                  wspec(3, 0), wspec(3, 1), wspec(3, 2), wspec(1, 0), wspec(1, 1), wspec(1, 2)],
        out_specs=[out_spec, out_spec],
        out_shape=[jax.ShapeDtypeStruct((2, half, b * width), F32),
                   jax.ShapeDtypeStruct((2, half, b * width), BF16)],
        compiler_params=_params(("parallel", "parallel"), block_bytes, 8 * _nbytes((seq, tc), F32)),
        name="hyena_short_conv",
    )(zh, zh, zh, conv_w, conv_w, conv_w, conv_b[None, :], conv_b[None, :], conv_b[None, :])


def _dft_matrices(seq):
    n = 2 * seq
    half = seq // 2
    assert seq % 2 == 0 and n % 4 == 0
    idx = jnp.arange(half, dtype=jnp.int32)
    alt = jnp.where((idx & 1) == 0, 1.0, -1.0).astype(F32)

    def cos_sin(k, s):
        phase = _mod_nonneg(k * s, n)
        return _cos_turns(phase, n), _cos_turns(_mod_nonneg(phase + (n - n // 4), n), n)

    def forward(c, s):
        return jnp.concatenate([c, jnp.where(idx[:, None] == 0, alt[None, :], s)], axis=0).astype(BF16)

    def inverse(c, s):
        return jnp.concatenate([c, jnp.where(idx[None, :] == 0, alt[:, None], s)], axis=1).astype(BF16)

    even = cos_sin(idx[:, None], 2 * idx[None, :])
    odd = cos_sin(idx[:, None], 2 * idx[None, :] + 1)
    odd_t = cos_sin(idx[None, :], 2 * idx[:, None] + 1)
    return forward(*even), forward(*odd), inverse(*even), inverse(*odd_t)


def _div_nonneg(x, d):
    return x >> (d.bit_length() - 1) if d & (d - 1) == 0 else x // d


def _mod_nonneg(x, d):
    return x & (d - 1) if d & (d - 1) == 0 else x % d


def _cos_turns(phase, n):
    quarter = n // 4
    quad = _div_nonneg(phase, quarter)
    rem = phase - quad * quarter
    odd = (quad & 1) == 1
    x = jnp.where(odd, quarter - rem, rem).astype(F32) * (2.0 * math.pi / n)
    x2 = x * x
    acc = jnp.full_like(x2, 1.0 / math.factorial(16))
    for order in range(14, -1, -2):
        acc = acc * (-x2) + 1.0 / math.factorial(order)
    return jnp.where((quad == 1) | (quad == 2), -acc, acc)


def _spectrum_multiply(eu, ou, eg, og, mid_g, batch, width, *, tr=256, tc=512):
    seq = eu.shape[0]
    half = seq // 2
    n = 2 * seq
    tr, tc = min(tr, half), min(tc, width)
    nwc = width // tc

    def body(eu_ref, ou_ref, eg_ref, og_ref, mid_ref, pe_ref, po_ref):
        row = pl.program_id(0) * tr + lax.broadcasted_iota(jnp.int32, (tr, tc), 0)
        first = row == 0

        def pair(e_ref, o_ref):
            ec, es, oc, os_ = e_ref[0], e_ref[1], o_ref[0], o_ref[1]
            return (ec + oc, jnp.where(first, 0.0, es + os_)), (ec - oc, jnp.where(first, 0.0, os_ - es))

        def product(u, g):
            return u[0] * g[0] - u[1] * g[1], u[0] * g[1] + u[1] * g[0]

        (u_lo, u_hi), (g_lo, g_hi) = pair(eu_ref, ou_ref), pair(eg_ref, og_ref)
        ya_lo, yb_lo = product(u_lo, g_lo)
        ya_hi, yb_hi = product(u_hi, g_hi)
        ya_mid, yb_mid = product((eu_ref[1], ou_ref[1]), (mid_ref[...], og_ref[1]))
        w_cos = jnp.where(first, 1.0 / n, 2.0 / n)
        w_sin = 2.0 / n
        pe_ref[0] = (w_cos * (ya_lo + ya_hi)).astype(pe_ref.dtype)
        po_ref[0] = (w_cos * (ya_lo - ya_hi)).astype(po_ref.dtype)
        pe_ref[1] = (w_sin * jnp.where(first, ya_mid, yb_lo - yb_hi)).astype(pe_ref.dtype)
        po_ref[1] = (w_sin * jnp.where(first, yb_mid, yb_lo + yb_hi)).astype(po_ref.dtype)

    u_spec = pl.BlockSpec((2, tr, tc), lambda i, j, bb: (0, i, bb * nwc + j))
    g_spec = pl.BlockSpec((2, tr, tc), lambda i, j, bb: (0, i, j))
    block_bytes = 4 * _nbytes((2, tr, tc), F32) + 2 * _nbytes((2, tr, tc), BF16)
    out_sds = jax.ShapeDtypeStruct((2, half, batch * width), BF16)
    pe, po = pl.pallas_call(
        body,
        grid=(half // tr, nwc, batch),
        in_specs=[u_spec, u_spec, g_spec, g_spec, pl.BlockSpec((1, tc), lambda i, j, bb: (0, j))],
        out_specs=[u_spec, u_spec],
        out_shape=[out_sds, out_sds],
        compiler_params=_params(("parallel", "parallel", "arbitrary"), block_bytes, 24 * _nbytes((tr, tc), F32)),
        name="hyena_spectrum_multiply",
    )(eu.reshape(2, half, batch * width), ou.reshape(2, half, batch * width),
      eg.reshape(2, half, width), og.reshape(2, half, width), mid_g)
    return pe.reshape(seq, batch * width), po.reshape(seq, batch * width)


def _hyena_branch(zh, conv_w, conv_b, filt, hy_bias):
    b, seq, w3 = zh.shape
    width = w3 // 3
    half = seq // 2
    x0, vx = _hyena_prep(zh, conv_w.astype(F32), conv_b.astype(F32))
    h_sum_dif, mid_g = _hyena_filters(seq, *filt)
    h_sum_dif = h_sum_dif.reshape(2, half, 2 * width)
    mats = _dft_matrices(seq)
    forward_mats, inverse_mats = mats[:2], mats[2:]
    tm, tn = min(TILE_M, half), min(TILE_N, width)
    nw = width // tn

    def store(acc, ex, outs):
        outs[0][...] = acc

    def forward(mat, samples, parity, cols, col_shift, name):
        return _matmul(mat, samples, tm=tm, tn=tn, tk=TILE_K, epilogue=store, w_index=(parity,),
                       w_cols=(0, cols), w_col_shift=col_shift,
                       out_shape=[jax.ShapeDtypeStruct((seq, cols), F32)],
                       out_specs=[pl.BlockSpec((tm, tn), lambda i, j, k: (i, j))], name=name)[0]

    def filter_cols(i):
        return (i // (half // tm)) * nw

    data = [forward(m, vx, p, b * width, None, "hyena_dft_forward") for p, m in enumerate(forward_mats)]
    filt_spec = [forward(m, h_sum_dif, p, width, filter_cols, "hyena_dft_filters")
                 for p, m in enumerate(forward_mats)]
    spectra = _spectrum_multiply(data[0], data[1], filt_spec[0], filt_spec[1], mid_g, b, width)

    return _hyena_inverse(inverse_mats, spectra, x0, vx, hy_bias.astype(F32)[None, :], b, width)


def _hyena_inverse(mats, spectra, x0, vx, bias, batch, width):
    half, seq = mats[0].shape
    tm, tn = min(TILE_M // 2, half), min(TILE_N, width)
    nw = width // tn

    def body(me_ref, mo_ref, pe_ref, po_ref, x0_ref, vx_ref, bias_ref, o_ref, rows_ref):
        for parity, (m_ref, p_ref) in enumerate(((me_ref, pe_ref), (mo_ref, po_ref))):
            conv = jnp.dot(m_ref[...], p_ref[...], preferred_element_type=F32)
            gated = x0_ref[parity] * (conv + bias_ref[...] * vx_ref[parity].astype(F32))
            for c in range(tn // LANES):
                rows_ref[c, pl.ds(parity, tm, stride=2), :] = gated[:, c * LANES:(c + 1) * LANES]
        for c in range(tn // LANES):
            o_ref[:, c * LANES:(c + 1) * LANES] = rows_ref[c].astype(o_ref.dtype)

    mat_spec = pl.BlockSpec((tm, seq), lambda i, j: (i, 0))
    spec_spec = pl.BlockSpec((seq, tn), lambda i, j: (0, j))
    split_spec = pl.BlockSpec((2, tm, tn), lambda i, j: (0, i, j))
    block_bytes = (2 * _nbytes((tm, seq), BF16) + 2 * _nbytes((seq, tn), BF16) + _nbytes((2, tm, tn), F32)
                   + 2 * _nbytes((2, tm, tn), BF16))
    return pl.pallas_call(
        body,
        grid=(half // tm, batch * width // tn),
        in_specs=[mat_spec, mat_spec, spec_spec, spec_spec, split_spec, split_spec,
                  pl.BlockSpec((1, tn), lambda i, j: (0, j % nw))],
        out_specs=pl.BlockSpec((None, 2 * tm, tn), lambda i, j: (j // nw, i, j % nw)),
        out_shape=jax.ShapeDtypeStruct((batch, 2 * half, width), BF16),
        scratch_shapes=[pltpu.VMEM((tn // LANES, 2 * tm, LANES), F32)],
        compiler_params=_params(("parallel", "parallel"), block_bytes, 8 * _nbytes((2 * tm, tn), F32)),
        name="hyena_dft_inverse",
    )(mats[0], mats[1], spectra[0], spectra[1], x0, vx, bias)


def _ab_latents(h, w1, q_norm_g, kv_norm_g, tables, seq, q_lora, kv_lora, *, tm=512, tk=1024):
    m = h.shape[0]
    n1 = w1.shape[1]
    tm = min(tm, m, seq)
    tiles_per_seq = seq // tm

    def epilogue(acc, ex, outs):
        outs[0][...] = _rms(acc[:, :q_lora], ex[0][...]).astype(BF16)
        outs[1][...] = _rms(acc[:, q_lora:q_lora + kv_lora], ex[1][...]).astype(BF16)
        kr = acc[:, q_lora + kv_lora:]
        if tables is not None:
            kr = _rope_apply(kr, ex[2], ex[3], MLA_ROPE_LAYOUT)
        outs[2][...] = kr

    extras = [(q_norm_g.astype(F32)[None, :], pl.BlockSpec((1, q_lora), lambda i, j, k: (0, 0))),
              (kv_norm_g.astype(F32)[None, :], pl.BlockSpec((1, kv_lora), lambda i, j, k: (0, 0)))]
    if tables is not None:
        spec = pl.BlockSpec((tm, LANES), lambda i, j, k: (i % tiles_per_seq, 0))
        extras += [(tables[0], spec), (tables[1], spec)]
    widths = (q_lora, kv_lora, MLA_HEAD_PAD)
    dtypes = (BF16, BF16, F32)
    return _matmul(h, w1, tm=tm, tn=n1, tk=tk, epilogue=epilogue, extras=extras,
                   out_shape=[jax.ShapeDtypeStruct((m, wd), dt) for wd, dt in zip(widths, dtypes)],
                   out_specs=[pl.BlockSpec((tm, wd), lambda i, j, k: (i, 0)) for wd in widths],
                   name="mla_latent_projection")


def _mla_keys(ckv, w_uk, kr, heads, *, tm=1024, tn=1024):
    m = ckv.shape[0]
    n = heads * MLA_HEAD_PAD
    tm, tn = min(tm, m), min(tn, n)
    reps = tn // MLA_HEAD_PAD

    def epilogue(acc, ex, outs):
        rot = ex[0][...]
        if reps > 1:
            rot = jnp.concatenate([rot] * reps, axis=1)
        outs[0][...] = (acc + rot).astype(BF16)

    extras = [(kr, pl.BlockSpec((tm, MLA_HEAD_PAD), lambda i, j, k: (i, 0)))]
    return _matmul(ckv, w_uk, tm=tm, tn=tn, tk=ckv.shape[1], epilogue=epilogue, extras=extras,
                   out_shape=[jax.ShapeDtypeStruct((m, n), BF16)],
                   out_specs=[pl.BlockSpec((tm, tn), lambda i, j, k: (i, j))], name="mla_key_up")[0]


def _ab_mixer(h, hc, batch, seq, seq_c, need_ctx, w_in, q_norm_g, kv_norm_g, w_uq, w_ukv,
              conv_w, conv_b, filt, hy_bias):
    d = w_in.shape[0]
    heads = d // (2 * V_DIM)
    q_lora, kv_lora = w_uq.shape[0], w_ukv.shape[0]
    kv_end = q_lora + kv_lora + ROPE_DIM
    mla_scale = (NOPE_DIM + ROPE_DIM) ** -0.5
    zero_pad = MLA_HEAD_PAD - NOPE_DIM - ROPE_DIM

    w1 = jnp.concatenate([w_in[:, :q_lora + kv_lora], jnp.zeros((d, NOPE_DIM), w_in.dtype),
                          w_in[:, q_lora + kv_lora:kv_end], jnp.zeros((d, zero_pad), w_in.dtype)],
                         axis=1).astype(BF16)
    w_hy = w_in[:, kv_end:].astype(BF16)
    uq = w_uq.reshape(q_lora, heads, NOPE_DIM + ROPE_DIM)
    w_uq_p = jnp.pad(uq, ((0, 0), (0, 0), (0, zero_pad))).reshape(q_lora, heads * MLA_HEAD_PAD).astype(BF16)
    ukv = w_ukv.reshape(kv_lora, heads, NOPE_DIM + V_DIM)
    w_uk_p = jnp.pad(ukv[..., :NOPE_DIM], ((0, 0), (0, 0), (0, MLA_HEAD_PAD - NOPE_DIM))
                     ).reshape(kv_lora, heads * MLA_HEAD_PAD).astype(BF16)
    w_uv = ukv[..., NOPE_DIM:].reshape(kv_lora, heads * V_DIM).astype(BF16)

    tables = _rope_tables(seq, MLA_ROPE_LAYOUT)
    rope_kw = dict(layout=MLA_ROPE_LAYOUT)

    cq, ckv, kr = _ab_latents(h, w1, q_norm_g, kv_norm_g, tables, seq, q_lora, kv_lora)
    cqc, ckvc, krc = _ab_latents(hc, w1, q_norm_g, kv_norm_g, None, seq_c, q_lora, kv_lora)
    q = _mm_rope(cq, w_uq_p, tables, mla_scale, seq, BF16, name="mla_query_up", **rope_kw)
    k = _mla_keys(ckv, w_uk_p, kr, heads)
    v = _mm_plain(ckv, w_uv, BF16, name="mla_value_up")
    kc = _mla_keys(ckvc, w_uk_p, krc, heads)
    vc = _mm_plain(ckvc, w_uv, BF16, name="mla_value_up")

    def split(t, s):
        return t.reshape(batch, s, t.shape[-1])

    att = _mla_attention(split(q, seq), [split(kc, seq_c), split(k, seq)], [split(vc, seq_c), split(v, seq)], heads)
    zh = _mm_plain(h, w_hy, F32, tn=512, name="hyena_in_projection")
    hy = _hyena_branch(split(zh, seq), conv_w, conv_b, filt, hy_bias)
    mix = jnp.concatenate([att, hy], axis=-1).reshape(batch * seq, -1)
    if not need_ctx:
        return mix, None
    qc = _mm_rope(cqc, w_uq_p, None, mla_scale, seq_c, BF16, name="mla_query_up", **rope_kw)
    attc = _mla_attention(split(qc, seq_c), [split(kc, seq_c)], [split(vc, seq_c)], heads)
    zhc = _mm_plain(hc, w_hy, F32, tn=512, name="hyena_in_projection")
    hyc = _hyena_branch(split(zhc, seq_c), conv_w, conv_b, filt, hy_bias)
    return mix, jnp.concatenate([attc, hyc], axis=-1).reshape(batch * seq_c, -1)


def _diff_mixer(h, hc, batch, seq, seq_c, need_ctx, w_in, lam_p, subln_g, lambda_init):
    d = w_in.shape[0]
    hd = w_in.shape[1] // 3
    heads = hd // (2 * DIFF_DIM)
    scale = DIFF_DIM ** -0.5
    w = _cast_deinterleave_columns(w_in, 2 * hd)
    q_cols, k_cols, v_cols = (0, hd), (hd, hd), (2 * hd, hd)
    tables = _rope_tables(seq, DIFF_ROPE_LAYOUT)
    rope_kw = dict(layout=DIFF_ROPE_LAYOUT)

    q = _mm_rope(h, w, tables, scale, seq, BF16, w_cols=q_cols, name="diff_query_projection", **rope_kw)
    k = _mm_rope(h, w, tables, 1.0, seq, BF16, w_cols=k_cols, name="diff_key_projection", **rope_kw)
    v = _mm_plain(h, w, BF16, w_cols=v_cols, name="diff_value_projection")
    kc = _mm_rope(hc, w, None, 1.0, seq_c, BF16, w_cols=k_cols, name="diff_key_projection", **rope_kw)
    vc = _mm_plain(hc, w, BF16, w_cols=v_cols, name="diff_value_projection")

    def split(t, s):
        return t.reshape(batch, s, t.shape[-1])

    o = _diff_attention(split(q, seq), [split(kc, seq_c), split(k, seq)], [split(vc, seq_c), split(v, seq)],
                        lam_p, subln_g, lambda_init, heads)
    mix = o.reshape(batch * seq, hd)
    if not need_ctx:
        return mix, None
    qc = _mm_rope(hc, w, None, scale, seq_c, BF16, w_cols=q_cols, name="diff_query_projection", **rope_kw)
    oc = _diff_attention(split(qc, seq_c), [split(kc, seq_c)], [split(vc, seq_c)], lam_p, subln_g, lambda_init, heads)
    return mix, oc.reshape(batch * seq_c, hd)


def kernel(x, c, ctx, c_ctx, mod_w, mod_b, norm_g, ffn_w_gate, ffn_w_up, ffn_w_down, ab_w_in, mla_q_norm_g, mla_kv_norm_g, mla_w_uq, mla_w_ukv, hy_conv_w, hy_conv_b, hy_w1, hy_b1, hy_w2, hy_b2, hy_w3, hy_b3, hy_freq, hy_w_out, hy_bias, ab_w_out, c_w_in, c_lambda, c_subln_g, c_w_out, final_norm_g):
    batch, seq, d = x.shape
    seq_c = ctx.shape[1]
    depth = mod_w.shape[0]
    rows_c = batch * seq_c
    xs = x.reshape(batch * seq, d).astype(F32)
    xc = ctx.reshape(rows_c, d).astype(F32)

    cvec = jnp.concatenate([c.astype(F32), c_ctx.astype(F32)[None, :]], axis=0)
    cvec = jnp.pad(cvec, ((0, -cvec.shape[0] % 8), (0, 0)))
    mods_all = _modulation(cvec, mod_w.astype(F32), mod_b.astype(F32))

    for layer in range(depth):
        need_ctx = layer < depth - 1
        mods = mods_all[layer, :batch].reshape(batch, 3 * N_SUB, d)
        modc = mods_all[layer, batch:batch + 1].reshape(1, 3 * N_SUB, d)
        g = norm_g[layer].astype(F32)

        def ffn(t, m, rows, sub, which):
            return _half_ffn(t, m, sub, rows, g[sub], ffn_w_gate, ffn_w_up, ffn_w_down, (layer, which))

        xs = ffn(xs, mods, seq, 0, 0)
        xc = ffn(xc, modc, rows_c, 0, 0)
        h = _norm_mod(xs, g[1], mods, 1, seq, BF16)
        hc = _norm_mod(xc, g[1], modc, 1, rows_c, BF16)
        i = layer // 2
        if layer % 2 == 0:
            filt = (hy_w1[i], hy_b1[i], hy_w2[i], hy_b2[i], hy_w3[i], hy_b3[i], hy_freq[i], hy_w_out[i])
            mix, mixc = _ab_mixer(h, hc, batch, seq, seq_c, need_ctx, ab_w_in[i], mla_q_norm_g[i],
                                  mla_kv_norm_g[i], mla_w_uq[i], mla_w_ukv[i], hy_conv_w[i], hy_conv_b[i],
                                  filt, hy_bias[i])
            w_out = ab_w_out[i].astype(BF16)
        else:
            lambda_init = 0.8 - 0.6 * math.exp(-0.3 * layer)
            mix, mixc = _diff_mixer(h, hc, batch, seq, seq_c, need_ctx, c_w_in[i], c_lambda[i], c_subln_g[i],
                                    lambda_init)
            w_out = c_w_out[i].astype(BF16)
        xs = _mm_residual(mix, w_out, xs, mods, 1, 1.0, seq, name="mixer_out_residual")
        xs = ffn(xs, mods, seq, 2, 1)
        if need_ctx:
            xc = _mm_residual(mixc, w_out, xc, modc, 1, 1.0, rows_c, name="mixer_out_residual")
            xc = ffn(xc, modc, rows_c, 2, 1)

    out = _norm_mod(xs, final_norm_g.astype(F32), None, 0, seq, x.dtype)
    return out.reshape(batch, seq, d)
```

```python
import functools
import math
from typing import NamedTuple

import numpy as np
import jax
import jax.numpy as jnp
from jax import lax
from jax.experimental import pallas as pl
from jax.experimental.pallas import tpu as pltpu

F32 = jnp.float32
BF16 = jnp.bfloat16

EPS = 1e-6
ROPE_THETA = 10000.0
GRID_W = 64
N_SUB = 3
NOPE_DIM = 128
ROPE_DIM = 64
V_DIM = 128
DIFF_DIM = 128
HY_EMB = 33
HY_TARGET = 1e-2
HY_SHORT_DECAY_PCT = 0.3
HY_LONG_DECAY_PCT = 1.5

LANES = 128
MXU_DIM_V7X = 256
VMEM_BYTES_V7X = 64 * 1024 * 1024
VMEM_COMPILER_RESERVE = 8 * 1024 * 1024
MLA_HEAD_PAD = MXU_DIM_V7X

TILE_M = 1024
TILE_N = 512
TILE_K = 4096
TILE_K_LONG = 2 * TILE_K
ATTN_TQ = 256
ATTN_CHUNK = 256


def _params(semantics, block_bytes, temp_bytes=0):
    want = 2 * block_bytes + temp_bytes
    limit = min(VMEM_BYTES_V7X - VMEM_COMPILER_RESERVE, max(want, 32 * 1024 * 1024))
    return pltpu.CompilerParams(dimension_semantics=semantics, vmem_limit_bytes=int(limit))


def _nbytes(shape, dtype):
    return math.prod(shape) * jnp.dtype(dtype).itemsize


def _rms(x, g):
    return x * lax.rsqrt(jnp.mean(x * x, axis=-1, keepdims=True) + EPS) * g


def _matmul(a, w, *, tm, tn, tk, epilogue, out_shape, out_specs, extras=(), w_index=(), w_cols=None,
            w_col_shift=None, resident_a=False, name):
    a_parts = list(a) if isinstance(a, (list, tuple)) else [a]
    m = a_parts[0].shape[0]
    kdim = sum(p.shape[1] for p in a_parts)
    assert w.ndim == 2 + len(w_index) and w.shape[-2] == kdim
    col0, n = (0, w.shape[-1]) if w_cols is None else w_cols
    tm, tn, tk = min(tm, m), min(tn, n), min(tk, kdim)
    assert m % tm == 0 and n % tn == 0 and kdim % tk == 0 and col0 % tn == 0, (name, a.shape, w.shape, tm, tn, tk)
    j0 = col0 // tn
    nk = kdim // tk
    n_ex, n_out = len(extras), len(out_shape)

    def body(*refs):
        a_refs, w_ref = refs[:n_a], refs[n_a]
        ex = refs[n_a + 1:n_a + 1 + n_ex]
        outs = refs[n_a + 1 + n_ex:n_a + 1 + n_ex + n_out]
        prod, k0 = None, 0
        for a_ref in a_refs:
            rows = a_ref.shape[1]
            part = jnp.dot(a_ref[...], w_ref[k0:k0 + rows, :].astype(BF16), preferred_element_type=F32)
            prod = part if prod is None else prod + part
            k0 += rows
        if nk == 1:
            epilogue(prod, ex, outs)
            return
        acc_ref = refs[-1]
        k = pl.program_id(2)

        @pl.when(k == 0)
        def _():
            acc_ref[...] = prod

        if nk > 2:
            @pl.when((k > 0) & (k < nk - 1))
            def _():
                acc_ref[...] += prod

        @pl.when(k == nk - 1)
        def _():
            epilogue(acc_ref[...] + prod, ex, outs)

    n_a = len(a_parts)
    assert n_a == 1 or nk == 1
    a_mode = pl.Buffered(1) if resident_a else None
    in_specs = [pl.BlockSpec((tm, tk if n_a == 1 else p.shape[1]), lambda i, j, k: (i, k), pipeline_mode=a_mode)
                for p in a_parts]
    in_specs += [pl.BlockSpec((None,) * len(w_index) + (tk, tn),
                             lambda i, j, k: (*w_index, k, j + j0 + (0 if w_col_shift is None else w_col_shift(i))))]
    in_specs += [spec for _, spec in extras]
    block_bytes = _nbytes((tm, tk), a_parts[0].dtype) + _nbytes((tk, tn), w.dtype)
    for arr, spec in extras:
        block_bytes += _nbytes([d for d in spec.block_shape if d is not None], arr.dtype)
    for sds, spec in zip(out_shape, out_specs):
        block_bytes += _nbytes([d for d in spec.block_shape if d is not None], sds.dtype)
    acc_bytes = _nbytes((tm, tn), F32)
    return pl.pallas_call(
        body,
        grid=(m // tm, n // tn, nk),
        in_specs=in_specs,
        out_specs=list(out_specs),
        out_shape=list(out_shape),
        scratch_shapes=[pltpu.VMEM((tm, tn), F32)] if nk > 1 else [],
        compiler_params=_params(("parallel", "parallel", "arbitrary"), block_bytes, 8 * acc_bytes),
        name=name,
    )(*a_parts, w, *[arr for arr, _ in extras])


def _mm_plain(a, w, out_dtype, *, tm=TILE_M, tn=TILE_N, tk=TILE_K, w_cols=None, name):
    m, n = a.shape[0], (w.shape[1] if w_cols is None else w_cols[1])
    tm, tn = min(tm, m), min(tn, n)

    def epilogue(acc, ex, outs):
        outs[0][...] = acc.astype(out_dtype)

    return _matmul(a, w, tm=tm, tn=tn, tk=tk, epilogue=epilogue, w_cols=w_cols,
                   out_shape=[jax.ShapeDtypeStruct((m, n), out_dtype)],
                   out_specs=[pl.BlockSpec((tm, tn), lambda i, j, k: (i, j))], name=name)[0]


def _mm_residual(a, w, res, mods, sub, coef, rows_per_group, *, tm=TILE_M, tn=TILE_N, tk=TILE_K, w_index=(),
                 resident_a=False, name):
    m, n = (a[0] if isinstance(a, (list, tuple)) else a).shape[0], w.shape[-1]
    tm, tn = min(tm, m, rows_per_group), min(tn, n)
    tiles_per_group = rows_per_group // tm
    gate_row = 3 * sub + 2

    def epilogue(acc, ex, outs):
        res_ref, mod_ref = ex
        gate = mod_ref[gate_row:gate_row + 1, :]
        outs[0][...] = res_ref[...] + (coef * gate) * acc

    extras = [(res, pl.BlockSpec((tm, tn), lambda i, j, k: (i, j))),
              (mods, pl.BlockSpec((None, 3 * N_SUB, tn), lambda i, j, k: (i // tiles_per_group, 0, j)))]
    return _matmul(a, w, tm=tm, tn=tn, tk=tk, epilogue=epilogue, extras=extras, w_index=w_index,
                   resident_a=resident_a, out_shape=[jax.ShapeDtypeStruct((m, n), F32)],
                   out_specs=[pl.BlockSpec((tm, tn), lambda i, j, k: (i, j))], name=name)[0]


class RopeLayout(NamedTuple):
    group: int
    lo: int
    n_freq: int
    split: bool


MLA_ROPE_LAYOUT = RopeLayout(group=MLA_HEAD_PAD, lo=NOPE_DIM, n_freq=ROPE_DIM // 4, split=False)
DIFF_ROPE_LAYOUT = RopeLayout(group=DIFF_DIM, lo=0, n_freq=DIFF_DIM // 4, split=True)


def _rope_tables(seq, layout):
    n_freq = layout.n_freq
    pos = jnp.arange(seq, dtype=jnp.int32)
    row_pos = (pos // GRID_W).astype(F32)
    col_pos = (pos % GRID_W).astype(F32)
    inv = ROPE_THETA ** (-jnp.arange(n_freq, dtype=F32) / n_freq)
    ang = jnp.concatenate([row_pos[:, None] * inv, col_pos[:, None] * inv], axis=-1)
    cos, sin = jnp.cos(ang), jnp.sin(ang)
    if layout.split:
        assert 4 * n_freq == LANES
        return jnp.concatenate([cos, cos], axis=-1), jnp.concatenate([-sin, sin], axis=-1)
    cos = jnp.repeat(cos, 2, axis=-1)
    sin = jnp.repeat(sin, 2, axis=-1) * jnp.tile(jnp.array([-1.0, 1.0], F32), 2 * n_freq)
    rest = LANES - 4 * n_freq
    return (jnp.concatenate([cos, jnp.ones((seq, rest), F32)], axis=-1),
            jnp.concatenate([sin, jnp.zeros((seq, rest), F32)], axis=-1))


def _rope_apply(x, cos_ref, sin_ref, layout):
    assert layout.lo % LANES == 0 and layout.group % LANES == 0 and x.shape[-1] % layout.group == 0
    cos, sin = cos_ref[...], sin_ref[...]
    slabs = []
    for c0 in range(0, x.shape[-1], LANES):
        slab = x[:, c0:c0 + LANES]
        if c0 % layout.group == layout.lo:
            if layout.split:
                partner = pltpu.roll(slab, LANES // 2, 1)
            else:
                even = (lax.broadcasted_iota(jnp.int32, slab.shape, 1) & 1) == 0
                partner = jnp.where(even, pltpu.roll(slab, LANES - 1, 1), pltpu.roll(slab, 1, 1))
            slab = slab * cos + partner * sin
        slabs.append(slab)
    return jnp.concatenate(slabs, axis=1) if len(slabs) > 1 else slabs[0]


def _cast_deinterleave_columns(w, n_perm):
    kdim, n = w.shape
    perm = np.concatenate([np.arange(0, LANES, 2), np.arange(1, LANES, 2)])
    p = np.zeros((LANES, LANES), np.float32)
    p[perm, np.arange(LANES)] = 1.0
    tm, tn = min(TILE_M, kdim), min(TILE_N, n)
    assert n_perm % tn == 0
    perm_blocks = n_perm // tn

    def body(w_ref, p_ref, o_ref):
        @pl.when(pl.program_id(1) < perm_blocks)
        def _():
            pm = p_ref[...]
            for c0 in range(0, tn, LANES):
                o_ref[:, c0:c0 + LANES] = jnp.dot(w_ref[:, c0:c0 + LANES].astype(BF16), pm,
                                                  preferred_element_type=F32).astype(BF16)

        @pl.when(pl.program_id(1) >= perm_blocks)
        def _():
            o_ref[...] = w_ref[...].astype(BF16)

    return pl.pallas_call(
        body,
        grid=(kdim // tm, n // tn),
        in_specs=[pl.BlockSpec((tm, tn), lambda i, j: (i, j)), pl.BlockSpec((LANES, LANES), lambda i, j: (0, 0))],
        out_specs=pl.BlockSpec((tm, tn), lambda i, j: (i, j)),
        out_shape=jax.ShapeDtypeStruct((kdim, n), BF16),
        compiler_params=_params(("parallel", "parallel"), _nbytes((tm, tn), w.dtype) + _nbytes((tm, tn), BF16)),
        name="cast_deinterleave_rotary_columns",
    )(w, jnp.asarray(p, BF16))


def _mm_rope(a, w, tables, scale, seq, out_dtype, *, layout, tm=TILE_M, tn=TILE_N, tk=TILE_K, w_cols=None, name):
    m, n = a.shape[0], (w.shape[1] if w_cols is None else w_cols[1])
    tm, tn = min(tm, m, seq), min(tn, n)
    tiles_per_seq = seq // tm

    def epilogue(acc, ex, outs):
        y = acc if tables is None else _rope_apply(acc, ex[0], ex[1], layout)
        outs[0][...] = (y * scale).astype(out_dtype)

    extras = []
    if tables is not None:
        spec = pl.BlockSpec((tm, LANES), lambda i, j, k: (i % tiles_per_seq, 0))
        extras = [(tables[0], spec), (tables[1], spec)]
    return _matmul(a, w, tm=tm, tn=tn, tk=tk, epilogue=epilogue, extras=extras, w_cols=w_cols,
                   out_shape=[jax.ShapeDtypeStruct((m, n), out_dtype)],
                   out_specs=[pl.BlockSpec((tm, tn), lambda i, j, k: (i, j))], name=name)[0]


def _modulation(cvec, mod_w, mod_b, *, tn=512):
    depth, d, n = mod_w.shape
    rows = cvec.shape[0]
    tn = min(tn, n)

    def body(c_ref, w_ref, b_ref, o_ref):
        cv = c_ref[...]
        s = cv * (1.0 / (1.0 + jnp.exp(-cv)))
        o_ref[...] = jnp.dot(s, w_ref[...], preferred_element_type=F32) + b_ref[...]

    block_bytes = _nbytes((d, tn), F32) + _nbytes((rows, d), F32) + 2 * _nbytes((rows, tn), F32)
    return pl.pallas_call(
        body,
        grid=(depth, n // tn),
        in_specs=[pl.BlockSpec((rows, d), lambda l, j: (0, 0)),
                  pl.BlockSpec((None, d, tn), lambda l, j: (l, 0, j)),
                  pl.BlockSpec((None, 1, tn), lambda l, j: (l, 0, j))],
        out_specs=pl.BlockSpec((None, rows, tn), lambda l, j: (l, 0, j)),
        out_shape=jax.ShapeDtypeStruct((depth, rows, n), F32),
        compiler_params=_params(("parallel", "parallel"), block_bytes),
        name="adaln_modulation",
    )(cvec, mod_w, mod_b.reshape(depth, 1, n))


def _norm_mod(x, g, mods, sub, rows_per_group, out_dtype, *, tm=512):
    m, d = x.shape
    tm = min(tm, m, rows_per_group)
    tiles_per_group = rows_per_group // tm

    def body(*refs):
        if mods is None:
            x_ref, g_ref, o_ref = refs
        else:
            x_ref, g_ref, mod_ref, o_ref = refs
        y = _rms(x_ref[...], g_ref[...])
        if mods is not None:
            shift = mod_ref[3 * sub:3 * sub + 1, :]
            scale = mod_ref[3 * sub + 1:3 * sub + 2, :]
            y = y * (1.0 + scale) + shift
        o_ref[...] = y.astype(out_dtype)

    in_specs = [pl.BlockSpec((tm, d), lambda i: (i, 0)), pl.BlockSpec((1, d), lambda i: (0, 0))]
    args = [x, g.reshape(1, d)]
    if mods is not None:
        in_specs.append(pl.BlockSpec((None, 3 * N_SUB, d), lambda i: (i // tiles_per_group, 0, 0)))
        args.append(mods)
    block_bytes = _nbytes((tm, d), F32) + _nbytes((tm, d), out_dtype) + _nbytes((16, d), F32)
    return pl.pallas_call(
        body,
        grid=(m // tm,),
        in_specs=in_specs,
        out_specs=pl.BlockSpec((tm, d), lambda i: (i, 0)),
        out_shape=jax.ShapeDtypeStruct((m, d), out_dtype),
        compiler_params=_params(("parallel",), block_bytes, 2 * _nbytes((tm, d), F32)),
        name="rmsnorm_modulate",
    )(*args)


def _gate_up(h, w_gate, w_up, w_index, *, tm=2 * TILE_M, tn=TILE_N, tk=TILE_K):
    m, kdim = h.shape
    n = w_gate.shape[-1]
    tn = tn * jnp.dtype(BF16).itemsize // jnp.dtype(w_gate.dtype).itemsize
    tm, tn, tk = min(tm, m), min(tn, n), min(tk, kdim)
    assert m % tm == 0 and n % tn == 0 and kdim % tk == 0
    nk = kdim // tk

    def finish(g, u, o_ref):
        o_ref[...] = (g * (1.0 / (1.0 + jnp.exp(-g))) * u).astype(o_ref.dtype)

    def body(h_ref, wg_ref, wu_ref, o_ref, *acc):
        hh = h_ref[...]
        pg = jnp.dot(hh, wg_ref[...].astype(BF16), preferred_element_type=F32)
        pu = jnp.dot(hh, wu_ref[...].astype(BF16), preferred_element_type=F32)
        if nk == 1:
            finish(pg, pu, o_ref)
            return
        accg, accu = acc
        k = pl.program_id(2)

        @pl.when(k == 0)
        def _():
            accg[...] = pg
            accu[...] = pu

        if nk > 2:
            @pl.when((k > 0) & (k < nk - 1))
            def _():
                accg[...] += pg
                accu[...] += pu

        @pl.when(k == nk - 1)
        def _():
            finish(accg[...] + pg, accu[...] + pu, o_ref)

    block_bytes = _nbytes((tm, tk), BF16) + 2 * _nbytes((tk, tn), w_gate.dtype) + _nbytes((tm, tn), BF16)
    acc_bytes = _nbytes((tm, tn), F32) + _nbytes((tk, tn), BF16)
    w_spec = pl.BlockSpec((None,) * len(w_index) + (tk, tn), lambda i, j, k: (*w_index, k, j))
    return pl.pallas_call(
        body,
        grid=(m // tm, n // tn, nk),
        in_specs=[pl.BlockSpec((tm, tk), lambda i, j, k: (i, k), pipeline_mode=pl.Buffered(1)), w_spec, w_spec],
        out_specs=pl.BlockSpec((tm, tn), lambda i, j, k: (i, j)),
        out_shape=jax.ShapeDtypeStruct((m, n), BF16),
        scratch_shapes=[pltpu.VMEM((tm, tn), F32)] * 2 if nk > 1 else [],
        compiler_params=_params(("parallel", "parallel", "arbitrary"), block_bytes, 6 * acc_bytes),
        name="ffn_gate_up",
    )(h, w_gate, w_up)


def _half_ffn(x, mods, sub, rows_per_group, g, w_gate, w_up, w_down, w_index):
    h = _norm_mod(x, g, mods, sub, rows_per_group, BF16)
    u = _gate_up(h, w_gate, w_up, w_index)
    tn = TILE_N * jnp.dtype(BF16).itemsize // jnp.dtype(w_down.dtype).itemsize
    return _mm_residual(u, w_down, x, mods, sub, 0.5, rows_per_group, w_index=w_index,
                        tm=TILE_M, tn=tn, tk=TILE_K_LONG, resident_a=True, name="ffn_down_residual")


class ScoreSet(NamedTuple):
    q: object
    k_cols: slice
    write: object
    read: object


def _lane_fold(x, op):
    return functools.reduce(op, [x[:, c0:c0 + LANES] for c0 in range(0, x.shape[-1], LANES)])


def _attention_step(sets, k_refs, v_refs, chunk):
    state = [dict(mx=None, ls=None, acc=None, m_old=s.read[1][...]) for s in sets]
    off = 0
    for k_ref, v_ref in zip(k_refs, v_refs):
        total = k_ref.shape[0]
        for c0 in range(0, total, chunk):
            rows = min(chunk, total - c0)
            cols = slice(off + c0, off + c0 + rows)
            for s, st in zip(sets, state):
                if s.write is not None:
                    new = lax.dot_general(s.q, k_ref[c0:c0 + rows, s.k_cols], (((1,), (1,)), ((), ())),
                                          preferred_element_type=F32)
                    s.write[0][:, cols] = new
                    part = _lane_fold(new, jnp.maximum)
                    st["mx"] = part if st["mx"] is None else jnp.maximum(st["mx"], part)
            for s, st in zip(sets, state):
                m_old = jnp.concatenate([st["m_old"]] * (rows // LANES), axis=1) if rows > LANES else st["m_old"]
                e = jnp.exp(s.read[0][:, cols] - m_old)
                part = _lane_fold(e, jnp.add)
                st["ls"] = part if st["ls"] is None else st["ls"] + part
                pv = jnp.dot(e.astype(BF16), v_ref[c0:c0 + rows, :], preferred_element_type=F32)
                st["acc"] = pv if st["acc"] is None else st["acc"] + pv
        off += total
    out = []
    for s, st in zip(sets, state):
        if s.write is not None:
            s.write[1][...] = jnp.broadcast_to(jnp.max(st["mx"], axis=-1, keepdims=True), st["mx"].shape)
        out.append((st["acc"], jnp.sum(st["ls"], axis=-1, keepdims=True)))
    return out


def _skewed_steps(nq, score_bufs, emit):
    i = pl.program_id(2)
    buf_a, buf_b = score_bufs

    @pl.when(i == 0)
    def _():
        for pair in buf_b:
            for ref in pair:
                ref[...] = jnp.zeros_like(ref)

    even = lax.rem(i, 2) == 0

    @pl.when((i < nq) & even)
    def _():
        emit(buf_a, buf_b)

    @pl.when((i < nq) & jnp.logical_not(even))
    def _():
        emit(buf_b, buf_a)

    @pl.when(i == nq)
    def _():
        emit(None, buf_a if (nq - 1) % 2 == 0 else buf_b)


def _skewed_maps(nq):
    return (lambda bb, h, i: (bb, jnp.minimum(i, nq - 1), h)), (lambda bb, h, i: (bb, jnp.maximum(i - 1, 0), h))


def _kv_specs(arrays, width):
    return [pl.BlockSpec((None, a.shape[1], width), lambda bb, h, i: (bb, 0, h)) for a in arrays]


def _mla_attention(q, ks, vs, heads, *, tq=2 * ATTN_TQ):
    b, sq, _ = q.shape
    sk = sum(k.shape[1] for k in ks)
    tq = min(tq, sq)
    nkv = len(ks)

    nq = sq // tq
    q_map, o_map = _skewed_maps(nq)

    def body(q_ref, *refs):
        k_refs, v_refs = refs[:nkv], refs[nkv:2 * nkv]
        o_ref = refs[2 * nkv]
        s_a, m_a, s_b, m_b = refs[2 * nkv + 1:]

        def emit(write, read):
            sets = [ScoreSet(q_ref[...], slice(None), None if write is None else write[0], read[0])]
            (o, l), = _attention_step(sets, k_refs, v_refs, ATTN_CHUNK)
            o_ref[...] = (o / l).astype(o_ref.dtype)

        _skewed_steps(nq, ([(s_a, m_a)], [(s_b, m_b)]), emit)

    block_bytes = (_nbytes((tq, MLA_HEAD_PAD), BF16) + _nbytes((sk, MLA_HEAD_PAD), BF16)
                   + _nbytes((sk, V_DIM), BF16) + _nbytes((tq, V_DIM), BF16))
    score_bytes = _nbytes((tq, sk), F32)
    return pl.pallas_call(
        body,
        grid=(b, heads, nq + 1),
        in_specs=([pl.BlockSpec((None, tq, MLA_HEAD_PAD), q_map)]
                  + _kv_specs(ks, MLA_HEAD_PAD) + _kv_specs(vs, V_DIM)),
        out_specs=pl.BlockSpec((None, tq, V_DIM), o_map),
        out_shape=jax.ShapeDtypeStruct((b, sq, heads * V_DIM), BF16),
        scratch_shapes=[pltpu.VMEM((tq, sk), F32), pltpu.VMEM((tq, LANES), F32)] * 2,
        compiler_params=_params(("parallel", "parallel", "arbitrary"), block_bytes, 4 * score_bytes),
        name="mla_attention",
    )(q, *ks, *vs)


def _diff_attention(q, ks, vs, lam_p, subln_g, lambda_init, heads, *, tq=ATTN_TQ):
    b, sq, _ = q.shape
    sk = sum(k.shape[1] for k in ks)
    tq = min(tq, sq)
    hw = 2 * DIFF_DIM
    nkv = len(ks)
    nq = sq // tq
    q_map, o_map = _skewed_maps(nq)

    def body(lam_ref, g_ref, q_ref, *refs):
        k_refs, v_refs = refs[:nkv], refs[nkv:2 * nkv]
        o_ref = refs[2 * nkv]
        bufs = refs[2 * nkv + 1:]
        set_a = [(bufs[0], bufs[1]), (bufs[2], bufs[3])]
        set_b = [(bufs[4], bufs[5]), (bufs[6], bufs[7])]

        def emit(write, read):
            halves = [(q_ref[:, :DIFF_DIM], slice(0, DIFF_DIM)), (q_ref[:, DIFF_DIM:], slice(DIFF_DIM, hw))]
            sets = [ScoreSet(qh, cols, None if write is None else write[n], read[n])
                    for n, (qh, cols) in enumerate(halves)]
            (o1, l1), (o2, l2) = _attention_step(sets, k_refs, v_refs, ATTN_CHUNK)
            lp = lam_ref[...]
            lam = (jnp.exp(jnp.sum(lp[0:1] * lp[1:2], axis=-1, keepdims=True))
                   - jnp.exp(jnp.sum(lp[2:3] * lp[3:4], axis=-1, keepdims=True)) + lambda_init)
            o = o1 / l1 - (lam / l2) * o2
            o_ref[...] = (_rms(o, g_ref[...]) * (1.0 - lambda_init)).astype(o_ref.dtype)

        _skewed_steps(nq, (set_a, set_b), emit)

    block_bytes = 2 * _nbytes((tq, hw), BF16) + 2 * _nbytes((sk, hw), BF16)
    score_bytes = _nbytes((tq, sk), F32)
    return pl.pallas_call(
        body,
        grid=(b, heads, nq + 1),
        in_specs=([pl.BlockSpec((4, DIFF_DIM), lambda bb, h, i: (0, 0)),
                   pl.BlockSpec((1, hw), lambda bb, h, i: (0, 0)),
                   pl.BlockSpec((None, tq, hw), q_map)]
                  + _kv_specs(ks, hw) + _kv_specs(vs, hw)),
        out_specs=pl.BlockSpec((None, tq, hw), o_map),
        out_shape=jax.ShapeDtypeStruct((b, sq, heads * hw), BF16),
        scratch_shapes=[pltpu.VMEM((tq, sk), F32), pltpu.VMEM((tq, LANES), F32)] * 4,
        compiler_params=_params(("parallel", "parallel", "arbitrary"), block_bytes, 6 * score_bytes),
        name="diff_attention",
    )(lam_p.astype(F32), subln_g.reshape(1, hw).astype(F32), q, *ks, *vs)


def _hyena_filters(seq, w1, b1, w2, b2, w3, b3, freq, w_out, *, tl=256):
    hid = w1.shape[1]
    width2 = w_out.shape[1]
    width = width2 // 2
    bands = (HY_EMB - 1) // 2
    t = jnp.linspace(0.0, 1.0, seq, dtype=F32)[:, None]
    ang = ((2.0 * math.pi / seq) * jnp.arange(seq, dtype=F32)[:, None]
           * jnp.linspace(1e-4, bands - 1, bands, dtype=F32)[None, :])
    z = jnp.concatenate([t, jnp.cos(ang), -jnp.sin(ang), jnp.zeros((seq, LANES - HY_EMB), F32)], axis=-1)
    z = jnp.concatenate([z[0::2], z[1::2]], axis=0)
    deltas =jnp.abs(jnp.linspace(math.log(HY_TARGET) / HY_LONG_DECAY_PCT,
                                  math.log(HY_TARGET) / HY_SHORT_DECAY_PCT, width, dtype=F32))
    deltas2 = jnp.concatenate([deltas, deltas])[None, :]

    def pad2(w):
        return jnp.pad(w.astype(F32), ((0, LANES - w.shape[0]), (0, LANES - w.shape[1])))

    def pad_row(v):
        return jnp.pad(v.astype(F32), (0, LANES - v.shape[0]))[None, :]

    w_out_p = jnp.pad(w_out.astype(F32), ((0, LANES - hid), (0, 0)))
    tl, tn = min(tl, seq), width2
    exact = lax.Precision.HIGHEST

    def body(z_ref, w1_ref, b1_ref, w2_ref, b2_ref, w3_ref, b3_ref, f_ref, wo_ref, d_ref, o_ref, mid_ref):
        zz, f = z_ref[...], f_ref[...]
        a = jnp.sin(f * (jnp.dot(zz, w1_ref[...], precision=exact, preferred_element_type=F32) + b1_ref[...]))
        a = jnp.sin(f * (jnp.dot(a, w2_ref[...], precision=exact, preferred_element_type=F32) + b2_ref[...]))
        a = jnp.sin(f * (jnp.dot(a, w3_ref[...], precision=exact, preferred_element_type=F32) + b3_ref[...]))
        h = jnp.dot(a, wo_ref[...], precision=exact, preferred_element_type=F32)
        h = h * jnp.exp(-zz[:, 0:1] * d_ref[...])
        row = pl.program_id(0) * tl + lax.broadcasted_iota(jnp.int32, (tl, width), 0)
        h_fwd = h[:, :width]
        h_bwd = jnp.where(row == 0, 0.0, h[:, width:])
        h_sum = h_fwd + h_bwd
        o_ref[:, :width] = h_sum.astype(o_ref.dtype)
        o_ref[:, width:] = (h_fwd - h_bwd).astype(o_ref.dtype)
        signed = jnp.where(row >= seq // 2, 0.0, jnp.where((row & 1) == 0, h_sum, -h_sum))
        alternating = jnp.sum(signed, axis=0, keepdims=True)

        @pl.when(pl.program_id(0) == 0)
        def _():
            mid_ref[...] = alternating

        @pl.when(pl.program_id(0) > 0)
        def _():
            mid_ref[...] += alternating

    sq = pl.BlockSpec((LANES, LANES), lambda i: (0, 0))
    vec = pl.BlockSpec((1, LANES), lambda i: (0, 0))
    block_bytes = _nbytes((tl, LANES), F32) + _nbytes((LANES, tn), F32) + 2 * _nbytes((tl, tn), F32)
    return pl.pallas_call(
        body,
        grid=(seq // tl,),
        in_specs=[pl.BlockSpec((tl, LANES), lambda i: (i, 0)), sq, vec, sq, vec, sq, vec, vec,
                  pl.BlockSpec((LANES, tn), lambda i: (0, 0)),
                  pl.BlockSpec((1, tn), lambda i: (0, 0))],
        out_specs=[pl.BlockSpec((tl, tn), lambda i: (i, 0)), pl.BlockSpec((1, width), lambda i: (0, 0))],
        out_shape=[jax.ShapeDtypeStruct((seq, width2), BF16), jax.ShapeDtypeStruct((1, width), F32)],
        compiler_params=_params(("arbitrary",), block_bytes, 4 * _nbytes((tl, tn), F32)),
        name="hyena_filters",
    )(z, pad2(w1), pad_row(b1), pad2(w2), pad_row(b2), pad2(w3), pad_row(b3), pad_row(freq), w_out_p, deltas2)


def _hyena_prep(zh, conv_w, conv_b, *, tc=128):
    b, seq, w3 = zh.shape
    width = w3 // 3
    half = seq // 2
    tc = min(tc, width)
    nw = width // tc

    def body(z0_ref, z1_ref, z2_ref, w0_ref, w1_ref, w2_ref, b0_ref, b1_ref, b2_ref, x0_ref, vx_ref):
        row = lax.broadcasted_iota(jnp.int32, (half, 1), 0)

        def conv(z_ref, w_ref, b_ref):
            ev = z_ref[pl.ds(0, half, stride=2), :]
            od = z_ref[pl.ds(1, half, stride=2), :]
            w, bias = w_ref[...], b_ref[...]
            before_even = jnp.where(row == 0, 0.0, pltpu.roll(od, 1, 0))
            after_odd = jnp.where(row == half - 1, 0.0, pltpu.roll(ev, half - 1, 0))
            return (before_even * w[0:1] + ev * w[1:2] + od * w[2:3] + bias,
                    ev * w[0:1] + od * w[1:2] + after_odd * w[2:3] + bias)

        x0 = conv(z0_ref, w0_ref, b0_ref)
        x1 = conv(z1_ref, w1_ref, b1_ref)
        v = conv(z2_ref, w2_ref, b2_ref)
        for parity in range(2):
            x0_ref[parity] = x0[parity]
            vx_ref[parity] = (v[parity] * x1[parity]).astype(vx_ref.dtype)

    def zspec(part):
        return pl.BlockSpec((None, seq, tc), lambda bb, j: (bb, 0, part * nw + j))

    def wspec(rows, part):
        return pl.BlockSpec((rows, tc), lambda bb, j: (0, part * nw + j))

    out_spec = pl.BlockSpec((2, half, tc), lambda bb, j: (0, 0, bb * nw + j))
    block_bytes = 4 * _nbytes((seq, tc), F32) + _nbytes((seq, tc), BF16)
    return pl.pallas_call(
        body,
        grid=(b, nw),
        in_specs=[zspec(0), zspec(1), zspec(2),
                  wspec(3, 0), wspec(3, 1), wspec(3, 2), wspec(1, 0), wspec(1, 1), wspec(1, 2)],
        out_specs=[out_spec, out_spec],
        out_shape=[jax.ShapeDtypeStruct((2, half, b * width), F32),
                   jax.ShapeDtypeStruct((2, half, b * width), BF16)],
        compiler_params=_params(("parallel", "parallel"), block_bytes, 8 * _nbytes((seq, tc), F32)),
        name="hyena_short_conv",
    )(zh, zh, zh, conv_w, conv_w, conv_w, conv_b[None, :], conv_b[None, :], conv_b[None, :])


def _dft_matrices(seq):
    n = 2 * seq
    half = seq // 2
    assert seq % 2 == 0 and n % 4 == 0
    idx = jnp.arange(half, dtype=jnp.int32)
    alt = jnp.where((idx & 1) == 0, 1.0, -1.0).astype(F32)

    def cos_sin(k, s):
        phase = _mod_nonneg(k * s, n)
        return _cos_turns(phase, n), _cos_turns(_mod_nonneg(phase + (n - n // 4), n), n)

    def forward(c, s):
        return jnp.concatenate([c, jnp.where(idx[:, None] == 0, alt[None, :], s)], axis=0).astype(BF16)

    def inverse(c, s):
        return jnp.concatenate([c, jnp.where(idx[None, :] == 0, alt[:, None], s)], axis=1).astype(BF16)

    even = cos_sin(idx[:, None], 2 * idx[None, :])
    odd = cos_sin(idx[:, None], 2 * idx[None, :] + 1)
    odd_t = cos_sin(idx[None, :], 2 * idx[:, None] + 1)
    return forward(*even), forward(*odd), inverse(*even), inverse(*odd_t)


def _div_nonneg(x, d):
    return x >> (d.bit_length() - 1) if d & (d - 1) == 0 else x // d


def _mod_nonneg(x, d):
    return x & (d - 1) if d & (d - 1) == 0 else x % d


def _cos_turns(phase, n):
    quarter = n // 4
    quad = _div_nonneg(phase, quarter)
    rem = phase - quad * quarter
    odd = (quad & 1) == 1
    x = jnp.where(odd, quarter - rem, rem).astype(F32) * (2.0 * math.pi / n)
    x2 = x * x
    acc = jnp.full_like(x2, 1.0 / math.factorial(16))
    for order in range(14, -1, -2):
        acc = acc * (-x2) + 1.0 / math.factorial(order)
    return jnp.where((quad == 1) | (quad == 2), -acc, acc)


def _spectrum_multiply(eu, ou, eg, og, mid_g, batch, width, *, tr=256, tc=512):
    seq = eu.shape[0]
    half = seq // 2
    n = 2 * seq
    tr, tc = min(tr, half), min(tc, width)
    nwc = width // tc

    def body(eu_ref, ou_ref, eg_ref, og_ref, mid_ref, pe_ref, po_ref):
        row = pl.program_id(0) * tr + lax.broadcasted_iota(jnp.int32, (tr, tc), 0)
        first = row == 0

        def pair(e_ref, o_ref):
            ec, es, oc, os_ = e_ref[0], e_ref[1], o_ref[0], o_ref[1]
            return (ec + oc, jnp.where(first, 0.0, es + os_)), (ec - oc, jnp.where(first, 0.0, os_ - es))

        def product(u, g):
            return u[0] * g[0] - u[1] * g[1], u[0] * g[1] + u[1] * g[0]

        (u_lo, u_hi), (g_lo, g_hi) = pair(eu_ref, ou_ref), pair(eg_ref, og_ref)
        ya_lo, yb_lo = product(u_lo, g_lo)
        ya_hi, yb_hi = product(u_hi, g_hi)
        ya_mid, yb_mid = product((eu_ref[1], ou_ref[1]), (mid_ref[...], og_ref[1]))
        w_cos = jnp.where(first, 1.0 / n, 2.0 / n)
        w_sin = 2.0 / n
        pe_ref[0] = (w_cos * (ya_lo + ya_hi)).astype(pe_ref.dtype)
        po_ref[0] = (w_cos * (ya_lo - ya_hi)).astype(po_ref.dtype)
        pe_ref[1] = (w_sin * jnp.where(first, ya_mid, yb_lo - yb_hi)).astype(pe_ref.dtype)
        po_ref[1] = (w_sin * jnp.where(first, yb_mid, yb_lo + yb_hi)).astype(po_ref.dtype)

    u_spec = pl.BlockSpec((2, tr, tc), lambda i, j, bb: (0, i, bb * nwc + j))
    g_spec = pl.BlockSpec((2, tr, tc), lambda i, j, bb: (0, i, j))
    block_bytes = 4 * _nbytes((2, tr, tc), F32) + 2 * _nbytes((2, tr, tc), BF16)
    out_sds = jax.ShapeDtypeStruct((2, half, batch * width), BF16)
    pe, po = pl.pallas_call(
        body,
        grid=(half // tr, nwc, batch),
        in_specs=[u_spec, u_spec, g_spec, g_spec, pl.BlockSpec((1, tc), lambda i, j, bb: (0, j))],
        out_specs=[u_spec, u_spec],
        out_shape=[out_sds, out_sds],
        compiler_params=_params(("parallel", "parallel", "arbitrary"), block_bytes, 24 * _nbytes((tr, tc), F32)),
        name="hyena_spectrum_multiply",
    )(eu.reshape(2, half, batch * width), ou.reshape(2, half, batch * width),
      eg.reshape(2, half, width), og.reshape(2, half, width), mid_g)
    return pe.reshape(seq, batch * width), po.reshape(seq, batch * width)


def _hyena_branch(zh, conv_w, conv_b, filt, hy_bias):
    b, seq, w3 = zh.shape
    width = w3 // 3
    half = seq // 2
    x0, vx = _hyena_prep(zh, conv_w.astype(F32), conv_b.astype(F32))
    h_sum_dif, mid_g = _hyena_filters(seq, *filt)
    h_sum_dif = h_sum_dif.reshape(2, half, 2 * width)
    mats = _dft_matrices(seq)
    forward_mats, inverse_mats = mats[:2], mats[2:]
    tm, tn = min(TILE_M, half), min(TILE_N, width)
    nw = width // tn

    def store(acc, ex, outs):
        outs[0][...] = acc

    def forward(mat, samples, parity, cols, col_shift, name):
        return _matmul(mat, samples, tm=tm, tn=tn, tk=TILE_K, epilogue=store, w_index=(parity,),
                       w_cols=(0, cols), w_col_shift=col_shift,
                       out_shape=[jax.ShapeDtypeStruct((seq, cols), F32)],
                       out_specs=[pl.BlockSpec((tm, tn), lambda i, j, k: (i, j))], name=name)[0]

    def filter_cols(i):
        return (i // (half // tm)) * nw

    data = [forward(m, vx, p, b * width, None, "hyena_dft_forward") for p, m in enumerate(forward_mats)]
    filt_spec = [forward(m, h_sum_dif, p, width, filter_cols, "hyena_dft_filters")
                 for p, m in enumerate(forward_mats)]
    spectra = _spectrum_multiply(data[0], data[1], filt_spec[0], filt_spec[1], mid_g, b, width)

    return _hyena_inverse(inverse_mats, spectra, x0, vx, hy_bias.astype(F32)[None, :], b, width)


def _hyena_inverse(mats, spectra, x0, vx, bias, batch, width):
    half, seq = mats[0].shape
    tm, tn = min(TILE_M // 2, half), min(TILE_N, width)
    nw = width // tn

    def body(me_ref, mo_ref, pe_ref, po_ref, x0_ref, vx_ref, bias_ref, o_ref, rows_ref):
        for parity, (m_ref, p_ref) in enumerate(((me_ref, pe_ref), (mo_ref, po_ref))):
            conv = jnp.dot(m_ref[...], p_ref[...], preferred_element_type=F32)
            gated = x0_ref[parity] * (conv + bias_ref[...] * vx_ref[parity].astype(F32))
            for c in range(tn // LANES):
                rows_ref[c, pl.ds(parity, tm, stride=2), :] = gated[:, c * LANES:(c + 1) * LANES]
        for c in range(tn // LANES):
            o_ref[:, c * LANES:(c + 1) * LANES] = rows_ref[c].astype(o_ref.dtype)

    mat_spec = pl.BlockSpec((tm, seq), lambda i, j: (i, 0))
    spec_spec = pl.BlockSpec((seq, tn), lambda i, j: (0, j))
    split_spec = pl.BlockSpec((2, tm, tn), lambda i, j: (0, i, j))
    block_bytes = (2 * _nbytes((tm, seq), BF16) + 2 * _nbytes((seq, tn), BF16) + _nbytes((2, tm, tn), F32)
                   + 2 * _nbytes((2, tm, tn), BF16))
    return pl.pallas_call(
        body,
        grid=(half // tm, batch * width // tn),
        in_specs=[mat_spec, mat_spec, spec_spec, spec_spec, split_spec, split_spec,
                  pl.BlockSpec((1, tn), lambda i, j: (0, j % nw))],
        out_specs=pl.BlockSpec((None, 2 * tm, tn), lambda i, j: (j // nw, i, j % nw)),
        out_shape=jax.ShapeDtypeStruct((batch, 2 * half, width), BF16),
        scratch_shapes=[pltpu.VMEM((tn // LANES, 2 * tm, LANES), F32)],
        compiler_params=_params(("parallel", "parallel"), block_bytes, 8 * _nbytes((2 * tm, tn), F32)),
        name="hyena_dft_inverse",
    )(mats[0], mats[1], spectra[0], spectra[1], x0, vx, bias)


def _ab_latents(h, w1, q_norm_g, kv_norm_g, tables, seq, q_lora, kv_lora, *, tm=512, tk=1024):
    m = h.shape[0]
    n1 = w1.shape[1]
    tm = min(tm, m, seq)
    tiles_per_seq = seq // tm

    def epilogue(acc, ex, outs):
        outs[0][...] = _rms(acc[:, :q_lora], ex[0][...]).astype(BF16)
        outs[1][...] = _rms(acc[:, q_lora:q_lora + kv_lora], ex[1][...]).astype(BF16)
        kr = acc[:, q_lora + kv_lora:]
        if tables is not None:
            kr = _rope_apply(kr, ex[2], ex[3], MLA_ROPE_LAYOUT)
        outs[2][...] = kr

    extras = [(q_norm_g.astype(F32)[None, :], pl.BlockSpec((1, q_lora), lambda i, j, k: (0, 0))),
              (kv_norm_g.astype(F32)[None, :], pl.BlockSpec((1, kv_lora), lambda i, j, k: (0, 0)))]
    if tables is not None:
        spec = pl.BlockSpec((tm, LANES), lambda i, j, k: (i % tiles_per_seq, 0))
        extras += [(tables[0], spec), (tables[1], spec)]
    widths = (q_lora, kv_lora, MLA_HEAD_PAD)
    dtypes = (BF16, BF16, F32)
    return _matmul(h, w1, tm=tm, tn=n1, tk=tk, epilogue=epilogue, extras=extras,
                   out_shape=[jax.ShapeDtypeStruct((m, wd), dt) for wd, dt in zip(widths, dtypes)],
                   out_specs=[pl.BlockSpec((tm, wd), lambda i, j, k: (i, 0)) for wd in widths],
                   name="mla_latent_projection")


def _mla_keys(ckv, w_uk, kr, heads, *, tm=1024, tn=1024):
    m = ckv.shape[0]
    n = heads * MLA_HEAD_PAD
    tm, tn = min(tm, m), min(tn, n)
    reps = tn // MLA_HEAD_PAD

    def epilogue(acc, ex, outs):
        rot = ex[0][...]
        if reps > 1:
            rot = jnp.concatenate([rot] * reps, axis=1)
        outs[0][...] = (acc + rot).astype(BF16)

    extras = [(kr, pl.BlockSpec((tm, MLA_HEAD_PAD), lambda i, j, k: (i, 0)))]
    return _matmul(ckv, w_uk, tm=tm, tn=tn, tk=ckv.shape[1], epilogue=epilogue, extras=extras,
                   out_shape=[jax.ShapeDtypeStruct((m, n), BF16)],
                   out_specs=[pl.BlockSpec((tm, tn), lambda i, j, k: (i, j))], name="mla_key_up")[0]


def _ab_mixer(h, hc, batch, seq, seq_c, need_ctx, w_in, q_norm_g, kv_norm_g, w_uq, w_ukv,
              conv_w, conv_b, filt, hy_bias):
    d = w_in.shape[0]
    heads = d // (2 * V_DIM)
    q_lora, kv_lora = w_uq.shape[0], w_ukv.shape[0]
    kv_end = q_lora + kv_lora + ROPE_DIM
    mla_scale = (NOPE_DIM + ROPE_DIM) ** -0.5
    zero_pad = MLA_HEAD_PAD - NOPE_DIM - ROPE_DIM

    w1 = jnp.concatenate([w_in[:, :q_lora + kv_lora], jnp.zeros((d, NOPE_DIM), w_in.dtype),
                          w_in[:, q_lora + kv_lora:kv_end], jnp.zeros((d, zero_pad), w_in.dtype)],
                         axis=1).astype(BF16)
    w_hy = w_in[:, kv_end:].astype(BF16)
    uq = w_uq.reshape(q_lora, heads, NOPE_DIM + ROPE_DIM)
    w_uq_p = jnp.pad(uq, ((0, 0), (0, 0), (0, zero_pad))).reshape(q_lora, heads * MLA_HEAD_PAD).astype(BF16)
    ukv = w_ukv.reshape(kv_lora, heads, NOPE_DIM + V_DIM)
    w_uk_p = jnp.pad(ukv[..., :NOPE_DIM], ((0, 0), (0, 0), (0, MLA_HEAD_PAD - NOPE_DIM))
                     ).reshape(kv_lora, heads * MLA_HEAD_PAD).astype(BF16)
    w_uv = ukv[..., NOPE_DIM:].reshape(kv_lora, heads * V_DIM).astype(BF16)

    tables = _rope_tables(seq, MLA_ROPE_LAYOUT)
    rope_kw = dict(layout=MLA_ROPE_LAYOUT)

    cq, ckv, kr = _ab_latents(h, w1, q_norm_g, kv_norm_g, tables, seq, q_lora, kv_lora)
    cqc, ckvc, krc = _ab_latents(hc, w1, q_norm_g, kv_norm_g, None, seq_c, q_lora, kv_lora)
    q = _mm_rope(cq, w_uq_p, tables, mla_scale, seq, BF16, name="mla_query_up", **rope_kw)
    k = _mla_keys(ckv, w_uk_p, kr, heads)
    v = _mm_plain(ckv, w_uv, BF16, name="mla_value_up")
    kc = _mla_keys(ckvc, w_uk_p, krc, heads)
    vc = _mm_plain(ckvc, w_uv, BF16, name="mla_value_up")

    def split(t, s):
        return t.reshape(batch, s, t.shape[-1])

    att = _mla_attention(split(q, seq), [split(kc, seq_c), split(k, seq)], [split(vc, seq_c), split(v, seq)], heads)
    zh = _mm_plain(h, w_hy, F32, tn=512, name="hyena_in_projection")
    hy = _hyena_branch(split(zh, seq), conv_w, conv_b, filt, hy_bias)
    def flat(t):
        return t.reshape(t.shape[0] * t.shape[1], t.shape[2])

    mix = [flat(att), flat(hy)]
    if not need_ctx:
        return mix, None
    qc = _mm_rope(cqc, w_uq_p, None, mla_scale, seq_c, BF16, name="mla_query_up", **rope_kw)
    attc = _mla_attention(split(qc, seq_c), [split(kc, seq_c)], [split(vc, seq_c)], heads)
    zhc = _mm_plain(hc, w_hy, F32, tn=512, name="hyena_in_projection")
    hyc = _hyena_branch(split(zhc, seq_c), conv_w, conv_b, filt, hy_bias)
    return mix, [flat(attc), flat(hyc)]


def _diff_mixer(h, hc, batch, seq, seq_c, need_ctx, w_in, lam_p, subln_g, lambda_init):
    d = w_in.shape[0]
    hd = w_in.shape[1] // 3
    heads = hd // (2 * DIFF_DIM)
    scale = DIFF_DIM ** -0.5
    w = _cast_deinterleave_columns(w_in, 2 * hd)
    q_cols, k_cols, v_cols = (0, hd), (hd, hd), (2 * hd, hd)
    tables = _rope_tables(seq, DIFF_ROPE_LAYOUT)
    rope_kw = dict(layout=DIFF_ROPE_LAYOUT)

    q = _mm_rope(h, w, tables, scale, seq, BF16, w_cols=q_cols, name="diff_query_projection", **rope_kw)
    k = _mm_rope(h, w, tables, 1.0, seq, BF16, w_cols=k_cols, name="diff_key_projection", **rope_kw)
    v = _mm_plain(h, w, BF16, w_cols=v_cols, name="diff_value_projection")
    kc = _mm_rope(hc, w, None, 1.0, seq_c, BF16, w_cols=k_cols, name="diff_key_projection", **rope_kw)
    vc = _mm_plain(hc, w, BF16, w_cols=v_cols, name="diff_value_projection")

    def split(t, s):
        return t.reshape(batch, s, t.shape[-1])

    o = _diff_attention(split(q, seq), [split(kc, seq_c), split(k, seq)], [split(vc, seq_c), split(v, seq)],
                        lam_p, subln_g, lambda_init, heads)
    mix = o.reshape(batch * seq, hd)
    if not need_ctx:
        return mix, None
    qc = _mm_rope(hc, w, None, scale, seq_c, BF16, w_cols=q_cols, name="diff_query_projection", **rope_kw)
    oc = _diff_attention(split(qc, seq_c), [split(kc, seq_c)], [split(vc, seq_c)], lam_p, subln_g, lambda_init, heads)
    return mix, oc.reshape(batch * seq_c, hd)


def kernel(x, c, ctx, c_ctx, mod_w, mod_b, norm_g, ffn_w_gate, ffn_w_up, ffn_w_down, ab_w_in, mla_q_norm_g, mla_kv_norm_g, mla_w_uq, mla_w_ukv, hy_conv_w, hy_conv_b, hy_w1, hy_b1, hy_w2, hy_b2, hy_w3, hy_b3, hy_freq, hy_w_out, hy_bias, ab_w_out, c_w_in, c_lambda, c_subln_g, c_w_out, final_norm_g):
    batch, seq, d = x.shape
    seq_c = ctx.shape[1]
    depth = mod_w.shape[0]
    rows_c = batch * seq_c
    xs = x.reshape(batch * seq, d).astype(F32)
    xc = ctx.reshape(rows_c, d).astype(F32)

    cvec = jnp.concatenate([c.astype(F32), c_ctx.astype(F32)[None, :]], axis=0)
    cvec = jnp.pad(cvec, ((0, -cvec.shape[0] % 8), (0, 0)))
    mods_all = _modulation(cvec, mod_w.astype(F32), mod_b.astype(F32))

    for layer in range(depth):
        need_ctx = layer < depth - 1
        mods = mods_all[layer, :batch].reshape(batch, 3 * N_SUB, d)
        modc = mods_all[layer, batch:batch + 1].reshape(1, 3 * N_SUB, d)
        g = norm_g[layer].astype(F32)

        def ffn(t, m, rows, sub, which):
            return _half_ffn(t, m, sub, rows, g[sub], ffn_w_gate, ffn_w_up, ffn_w_down, (layer, which))

        xs = ffn(xs, mods, seq, 0, 0)
        xc = ffn(xc, modc, rows_c, 0, 0)
        h = _norm_mod(xs, g[1], mods, 1, seq, BF16)
        hc = _norm_mod(xc, g[1], modc, 1, rows_c, BF16)
        i = layer // 2
        if layer % 2 == 0:
            filt = (hy_w1[i], hy_b1[i], hy_w2[i], hy_b2[i], hy_w3[i], hy_b3[i], hy_freq[i], hy_w_out[i])
            mix, mixc = _ab_mixer(h, hc, batch, seq, seq_c, need_ctx, ab_w_in[i], mla_q_norm_g[i],
                                  mla_kv_norm_g[i], mla_w_uq[i], mla_w_ukv[i], hy_conv_w[i], hy_conv_b[i],
                                  filt, hy_bias[i])
            w_out = ab_w_out[i].astype(BF16)
        else:
            lambda_init = 0.8 - 0.6 * math.exp(-0.3 * layer)
            mix, mixc = _diff_mixer(h, hc, batch, seq, seq_c, need_ctx, c_w_in[i], c_lambda[i], c_subln_g[i],
                                    lambda_init)
            w_out = c_w_out[i].astype(BF16)
        xs = _mm_residual(mix, w_out, xs, mods, 1, 1.0, seq, name="mixer_out_residual")
        xs = ffn(xs, mods, seq, 2, 1)
        if need_ctx:
            xc = _mm_residual(mixc, w_out, xc, modc, 1, 1.0, rows_c, name="mixer_out_residual")
            xc = ffn(xc, modc, rows_c, 2, 1)

    out = _norm_mod(xs, final_norm_g.astype(F32), None, 0, seq, x.dtype)
    return out.reshape(batch, seq, d)
```

```python
import functools
import math
from typing import NamedTuple

import numpy as np
import jax
import jax.numpy as jnp
from jax import lax
from jax.experimental import pallas as pl
from jax.experimental.pallas import tpu as pltpu

F32 = jnp.float32
BF16 = jnp.bfloat16

EPS = 1e-6
ROPE_THETA = 10000.0
GRID_W = 64
N_SUB = 3
NOPE_DIM = 128
ROPE_DIM = 64
V_DIM = 128
DIFF_DIM = 128
HY_EMB = 33
HY_TARGET = 1e-2
HY_SHORT_DECAY_PCT = 0.3
HY_LONG_DECAY_PCT = 1.5

LANES = 128
MXU_DIM_V7X = 256
VMEM_BYTES_V7X = 64 * 1024 * 1024
VMEM_COMPILER_RESERVE = 8 * 1024 * 1024
MLA_HEAD_PAD = MXU_DIM_V7X

TILE_M = 1024
TILE_N = 512
TILE_K = 4096
TILE_K_LONG = 2 * TILE_K
NORM_ROWS = 32
ATTN_TQ = 256
ATTN_CHUNK = 256


def _params(semantics, block_bytes, temp_bytes=0):
    want = 2 * block_bytes + temp_bytes
    limit = min(VMEM_BYTES_V7X - VMEM_COMPILER_RESERVE, max(want, 32 * 1024 * 1024))
    return pltpu.CompilerParams(dimension_semantics=semantics, vmem_limit_bytes=int(limit))


def _nbytes(shape, dtype):
    return math.prod(shape) * jnp.dtype(dtype).itemsize


def _rms(x, g):
    return x * lax.rsqrt(jnp.mean(x * x, axis=-1, keepdims=True) + EPS) * g


def _matmul(a, w, *, tm, tn, tk, epilogue, out_shape, out_specs, extras=(), w_index=(), w_cols=None,
            w_col_shift=None, resident_a=False, name):
    a_parts = list(a) if isinstance(a, (list, tuple)) else [a]
    m = a_parts[0].shape[0]
    kdim = sum(p.shape[1] for p in a_parts)
    assert w.ndim == 2 + len(w_index) and w.shape[-2] == kdim
    col0, n = (0, w.shape[-1]) if w_cols is None else w_cols
    tm, tn, tk = min(tm, m), min(tn, n), min(tk, kdim)
    assert m % tm == 0 and n % tn == 0 and kdim % tk == 0 and col0 % tn == 0, (name, a.shape, w.shape, tm, tn, tk)
    j0 = col0 // tn
    nk = kdim // tk
    n_ex, n_out = len(extras), len(out_shape)

    def body(*refs):
        a_refs, w_ref = refs[:n_a], refs[n_a]
        ex = refs[n_a + 1:n_a + 1 + n_ex]
        outs = refs[n_a + 1 + n_ex:n_a + 1 + n_ex + n_out]
        prod, k0 = None, 0
        for a_ref in a_refs:
            rows = a_ref.shape[1]
            part = jnp.dot(a_ref[...], w_ref[k0:k0 + rows, :].astype(BF16), preferred_element_type=F32)
            prod = part if prod is None else prod + part
            k0 += rows
        if nk == 1:
            epilogue(prod, ex, outs)
            return
        acc_ref = refs[-1]
        k = pl.program_id(2)

        @pl.when(k == 0)
        def _():
            acc_ref[...] = prod

        if nk > 2:
            @pl.when((k > 0) & (k < nk - 1))
            def _():
                acc_ref[...] += prod

        @pl.when(k == nk - 1)
        def _():
            epilogue(acc_ref[...] + prod, ex, outs)

    n_a = len(a_parts)
    assert n_a == 1 or nk == 1
    a_mode = pl.Buffered(1) if resident_a else None
    in_specs = [pl.BlockSpec((tm, tk if n_a == 1 else p.shape[1]), lambda i, j, k: (i, k), pipeline_mode=a_mode)
                for p in a_parts]
    in_specs += [pl.BlockSpec((None,) * len(w_index) + (tk, tn),
                             lambda i, j, k: (*w_index, k, j + j0 + (0 if w_col_shift is None else w_col_shift(i))))]
    in_specs += [spec for _, spec in extras]
    block_bytes = _nbytes((tm, tk), a_parts[0].dtype) + _nbytes((tk, tn), w.dtype)
    for arr, spec in extras:
        block_bytes += _nbytes([d for d in spec.block_shape if d is not None], arr.dtype)
    for sds, spec in zip(out_shape, out_specs):
        block_bytes += _nbytes([d for d in spec.block_shape if d is not None], sds.dtype)
    acc_bytes = _nbytes((tm, tn), F32)
    return pl.pallas_call(
        body,
        grid=(m // tm, n // tn, nk),
        in_specs=in_specs,
        out_specs=list(out_specs),
        out_shape=list(out_shape),
        scratch_shapes=[pltpu.VMEM((tm, tn), F32)] if nk > 1 else [],
        compiler_params=_params(("parallel", "parallel", "arbitrary"), block_bytes, 8 * acc_bytes),
        name=name,
    )(*a_parts, w, *[arr for arr, _ in extras])


def _mm_plain(a, w, out_dtype, *, tm=TILE_M, tn=TILE_N, tk=TILE_K, w_cols=None, name):
    m, n = a.shape[0], (w.shape[1] if w_cols is None else w_cols[1])
    tm, tn = min(tm, m), min(tn, n)

    def epilogue(acc, ex, outs):
        outs[0][...] = acc.astype(out_dtype)

    return _matmul(a, w, tm=tm, tn=tn, tk=tk, epilogue=epilogue, w_cols=w_cols,
                   out_shape=[jax.ShapeDtypeStruct((m, n), out_dtype)],
                   out_specs=[pl.BlockSpec((tm, tn), lambda i, j, k: (i, j))], name=name)[0]


def _mm_residual(a, w, res, mods, sub, coef, rows_per_group, *, tm=TILE_M, tn=TILE_N, tk=TILE_K, w_index=(),
                 resident_a=False, name):
    m, n = (a[0] if isinstance(a, (list, tuple)) else a).shape[0], w.shape[-1]
    tm, tn = min(tm, m, rows_per_group), min(tn, n)
    tiles_per_group = rows_per_group // tm
    gate_row = 3 * sub + 2

    def epilogue(acc, ex, outs):
        res_ref, mod_ref = ex
        gate = mod_ref[gate_row:gate_row + 1, :]
        outs[0][...] = res_ref[...] + (coef * gate) * acc

    extras = [(res, pl.BlockSpec((tm, tn), lambda i, j, k: (i, j))),
              (mods, pl.BlockSpec((None, 3 * N_SUB, tn), lambda i, j, k: (i // tiles_per_group, 0, j)))]
    return _matmul(a, w, tm=tm, tn=tn, tk=tk, epilogue=epilogue, extras=extras, w_index=w_index,
                   resident_a=resident_a, out_shape=[jax.ShapeDtypeStruct((m, n), F32)],
                   out_specs=[pl.BlockSpec((tm, tn), lambda i, j, k: (i, j))], name=name)[0]


class RopeLayout(NamedTuple):
    group: int
    lo: int
    n_freq: int
    split: bool


MLA_ROPE_LAYOUT = RopeLayout(group=MLA_HEAD_PAD, lo=NOPE_DIM, n_freq=ROPE_DIM // 4, split=False)
DIFF_ROPE_LAYOUT = RopeLayout(group=DIFF_DIM, lo=0, n_freq=DIFF_DIM // 4, split=True)


def _rope_tables(seq, layout):
    n_freq = layout.n_freq
    pos = jnp.arange(seq, dtype=jnp.int32)
    row_pos = (pos // GRID_W).astype(F32)
    col_pos = (pos % GRID_W).astype(F32)
    inv = ROPE_THETA ** (-jnp.arange(n_freq, dtype=F32) / n_freq)
    ang = jnp.concatenate([row_pos[:, None] * inv, col_pos[:, None] * inv], axis=-1)
    cos, sin = jnp.cos(ang), jnp.sin(ang)
    if layout.split:
        assert 4 * n_freq == LANES
        return jnp.concatenate([cos, cos], axis=-1), jnp.concatenate([-sin, sin], axis=-1)
    cos = jnp.repeat(cos, 2, axis=-1)
    sin = jnp.repeat(sin, 2, axis=-1) * jnp.tile(jnp.array([-1.0, 1.0], F32), 2 * n_freq)
    rest = LANES - 4 * n_freq
    return (jnp.concatenate([cos, jnp.ones((seq, rest), F32)], axis=-1),
            jnp.concatenate([sin, jnp.zeros((seq, rest), F32)], axis=-1))


def _rope_apply(x, cos_ref, sin_ref, layout):
    assert layout.lo % LANES == 0 and layout.group % LANES == 0 and x.shape[-1] % layout.group == 0
    cos, sin = cos_ref[...], sin_ref[...]
    slabs = []
    for c0 in range(0, x.shape[-1], LANES):
        slab = x[:, c0:c0 + LANES]
        if c0 % layout.group == layout.lo:
            if layout.split:
                partner = pltpu.roll(slab, LANES // 2, 1)
            else:
                even = (lax.broadcasted_iota(jnp.int32, slab.shape, 1) & 1) == 0
                partner = jnp.where(even, pltpu.roll(slab, LANES - 1, 1), pltpu.roll(slab, 1, 1))
            slab = slab * cos + partner * sin
        slabs.append(slab)
    return jnp.concatenate(slabs, axis=1) if len(slabs) > 1 else slabs[0]


def _cast_deinterleave_columns(w, n_perm):
    kdim, n = w.shape
    perm = np.concatenate([np.arange(0, LANES, 2), np.arange(1, LANES, 2)])
    p = np.zeros((LANES, LANES), np.float32)
    p[perm, np.arange(LANES)] = 1.0
    tm, tn = min(TILE_M, kdim), min(TILE_N, n)
    assert n_perm % tn == 0
    perm_blocks = n_perm // tn

    def body(w_ref, p_ref, o_ref):
        @pl.when(pl.program_id(1) < perm_blocks)
        def _():
            pm = p_ref[...]
            for c0 in range(0, tn, LANES):
                o_ref[:, c0:c0 + LANES] = jnp.dot(w_ref[:, c0:c0 + LANES].astype(BF16), pm,
                                                  preferred_element_type=F32).astype(BF16)

        @pl.when(pl.program_id(1) >= perm_blocks)
        def _():
            o_ref[...] = w_ref[...].astype(BF16)

    return pl.pallas_call(
        body,
        grid=(kdim // tm, n // tn),
        in_specs=[pl.BlockSpec((tm, tn), lambda i, j: (i, j)), pl.BlockSpec((LANES, LANES), lambda i, j: (0, 0))],
        out_specs=pl.BlockSpec((tm, tn), lambda i, j: (i, j)),
        out_shape=jax.ShapeDtypeStruct((kdim, n), BF16),
        compiler_params=_params(("parallel", "parallel"), _nbytes((tm, tn), w.dtype) + _nbytes((tm, tn), BF16)),
        name="cast_deinterleave_rotary_columns",
    )(w, jnp.asarray(p, BF16))


def _mm_rope(a, w, tables, scale, seq, out_dtype, *, layout, tm=TILE_M, tn=TILE_N, tk=TILE_K, w_cols=None, name):
    m, n = a.shape[0], (w.shape[1] if w_cols is None else w_cols[1])
    tm, tn = min(tm, m, seq), min(tn, n)
    tiles_per_seq = seq // tm

    def epilogue(acc, ex, outs):
        y = acc if tables is None else _rope_apply(acc, ex[0], ex[1], layout)
        outs[0][...] = (y * scale).astype(out_dtype)

    extras = []
    if tables is not None:
        spec = pl.BlockSpec((tm, LANES), lambda i, j, k: (i % tiles_per_seq, 0))
        extras = [(tables[0], spec), (tables[1], spec)]
    return _matmul(a, w, tm=tm, tn=tn, tk=tk, epilogue=epilogue, extras=extras, w_cols=w_cols,
                   out_shape=[jax.ShapeDtypeStruct((m, n), out_dtype)],
                   out_specs=[pl.BlockSpec((tm, tn), lambda i, j, k: (i, j))], name=name)[0]


def _modulation(cvec, mod_w, mod_b, *, tn=512):
    depth, d, n = mod_w.shape
    rows = cvec.shape[0]
    tn = min(tn, n)

    def body(c_ref, w_ref, b_ref, o_ref):
        cv = c_ref[...]
        s = cv * (1.0 / (1.0 + jnp.exp(-cv)))
        o_ref[...] = jnp.dot(s, w_ref[...], preferred_element_type=F32) + b_ref[...]

    block_bytes = _nbytes((d, tn), F32) + _nbytes((rows, d), F32) + 2 * _nbytes((rows, tn), F32)
    return pl.pallas_call(
        body,
        grid=(depth, n // tn),
        in_specs=[pl.BlockSpec((rows, d), lambda l, j: (0, 0)),
                  pl.BlockSpec((None, d, tn), lambda l, j: (l, 0, j)),
                  pl.BlockSpec((None, 1, tn), lambda l, j: (l, 0, j))],
        out_specs=pl.BlockSpec((None, rows, tn), lambda l, j: (l, 0, j)),
        out_shape=jax.ShapeDtypeStruct((depth, rows, n), F32),
        compiler_params=_params(("parallel", "parallel"), block_bytes),
        name="adaln_modulation",
    )(cvec, mod_w, mod_b.reshape(depth, 1, n))


def _norm_mod(x, g, mods, sub, rows_per_group, out_dtype, *, tm=512):
    m, d = x.shape
    tm = min(tm, m, rows_per_group)
    tiles_per_group = rows_per_group // tm

    def body(*refs):
        if mods is None:
            x_ref, g_ref, o_ref = refs
        else:
            x_ref, g_ref, mod_ref, o_ref = refs
        gain = g_ref[...]
        if mods is not None:
            gain = gain * (1.0 + mod_ref[3 * sub + 1:3 * sub + 2, :])
            shift = mod_ref[3 * sub:3 * sub + 1, :]
        rows = min(tm, NORM_ROWS)
        for r0 in range(0, tm, rows):
            xx = x_ref[r0:r0 + rows, :]
            y = xx * lax.rsqrt(jnp.mean(xx * xx, axis=-1, keepdims=True) + EPS) * gain
            if mods is not None:
                y = y + shift
            o_ref[r0:r0 + rows, :] = y.astype(out_dtype)

    in_specs = [pl.BlockSpec((tm, d), lambda i: (i, 0)), pl.BlockSpec((1, d), lambda i: (0, 0))]
    args = [x, g.reshape(1, d)]
    if mods is not None:
        in_specs.append(pl.BlockSpec((None, 3 * N_SUB, d), lambda i: (i // tiles_per_group, 0, 0)))
        args.append(mods)
    block_bytes = _nbytes((tm, d), F32) + _nbytes((tm, d), out_dtype) + _nbytes((16, d), F32)
    return pl.pallas_call(
        body,
        grid=(m // tm,),
        in_specs=in_specs,
        out_specs=pl.BlockSpec((tm, d), lambda i: (i, 0)),
        out_shape=jax.ShapeDtypeStruct((m, d), out_dtype),
        compiler_params=_params(("parallel",), block_bytes, 2 * _nbytes((tm, d), F32)),
        name="rmsnorm_modulate",
    )(*args)


def _gate_up(h, w_gate, w_up, w_index, *, tm=2 * TILE_M, tn=TILE_N, tk=TILE_K):
    m, kdim = h.shape
    n = w_gate.shape[-1]
    tn = tn * jnp.dtype(BF16).itemsize // jnp.dtype(w_gate.dtype).itemsize
    tm, tn, tk = min(tm, m), min(tn, n), min(tk, kdim)
    assert m % tm == 0 and n % tn == 0 and kdim % tk == 0
    nk = kdim // tk

    def finish(g, u, o_ref):
        o_ref[...] = (g * (1.0 / (1.0 + jnp.exp(-g))) * u).astype(o_ref.dtype)

    def body(h_ref, wg_ref, wu_ref, o_ref, *acc):
        hh = h_ref[...]
        pg = jnp.dot(hh, wg_ref[...].astype(BF16), preferred_element_type=F32)
        pu = jnp.dot(hh, wu_ref[...].astype(BF16), preferred_element_type=F32)
        if nk == 1:
            finish(pg, pu, o_ref)
            return
        accg, accu = acc
        k = pl.program_id(2)

        @pl.when(k == 0)
        def _():
            accg[...] = pg
            accu[...] = pu

        if nk > 2:
            @pl.when((k > 0) & (k < nk - 1))
            def _():
                accg[...] += pg
                accu[...] += pu

        @pl.when(k == nk - 1)
        def _():
            finish(accg[...] + pg, accu[...] + pu, o_ref)

    block_bytes = _nbytes((tm, tk), BF16) + 2 * _nbytes((tk, tn), w_gate.dtype) + _nbytes((tm, tn), BF16)
    acc_bytes = _nbytes((tm, tn), F32) + _nbytes((tk, tn), BF16)
    w_spec = pl.BlockSpec((None,) * len(w_index) + (tk, tn), lambda i, j, k: (*w_index, k, j))
    return pl.pallas_call(
        body,
        grid=(m // tm, n // tn, nk),
        in_specs=[pl.BlockSpec((tm, tk), lambda i, j, k: (i, k), pipeline_mode=pl.Buffered(1)), w_spec, w_spec],
        out_specs=pl.BlockSpec((tm, tn), lambda i, j, k: (i, j)),
        out_shape=jax.ShapeDtypeStruct((m, n), BF16),
        scratch_shapes=[pltpu.VMEM((tm, tn), F32)] * 2 if nk > 1 else [],
        compiler_params=_params(("parallel", "parallel", "arbitrary"), block_bytes, 6 * acc_bytes),
        name="ffn_gate_up",
    )(h, w_gate, w_up)


def _half_ffn(x, mods, sub, rows_per_group, g, w_gate, w_up, w_down, w_index):
    h = _norm_mod(x, g, mods, sub, rows_per_group, BF16)
    u = _gate_up(h, w_gate, w_up, w_index)
    tn = TILE_N * jnp.dtype(BF16).itemsize // jnp.dtype(w_down.dtype).itemsize
    return _mm_residual(u, w_down, x, mods, sub, 0.5, rows_per_group, w_index=w_index,
                        tm=TILE_M, tn=tn, tk=TILE_K_LONG, resident_a=True, name="ffn_down_residual")


class ScoreSet(NamedTuple):
    q: object
    k_cols: slice
    write: object
    read: object


def _lane_fold(x, op):
    return functools.reduce(op, [x[:, c0:c0 + LANES] for c0 in range(0, x.shape[-1], LANES)])


def _attention_step(sets, k_refs, v_refs, chunk):
    state = [dict(mx=None, ls=None, acc=None, m_old=s.read[1][...]) for s in sets]
    off = 0
    for k_ref, v_ref in zip(k_refs, v_refs):
        total = k_ref.shape[0]
        for c0 in range(0, total, chunk):
            rows = min(chunk, total - c0)
            cols = slice(off + c0, off + c0 + rows)
            for s, st in zip(sets, state):
                if s.write is not None:
                    new = lax.dot_general(s.q, k_ref[c0:c0 + rows, s.k_cols], (((1,), (1,)), ((), ())),
                                          preferred_element_type=F32)
                    s.write[0][:, cols] = new
                    part = _lane_fold(new, jnp.maximum)
                    st["mx"] = part if st["mx"] is None else jnp.maximum(st["mx"], part)
            for s, st in zip(sets, state):
                m_old = jnp.concatenate([st["m_old"]] * (rows // LANES), axis=1) if rows > LANES else st["m_old"]
                e = jnp.exp(s.read[0][:, cols] - m_old)
                part = _lane_fold(e, jnp.add)
                st["ls"] = part if st["ls"] is None else st["ls"] + part
                pv = jnp.dot(e.astype(BF16), v_ref[c0:c0 + rows, :], preferred_element_type=F32)
                st["acc"] = pv if st["acc"] is None else st["acc"] + pv
        off += total
    out = []
    for s, st in zip(sets, state):
        if s.write is not None:
            s.write[1][...] = jnp.broadcast_to(jnp.max(st["mx"], axis=-1, keepdims=True), st["mx"].shape)
        out.append((st["acc"], jnp.sum(st["ls"], axis=-1, keepdims=True)))
    return out


def _skewed_steps(nq, score_bufs, emit):
    i = pl.program_id(2)
    buf_a, buf_b = score_bufs

    @pl.when(i == 0)
    def _():
        for pair in buf_b:
            for ref in pair:
                ref[...] = jnp.zeros_like(ref)

    even = lax.rem(i, 2) == 0

    @pl.when((i < nq) & even)
    def _():
        emit(buf_a, buf_b)

    @pl.when((i < nq) & jnp.logical_not(even))
    def _():
        emit(buf_b, buf_a)

    @pl.when(i == nq)
    def _():
        emit(None, buf_a if (nq - 1) % 2 == 0 else buf_b)


def _skewed_maps(nq):
    return (lambda bb, h, i: (bb, jnp.minimum(i, nq - 1), h)), (lambda bb, h, i: (bb, jnp.maximum(i - 1, 0), h))


def _kv_specs(arrays, width):
    return [pl.BlockSpec((None, a.shape[1], width), lambda bb, h, i: (bb, 0, h)) for a in arrays]


def _mla_attention(q, ks, vs, heads, *, tq=2 * ATTN_TQ):
    b, sq, _ = q.shape
    sk = sum(k.shape[1] for k in ks)
    tq = min(tq, sq)
    nkv = len(ks)

    nq = sq // tq
    q_map, o_map = _skewed_maps(nq)

    def body(q_ref, *refs):
        k_refs, v_refs = refs[:nkv], refs[nkv:2 * nkv]
        o_ref = refs[2 * nkv]
        s_a, m_a, s_b, m_b = refs[2 * nkv + 1:]

        def emit(write, read):
            sets = [ScoreSet(q_ref[...], slice(None), None if write is None else write[0], read[0])]
            (o, l), = _attention_step(sets, k_refs, v_refs, ATTN_CHUNK)
            o_ref[...] = (o / l).astype(o_ref.dtype)

        _skewed_steps(nq, ([(s_a, m_a)], [(s_b, m_b)]), emit)

    block_bytes = (_nbytes((tq, MLA_HEAD_PAD), BF16) + _nbytes((sk, MLA_HEAD_PAD), BF16)
                   + _nbytes((sk, V_DIM), BF16) + _nbytes((tq, V_DIM), BF16))
    score_bytes = _nbytes((tq, sk), F32)
    return pl.pallas_call(
        body,
        grid=(b, heads, nq + 1),
        in_specs=([pl.BlockSpec((None, tq, MLA_HEAD_PAD), q_map)]
                  + _kv_specs(ks, MLA_HEAD_PAD) + _kv_specs(vs, V_DIM)),
        out_specs=pl.BlockSpec((None, tq, V_DIM), o_map),
        out_shape=jax.ShapeDtypeStruct((b, sq, heads * V_DIM), BF16),
        scratch_shapes=[pltpu.VMEM((tq, sk), F32), pltpu.VMEM((tq, LANES), F32)] * 2,
        compiler_params=_params(("parallel", "parallel", "arbitrary"), block_bytes, 4 * score_bytes),
        name="mla_attention",
    )(q, *ks, *vs)


def _diff_attention(q, ks, vs, lam_p, subln_g, lambda_init, heads, *, tq=ATTN_TQ):
    b, sq, _ = q.shape
    sk = sum(k.shape[1] for k in ks)
    tq = min(tq, sq)
    hw = 2 * DIFF_DIM
    nkv = len(ks)
    nq = sq // tq
    q_map, o_map = _skewed_maps(nq)

    def body(lam_ref, g_ref, q_ref, *refs):
        k_refs, v_refs = refs[:nkv], refs[nkv:2 * nkv]
        o_ref = refs[2 * nkv]
        bufs = refs[2 * nkv + 1:]
        set_a = [(bufs[0], bufs[1]), (bufs[2], bufs[3])]
        set_b = [(bufs[4], bufs[5]), (bufs[6], bufs[7])]

        def emit(write, read):
            halves = [(q_ref[:, :DIFF_DIM], slice(0, DIFF_DIM)), (q_ref[:, DIFF_DIM:], slice(DIFF_DIM, hw))]
            sets = [ScoreSet(qh, cols, None if write is None else write[n], read[n])
                    for n, (qh, cols) in enumerate(halves)]
            (o1, l1), (o2, l2) = _attention_step(sets, k_refs, v_refs, ATTN_CHUNK)
            lp = lam_ref[...]
            lam = (jnp.exp(jnp.sum(lp[0:1] * lp[1:2], axis=-1, keepdims=True))
                   - jnp.exp(jnp.sum(lp[2:3] * lp[3:4], axis=-1, keepdims=True)) + lambda_init)
            o = o1 / l1 - (lam / l2) * o2
            o_ref[...] = (_rms(o, g_ref[...]) * (1.0 - lambda_init)).astype(o_ref.dtype)

        _skewed_steps(nq, (set_a, set_b), emit)

    block_bytes = 2 * _nbytes((tq, hw), BF16) + 2 * _nbytes((sk, hw), BF16)
    score_bytes = _nbytes((tq, sk), F32)
    return pl.pallas_call(
        body,
        grid=(b, heads, nq + 1),
        in_specs=([pl.BlockSpec((4, DIFF_DIM), lambda bb, h, i: (0, 0)),
                   pl.BlockSpec((1, hw), lambda bb, h, i: (0, 0)),
                   pl.BlockSpec((None, tq, hw), q_map)]
                  + _kv_specs(ks, hw) + _kv_specs(vs, hw)),
        out_specs=pl.BlockSpec((None, tq, hw), o_map),
        out_shape=jax.ShapeDtypeStruct((b, sq, heads * hw), BF16),
        scratch_shapes=[pltpu.VMEM((tq, sk), F32), pltpu.VMEM((tq, LANES), F32)] * 4,
        compiler_params=_params(("parallel", "parallel", "arbitrary"), block_bytes, 6 * score_bytes),
        name="diff_attention",
    )(lam_p.astype(F32), subln_g.reshape(1, hw).astype(F32), q, *ks, *vs)


def _hyena_filters(seq, w1, b1, w2, b2, w3, b3, freq, w_out, *, tl=256):
    hid = w1.shape[1]
    width2 = w_out.shape[1]
    width = width2 // 2
    bands = (HY_EMB - 1) // 2
    t = jnp.linspace(0.0, 1.0, seq, dtype=F32)[:, None]
    ang = ((2.0 * math.pi / seq) * jnp.arange(seq, dtype=F32)[:, None]
           * jnp.linspace(1e-4, bands - 1, bands, dtype=F32)[None, :])
    z = jnp.concatenate([t, jnp.cos(ang), -jnp.sin(ang), jnp.zeros((seq, LANES - HY_EMB), F32)], axis=-1)
    z = jnp.concatenate([z[0::2], z[1::2]], axis=0)
    deltas =jnp.abs(jnp.linspace(math.log(HY_TARGET) / HY_LONG_DECAY_PCT,
                                  math.log(HY_TARGET) / HY_SHORT_DECAY_PCT, width, dtype=F32))
    deltas2 = jnp.concatenate([deltas, deltas])[None, :]

    def pad2(w):
        return jnp.pad(w.astype(F32), ((0, LANES - w.shape[0]), (0, LANES - w.shape[1])))

    def pad_row(v):
        return jnp.pad(v.astype(F32), (0, LANES - v.shape[0]))[None, :]

    w_out_p = jnp.pad(w_out.astype(F32), ((0, LANES - hid), (0, 0)))
    tl, tn = min(tl, seq), width2
    exact = lax.Precision.HIGHEST

    def body(z_ref, w1_ref, b1_ref, w2_ref, b2_ref, w3_ref, b3_ref, f_ref, wo_ref, d_ref, o_ref, mid_ref):
        zz, f = z_ref[...], f_ref[...]
        a = jnp.sin(f * (jnp.dot(zz, w1_ref[...], precision=exact, preferred_element_type=F32) + b1_ref[...]))
        a = jnp.sin(f * (jnp.dot(a, w2_ref[...], precision=exact, preferred_element_type=F32) + b2_ref[...]))
        a = jnp.sin(f * (jnp.dot(a, w3_ref[...], precision=exact, preferred_element_type=F32) + b3_ref[...]))
        h = jnp.dot(a, wo_ref[...], precision=exact, preferred_element_type=F32)
        h = h * jnp.exp(-zz[:, 0:1] * d_ref[...])
        row = pl.program_id(0) * tl + lax.broadcasted_iota(jnp.int32, (tl, width), 0)
        h_fwd = h[:, :width]
        h_bwd = jnp.where(row == 0, 0.0, h[:, width:])
        h_sum = h_fwd + h_bwd
        o_ref[:, :width] = h_sum.astype(o_ref.dtype)
        o_ref[:, width:] = (h_fwd - h_bwd).astype(o_ref.dtype)
        signed = jnp.where(row >= seq // 2, 0.0, jnp.where((row & 1) == 0, h_sum, -h_sum))
        alternating = jnp.sum(signed, axis=0, keepdims=True)

        @pl.when(pl.program_id(0) == 0)
        def _():
            mid_ref[...] = alternating

        @pl.when(pl.program_id(0) > 0)
        def _():
            mid_ref[...] += alternating

    sq = pl.BlockSpec((LANES, LANES), lambda i: (0, 0))
    vec = pl.BlockSpec((1, LANES), lambda i: (0, 0))
    block_bytes = _nbytes((tl, LANES), F32) + _nbytes((LANES, tn), F32) + 2 * _nbytes((tl, tn), F32)
    return pl.pallas_call(
        body,
        grid=(seq // tl,),
        in_specs=[pl.BlockSpec((tl, LANES), lambda i: (i, 0)), sq, vec, sq, vec, sq, vec, vec,
                  pl.BlockSpec((LANES, tn), lambda i: (0, 0)),
                  pl.BlockSpec((1, tn), lambda i: (0, 0))],
        out_specs=[pl.BlockSpec((tl, tn), lambda i: (i, 0)), pl.BlockSpec((1, width), lambda i: (0, 0))],
        out_shape=[jax.ShapeDtypeStruct((seq, width2), BF16), jax.ShapeDtypeStruct((1, width), F32)],
        compiler_params=_params(("arbitrary",), block_bytes, 4 * _nbytes((tl, tn), F32)),
        name="hyena_filters",
    )(z, pad2(w1), pad_row(b1), pad2(w2), pad_row(b2), pad2(w3), pad_row(b3), pad_row(freq), w_out_p, deltas2)


def _hyena_prep(zh, conv_w, conv_b, *, tc=128):
    b, seq, w3 = zh.shape
    width = w3 // 3
    half = seq // 2
    tc = min(tc, width)
    nw = width // tc

    def body(z0_ref, z1_ref, z2_ref, w0_ref, w1_ref, w2_ref, b0_ref, b1_ref, b2_ref, x0_ref, vx_ref):
        row = lax.broadcasted_iota(jnp.int32, (half, 1), 0)

        def conv(z_ref, w_ref, b_ref):
            ev = z_ref[pl.ds(0, half, stride=2), :]
            od = z_ref[pl.ds(1, half, stride=2), :]
            w, bias = w_ref[...], b_ref[...]
            before_even = jnp.where(row == 0, 0.0, pltpu.roll(od, 1, 0))
            after_odd = jnp.where(row == half - 1, 0.0, pltpu.roll(ev, half - 1, 0))
            return (before_even * w[0:1] + ev * w[1:2] + od * w[2:3] + bias,
                    ev * w[0:1] + od * w[1:2] + after_odd * w[2:3] + bias)

        x0 = conv(z0_ref, w0_ref, b0_ref)
        x1 = conv(z1_ref, w1_ref, b1_ref)
        v = conv(z2_ref, w2_ref, b2_ref)
        for parity in range(2):
            x0_ref[parity] = x0[parity]
            vx_ref[parity] = (v[parity] * x1[parity]).astype(vx_ref.dtype)

    def zspec(part):
        return pl.BlockSpec((None, seq, tc), lambda bb, j: (bb, 0, part * nw + j))

    def wspec(rows, part):
        return pl.BlockSpec((rows, tc), lambda bb, j: (0, part * nw + j))

    out_spec = pl.BlockSpec((2, half, tc), lambda bb, j: (0, 0, bb * nw + j))
    block_bytes = 4 * _nbytes((seq, tc), F32) + _nbytes((seq, tc), BF16)
    return pl.pallas_call(
        body,
        grid=(b, nw),
        in_specs=[zspec(0), zspec(1), zspec(2),
                  wspec(3, 0), wspec(3, 1), wspec(3, 2), wspec(1, 0), wspec(1, 1), wspec(1, 2)],
        out_specs=[out_spec, out_spec],
        out_shape=[jax.ShapeDtypeStruct((2, half, b * width), F32),
                   jax.ShapeDtypeStruct((2, half, b * width), BF16)],
        compiler_params=_params(("parallel", "parallel"), block_bytes, 8 * _nbytes((seq, tc), F32)),
        name="hyena_short_conv",
    )(zh, zh, zh, conv_w, conv_w, conv_w, conv_b[None, :], conv_b[None, :], conv_b[None, :])


def _dft_matrices(seq):
    n = 2 * seq
    half = seq // 2
    assert seq % 2 == 0 and n % 4 == 0
    idx = jnp.arange(half, dtype=jnp.int32)
    alt = jnp.where((idx & 1) == 0, 1.0, -1.0).astype(F32)

    def cos_sin(k, s):
        phase = _mod_nonneg(k * s, n)
        return _cos_turns(phase, n), _cos_turns(_mod_nonneg(phase + (n - n // 4), n), n)

    def forward(c, s):
        return jnp.concatenate([c, jnp.where(idx[:, None] == 0, alt[None, :], s)], axis=0).astype(BF16)

    def inverse(c, s):
        return jnp.concatenate([c, jnp.where(idx[None, :] == 0, alt[:, None], s)], axis=1).astype(BF16)

    kb = math.gcd(half, 64)
    k_hi, k_lo = jnp.arange(half // kb, dtype=jnp.int32) * kb, jnp.arange(kb, dtype=jnp.int32)

    def by_rows(sample):
        (ch, sh), (cl, sl) = cos_sin(k_hi[:, None], sample[None, :]), cos_sin(k_lo[:, None], sample[None, :])
        return ((ch[:, None, :] * cl[None] - sh[:, None, :] * sl[None]).reshape(half, half),
                (sh[:, None, :] * cl[None] + ch[:, None, :] * sl[None]).reshape(half, half))

    def by_cols(sample):
        (ch, sh), (cl, sl) = cos_sin(k_hi[None, :], sample[:, None]), cos_sin(k_lo[None, :], sample[:, None])
        return ((ch[:, :, None] * cl[:, None, :] - sh[:, :, None] * sl[:, None, :]).reshape(half, half),
                (sh[:, :, None] * cl[:, None, :] + ch[:, :, None] * sl[:, None, :]).reshape(half, half))

    even = by_rows(2 * idx)
    return forward(*even), forward(*by_rows(2 * idx + 1)), inverse(*even), inverse(*by_cols(2 * idx + 1))


def _div_nonneg(x, d):
    return x >> (d.bit_length() - 1) if d & (d - 1) == 0 else x // d


def _mod_nonneg(x, d):
    return x & (d - 1) if d & (d - 1) == 0 else x % d


def _cos_turns(phase, n):
    quarter = n // 4
    quad = _div_nonneg(phase, quarter)
    rem = phase - quad * quarter
    odd = (quad & 1) == 1
    x = jnp.where(odd, quarter - rem, rem).astype(F32) * (2.0 * math.pi / n)
    x2 = x * x
    acc = jnp.full_like(x2, 1.0 / math.factorial(16))
    for order in range(14, -1, -2):
        acc = acc * (-x2) + 1.0 / math.factorial(order)
    return jnp.where((quad == 1) | (quad == 2), -acc, acc)


def _spectrum_multiply(eu, ou, eg, og, mid_g, batch, width, *, tr=256, tc=512):
    seq = eu.shape[0]
    half = seq // 2
    n = 2 * seq
    tr, tc = min(tr, half), min(tc, width)
    nwc = width // tc

    def body(eu_ref, ou_ref, eg_ref, og_ref, mid_ref, pe_ref, po_ref):
        row = pl.program_id(0) * tr + lax.broadcasted_iota(jnp.int32, (tr, tc), 0)
        first = row == 0

        def pair(e_ref, o_ref):
            ec, es, oc, os_ = e_ref[0], e_ref[1], o_ref[0], o_ref[1]
            return (ec + oc, jnp.where(first, 0.0, es + os_)), (ec - oc, jnp.where(first, 0.0, os_ - es))

        def product(u, g):
            return u[0] * g[0] - u[1] * g[1], u[0] * g[1] + u[1] * g[0]

        (u_lo, u_hi), (g_lo, g_hi) = pair(eu_ref, ou_ref), pair(eg_ref, og_ref)
        ya_lo, yb_lo = product(u_lo, g_lo)
        ya_hi, yb_hi = product(u_hi, g_hi)
        ya_mid, yb_mid = product((eu_ref[1], ou_ref[1]), (mid_ref[...], og_ref[1]))
        w_cos = jnp.where(first, 1.0 / n, 2.0 / n)
        w_sin = 2.0 / n
        pe_ref[0] = (w_cos * (ya_lo + ya_hi)).astype(pe_ref.dtype)
        po_ref[0] = (w_cos * (ya_lo - ya_hi)).astype(po_ref.dtype)
        pe_ref[1] = (w_sin * jnp.where(first, ya_mid, yb_lo - yb_hi)).astype(pe_ref.dtype)
        po_ref[1] = (w_sin * jnp.where(first, yb_mid, yb_lo + yb_hi)).astype(po_ref.dtype)

    u_spec = pl.BlockSpec((2, tr, tc), lambda i, j, bb: (0, i, bb * nwc + j))
    g_spec = pl.BlockSpec((2, tr, tc), lambda i, j, bb: (0, i, j))
    block_bytes = 4 * _nbytes((2, tr, tc), F32) + 2 * _nbytes((2, tr, tc), BF16)
    out_sds = jax.ShapeDtypeStruct((2, half, batch * width), BF16)
    pe, po = pl.pallas_call(
        body,
        grid=(half // tr, nwc, batch),
        in_specs=[u_spec, u_spec, g_spec, g_spec, pl.BlockSpec((1, tc), lambda i, j, bb: (0, j))],
        out_specs=[u_spec, u_spec],
        out_shape=[out_sds, out_sds],
        compiler_params=_params(("parallel", "parallel", "arbitrary"), block_bytes, 24 * _nbytes((tr, tc), F32)),
        name="hyena_spectrum_multiply",
    )(eu.reshape(2, half, batch * width), ou.reshape(2, half, batch * width),
      eg.reshape(2, half, width), og.reshape(2, half, width), mid_g)
    return pe.reshape(seq, batch * width), po.reshape(seq, batch * width)


def _hyena_branch(zh, conv_w, conv_b, filt, hy_bias):
    b, seq, w3 = zh.shape
    width = w3 // 3
    half = seq // 2
    x0, vx = _hyena_prep(zh, conv_w.astype(F32), conv_b.astype(F32))
    h_sum_dif, mid_g = _hyena_filters(seq, *filt)
    h_sum_dif = h_sum_dif.reshape(2, half, 2 * width)
    mats = _dft_matrices(seq)
    forward_mats, inverse_mats = mats[:2], mats[2:]
    tm, tn = min(TILE_M, half), min(TILE_N, width)
    nw = width // tn

    def store(acc, ex, outs):
        outs[0][...] = acc

    def forward(mat, samples, parity, cols, col_shift, name):
        return _matmul(mat, samples, tm=tm, tn=tn, tk=TILE_K, epilogue=store, w_index=(parity,),
                       w_cols=(0, cols), w_col_shift=col_shift,
                       out_shape=[jax.ShapeDtypeStruct((seq, cols), F32)],
                       out_specs=[pl.BlockSpec((tm, tn), lambda i, j, k: (i, j))], name=name)[0]

    def filter_cols(i):
        return (i // (half // tm)) * nw

    data = [forward(m, vx, p, b * width, None, "hyena_dft_forward") for p, m in enumerate(forward_mats)]
    filt_spec = [forward(m, h_sum_dif, p, width, filter_cols, "hyena_dft_filters")
                 for p, m in enumerate(forward_mats)]
    spectra = _spectrum_multiply(data[0], data[1], filt_spec[0], filt_spec[1], mid_g, b, width)

    return _hyena_inverse(inverse_mats, spectra, x0, vx, hy_bias.astype(F32)[None, :], b, width)


def _hyena_inverse(mats, spectra, x0, vx, bias, batch, width):
    half, seq = mats[0].shape
    tm, tn = min(TILE_M // 2, half), min(TILE_N, width)
    nw = width // tn

    def body(me_ref, mo_ref, pe_ref, po_ref, x0_ref, vx_ref, bias_ref, o_ref, rows_ref):
        for parity, (m_ref, p_ref) in enumerate(((me_ref, pe_ref), (mo_ref, po_ref))):
            conv = jnp.dot(m_ref[...], p_ref[...], preferred_element_type=F32)
            gated = x0_ref[parity] * (conv + bias_ref[...] * vx_ref[parity].astype(F32))
            for c in range(tn // LANES):
                rows_ref[c, pl.ds(parity, tm, stride=2), :] = gated[:, c * LANES:(c + 1) * LANES]
        for c in range(tn // LANES):
            o_ref[:, c * LANES:(c + 1) * LANES] = rows_ref[c].astype(o_ref.dtype)

    mat_spec = pl.BlockSpec((tm, seq), lambda i, j: (i, 0))
    spec_spec = pl.BlockSpec((seq, tn), lambda i, j: (0, j))
    split_spec = pl.BlockSpec((2, tm, tn), lambda i, j: (0, i, j))
    block_bytes = (2 * _nbytes((tm, seq), BF16) + 2 * _nbytes((seq, tn), BF16) + _nbytes((2, tm, tn), F32)
                   + 2 * _nbytes((2, tm, tn), BF16))
    return pl.pallas_call(
        body,
        grid=(half // tm, batch * width // tn),
        in_specs=[mat_spec, mat_spec, spec_spec, spec_spec, split_spec, split_spec,
                  pl.BlockSpec((1, tn), lambda i, j: (0, j % nw))],
        out_specs=pl.BlockSpec((None, 2 * tm, tn), lambda i, j: (j // nw, i, j % nw)),
        out_shape=jax.ShapeDtypeStruct((batch, 2 * half, width), BF16),
        scratch_shapes=[pltpu.VMEM((tn // LANES, 2 * tm, LANES), F32)],
        compiler_params=_params(("parallel", "parallel"), block_bytes, 8 * _nbytes((2 * tm, tn), F32)),
        name="hyena_dft_inverse",
    )(mats[0], mats[1], spectra[0], spectra[1], x0, vx, bias)


def _ab_latents(h, w1, q_norm_g, kv_norm_g, tables, seq, q_lora, kv_lora, *, tm=512, tk=1024):
    m = h.shape[0]
    n1 = w1.shape[1]
    tm = min(tm, m, seq)
    tiles_per_seq = seq // tm

    def epilogue(acc, ex, outs):
        outs[0][...] = _rms(acc[:, :q_lora], ex[0][...]).astype(BF16)
        outs[1][...] = _rms(acc[:, q_lora:q_lora + kv_lora], ex[1][...]).astype(BF16)
        kr = acc[:, q_lora + kv_lora:]
        if tables is not None:
            kr = _rope_apply(kr, ex[2], ex[3], MLA_ROPE_LAYOUT)
        outs[2][...] = kr

    extras = [(q_norm_g.astype(F32)[None, :], pl.BlockSpec((1, q_lora), lambda i, j, k: (0, 0))),
              (kv_norm_g.astype(F32)[None, :], pl.BlockSpec((1, kv_lora), lambda i, j, k: (0, 0)))]
    if tables is not None:
        spec = pl.BlockSpec((tm, LANES), lambda i, j, k: (i % tiles_per_seq, 0))
        extras += [(tables[0], spec), (tables[1], spec)]
    widths = (q_lora, kv_lora, MLA_HEAD_PAD)
    dtypes = (BF16, BF16, F32)
    return _matmul(h, w1, tm=tm, tn=n1, tk=tk, epilogue=epilogue, extras=extras,
                   out_shape=[jax.ShapeDtypeStruct((m, wd), dt) for wd, dt in zip(widths, dtypes)],
                   out_specs=[pl.BlockSpec((tm, wd), lambda i, j, k: (i, 0)) for wd in widths],
                   name="mla_latent_projection")


def _mla_keys(ckv, w_uk, kr, heads, *, tm=1024, tn=1024):
    m = ckv.shape[0]
    n = heads * MLA_HEAD_PAD
    tm, tn = min(tm, m), min(tn, n)
    reps = tn // MLA_HEAD_PAD

    def epilogue(acc, ex, outs):
        rot = ex[0][...]
        if reps > 1:
            rot = jnp.concatenate([rot] * reps, axis=1)
        outs[0][...] = (acc + rot).astype(BF16)

    extras = [(kr, pl.BlockSpec((tm, MLA_HEAD_PAD), lambda i, j, k: (i, 0)))]
    return _matmul(ckv, w_uk, tm=tm, tn=tn, tk=ckv.shape[1], epilogue=epilogue, extras=extras,
                   out_shape=[jax.ShapeDtypeStruct((m, n), BF16)],
                   out_specs=[pl.BlockSpec((tm, tn), lambda i, j, k: (i, j))], name="mla_key_up")[0]


def _ab_mixer(h, hc, batch, seq, seq_c, need_ctx, w_in, q_norm_g, kv_norm_g, w_uq, w_ukv,
              conv_w, conv_b, filt, hy_bias):
    d = w_in.shape[0]
    heads = d // (2 * V_DIM)
    q_lora, kv_lora = w_uq.shape[0], w_ukv.shape[0]
    kv_end = q_lora + kv_lora + ROPE_DIM
    mla_scale = (NOPE_DIM + ROPE_DIM) ** -0.5
    zero_pad = MLA_HEAD_PAD - NOPE_DIM - ROPE_DIM

    w1 = jnp.concatenate([w_in[:, :q_lora + kv_lora], jnp.zeros((d, NOPE_DIM), w_in.dtype),
                          w_in[:, q_lora + kv_lora:kv_end], jnp.zeros((d, zero_pad), w_in.dtype)],
                         axis=1).astype(BF16)
    w_hy = w_in[:, kv_end:].astype(BF16)
    uq = w_uq.reshape(q_lora, heads, NOPE_DIM + ROPE_DIM)
    w_uq_p = jnp.pad(uq, ((0, 0), (0, 0), (0, zero_pad))).reshape(q_lora, heads * MLA_HEAD_PAD).astype(BF16)
    ukv = w_ukv.reshape(kv_lora, heads, NOPE_DIM + V_DIM)
    w_uk_p = jnp.pad(ukv[..., :NOPE_DIM], ((0, 0), (0, 0), (0, MLA_HEAD_PAD - NOPE_DIM))
                     ).reshape(kv_lora, heads * MLA_HEAD_PAD).astype(BF16)
    w_uv = ukv[..., NOPE_DIM:].reshape(kv_lora, heads * V_DIM).astype(BF16)

    tables = _rope_tables(seq, MLA_ROPE_LAYOUT)
    rope_kw = dict(layout=MLA_ROPE_LAYOUT)

    cq, ckv, kr = _ab_latents(h, w1, q_norm_g, kv_norm_g, tables, seq, q_lora, kv_lora)
    cqc, ckvc, krc = _ab_latents(hc, w1, q_norm_g, kv_norm_g, None, seq_c, q_lora, kv_lora)
    q = _mm_rope(cq, w_uq_p, tables, mla_scale, seq, BF16, name="mla_query_up", **rope_kw)
    k = _mla_keys(ckv, w_uk_p, kr, heads)
    v = _mm_plain(ckv, w_uv, BF16, name="mla_value_up")
    kc = _mla_keys(ckvc, w_uk_p, krc, heads)
    vc = _mm_plain(ckvc, w_uv, BF16, name="mla_value_up")

    def split(t, s):
        return t.reshape(batch, s, t.shape[-1])

    att = _mla_attention(split(q, seq), [split(kc, seq_c), split(k, seq)], [split(vc, seq_c), split(v, seq)], heads)
    zh = _mm_plain(h, w_hy, F32, tn=512, name="hyena_in_projection")
    hy = _hyena_branch(split(zh, seq), conv_w, conv_b, filt, hy_bias)
    def flat(t):
        return t.reshape(t.shape[0] * t.shape[1], t.shape[2])

    mix = [flat(att), flat(hy)]
    if not need_ctx:
        return mix, None
    qc = _mm_rope(cqc, w_uq_p, None, mla_scale, seq_c, BF16, name="mla_query_up", **rope_kw)
    attc = _mla_attention(split(qc, seq_c), [split(kc, seq_c)], [split(vc, seq_c)], heads)
    zhc = _mm_plain(hc, w_hy, F32, tn=512, name="hyena_in_projection")
    hyc = _hyena_branch(split(zhc, seq_c), conv_w, conv_b, filt, hy_bias)
    return mix, [flat(attc), flat(hyc)]


def _diff_mixer(h, hc, batch, seq, seq_c, need_ctx, w_in, lam_p, subln_g, lambda_init):
    d = w_in.shape[0]
    hd = w_in.shape[1] // 3
    heads = hd // (2 * DIFF_DIM)
    scale = DIFF_DIM ** -0.5
    w = _cast_deinterleave_columns(w_in, 2 * hd)
    q_cols, k_cols, v_cols = (0, hd), (hd, hd), (2 * hd, hd)
    tables = _rope_tables(seq, DIFF_ROPE_LAYOUT)
    rope_kw = dict(layout=DIFF_ROPE_LAYOUT)

    q = _mm_rope(h, w, tables, scale, seq, BF16, w_cols=q_cols, name="diff_query_projection", **rope_kw)
    k = _mm_rope(h, w, tables, 1.0, seq, BF16, w_cols=k_cols, name="diff_key_projection", **rope_kw)
    v = _mm_plain(h, w, BF16, w_cols=v_cols, name="diff_value_projection")
    kc = _mm_rope(hc, w, None, 1.0, seq_c, BF16, w_cols=k_cols, name="diff_key_projection", **rope_kw)
    vc = _mm_plain(hc, w, BF16, w_cols=v_cols, name="diff_value_projection")

    def split(t, s):
        return t.reshape(batch, s, t.shape[-1])

    o = _diff_attention(split(q, seq), [split(kc, seq_c), split(k, seq)], [split(vc, seq_c), split(v, seq)],
                        lam_p, subln_g, lambda_init, heads)
    mix = o.reshape(batch * seq, hd)
    if not need_ctx:
        return mix, None
    qc = _mm_rope(hc, w, None, scale, seq_c, BF16, w_cols=q_cols, name="diff_query_projection", **rope_kw)
    oc = _diff_attention(split(qc, seq_c), [split(kc, seq_c)], [split(vc, seq_c)], lam_p, subln_g, lambda_init, heads)
    return mix, oc.reshape(batch * seq_c, hd)


def kernel(x, c, ctx, c_ctx, mod_w, mod_b, norm_g, ffn_w_gate, ffn_w_up, ffn_w_down, ab_w_in, mla_q_norm_g, mla_kv_norm_g, mla_w_uq, mla_w_ukv, hy_conv_w, hy_conv_b, hy_w1, hy_b1, hy_w2, hy_b2, hy_w3, hy_b3, hy_freq, hy_w_out, hy_bias, ab_w_out, c_w_in, c_lambda, c_subln_g, c_w_out, final_norm_g):
    batch, seq, d = x.shape
    seq_c = ctx.shape[1]
    depth = mod_w.shape[0]
    rows_c = batch * seq_c
    xs = x.reshape(batch * seq, d).astype(F32)
    xc = ctx.reshape(rows_c, d).astype(F32)

    cvec = jnp.concatenate([c.astype(F32), c_ctx.astype(F32)[None, :]], axis=0)
    cvec = jnp.pad(cvec, ((0, -cvec.shape[0] % 8), (0, 0)))
    mods_all = _modulation(cvec, mod_w.astype(F32), mod_b.astype(F32))

    for layer in range(depth):
        need_ctx = layer < depth - 1
        mods = mods_all[layer, :batch].reshape(batch, 3 * N_SUB, d)
        modc = mods_all[layer, batch:batch + 1].reshape(1, 3 * N_SUB, d)
        g = norm_g[layer].astype(F32)

        def ffn(t, m, rows, sub, which):
            return _half_ffn(t, m, sub, rows, g[sub], ffn_w_gate, ffn_w_up, ffn_w_down, (layer, which))

        xs = ffn(xs, mods, seq, 0, 0)
        xc = ffn(xc, modc, rows_c, 0, 0)
        h = _norm_mod(xs, g[1], mods, 1, seq, BF16)
        hc = _norm_mod(xc, g[1], modc, 1, rows_c, BF16)
        i = layer // 2
        if layer % 2 == 0:
            filt = (hy_w1[i], hy_b1[i], hy_w2[i], hy_b2[i], hy_w3[i], hy_b3[i], hy_freq[i], hy_w_out[i])
            mix, mixc = _ab_mixer(h, hc, batch, seq, seq_c, need_ctx, ab_w_in[i], mla_q_norm_g[i],
                                  mla_kv_norm_g[i], mla_w_uq[i], mla_w_ukv[i], hy_conv_w[i], hy_conv_b[i],
                                  filt, hy_bias[i])
            w_out = ab_w_out[i].astype(BF16)
        else:
            lambda_init = 0.8 - 0.6 * math.exp(-0.3 * layer)
            mix, mixc = _diff_mixer(h, hc, batch, seq, seq_c, need_ctx, c_w_in[i], c_lambda[i], c_subln_g[i],
                                    lambda_init)
            w_out = c_w_out[i].astype(BF16)
        xs = _mm_residual(mix, w_out, xs, mods, 1, 1.0, seq, name="mixer_out_residual")
        xs = ffn(xs, mods, seq, 2, 1)
        if need_ctx:
            xc = _mm_residual(mixc, w_out, xc, modc, 1, 1.0, rows_c, name="mixer_out_residual")
            xc = ffn(xc, modc, rows_c, 2, 1)

    out = _norm_mod(xs, final_norm_g.astype(F32), None, 0, seq, x.dtype)
    return out.reshape(batch, seq, d)
```

```python
import functools
import math
from typing import NamedTuple

import numpy as np
import jax
import jax.numpy as jnp
from jax import lax
from jax.experimental import pallas as pl
from jax.experimental.pallas import tpu as pltpu

F32 = jnp.float32
BF16 = jnp.bfloat16

EPS = 1e-6
ROPE_THETA = 10000.0
GRID_W = 64
N_SUB = 3
NOPE_DIM = 128
ROPE_DIM = 64
V_DIM = 128
DIFF_DIM = 128
HY_EMB = 33
HY_TARGET = 1e-2
HY_SHORT_DECAY_PCT = 0.3
HY_LONG_DECAY_PCT = 1.5

LANES = 128
MXU_DIM_V7X = 256
VMEM_BYTES_V7X = 64 * 1024 * 1024
VMEM_COMPILER_RESERVE = 8 * 1024 * 1024
MLA_HEAD_PAD = MXU_DIM_V7X

TILE_M = 1024
TILE_N = 512
TILE_K = 4096
TILE_K_LONG = 2 * TILE_K
NORM_ROWS = 32
ATTN_TQ = 256
ATTN_CHUNK = 256


def _params(semantics, block_bytes, temp_bytes=0):
    want = 2 * block_bytes + temp_bytes
    limit = min(VMEM_BYTES_V7X - VMEM_COMPILER_RESERVE, max(want, 32 * 1024 * 1024))
    return pltpu.CompilerParams(dimension_semantics=semantics, vmem_limit_bytes=int(limit))


def _nbytes(shape, dtype):
    return math.prod(shape) * jnp.dtype(dtype).itemsize


def _rms(x, g):
    return x * lax.rsqrt(jnp.mean(x * x, axis=-1, keepdims=True) + EPS) * g


def _matmul(a, w, *, tm, tn, tk, epilogue, out_shape, out_specs, extras=(), w_index=(), w_cols=None,
            w_col_shift=None, resident_a=False, name):
    a_parts = list(a) if isinstance(a, (list, tuple)) else [a]
    m = a_parts[0].shape[0]
    kdim = sum(p.shape[1] for p in a_parts)
    assert w.ndim == 2 + len(w_index) and w.shape[-2] == kdim
    col0, n = (0, w.shape[-1]) if w_cols is None else w_cols
    tm, tn, tk = min(tm, m), min(tn, n), min(tk, kdim)
    assert m % tm == 0 and n % tn == 0 and kdim % tk == 0 and col0 % tn == 0, (name, a.shape, w.shape, tm, tn, tk)
    j0 = col0 // tn
    nk = kdim // tk
    n_ex, n_out = len(extras), len(out_shape)

    def body(*refs):
        a_refs, w_ref = refs[:n_a], refs[n_a]
        ex = refs[n_a + 1:n_a + 1 + n_ex]
        outs = refs[n_a + 1 + n_ex:n_a + 1 + n_ex + n_out]
        prod, k0 = None, 0
        for a_ref in a_refs:
            rows = a_ref.shape[1]
            part = jnp.dot(a_ref[...], w_ref[k0:k0 + rows, :].astype(BF16), preferred_element_type=F32)
            prod = part if prod is None else prod + part
            k0 += rows
        if nk == 1:
            epilogue(prod, ex, outs)
            return
        acc_ref = refs[-1]
        k = pl.program_id(2)

        @pl.when(k == 0)
        def _():
            acc_ref[...] = prod

        if nk > 2:
            @pl.when((k > 0) & (k < nk - 1))
            def _():
                acc_ref[...] += prod

        @pl.when(k == nk - 1)
        def _():
            epilogue(acc_ref[...] + prod, ex, outs)

    n_a = len(a_parts)
    assert n_a == 1 or nk == 1
    a_mode = pl.Buffered(1) if resident_a else None
    in_specs = [pl.BlockSpec((tm, tk if n_a == 1 else p.shape[1]), lambda i, j, k: (i, k), pipeline_mode=a_mode)
                for p in a_parts]
    in_specs += [pl.BlockSpec((None,) * len(w_index) + (tk, tn),
                             lambda i, j, k: (*w_index, k, j + j0 + (0 if w_col_shift is None else w_col_shift(i))))]
    in_specs += [spec for _, spec in extras]
    block_bytes = _nbytes((tm, tk), a_parts[0].dtype) + _nbytes((tk, tn), w.dtype)
    for arr, spec in extras:
        block_bytes += _nbytes([d for d in spec.block_shape if d is not None], arr.dtype)
    for sds, spec in zip(out_shape, out_specs):
        block_bytes += _nbytes([d for d in spec.block_shape if d is not None], sds.dtype)
    acc_bytes = _nbytes((tm, tn), F32)
    return pl.pallas_call(
        body,
        grid=(m // tm, n // tn, nk),
        in_specs=in_specs,
        out_specs=list(out_specs),
        out_shape=list(out_shape),
        scratch_shapes=[pltpu.VMEM((tm, tn), F32)] if nk > 1 else [],
        compiler_params=_params(("parallel", "parallel", "arbitrary"), block_bytes, 8 * acc_bytes),
        name=name,
    )(*a_parts, w, *[arr for arr, _ in extras])


def _mm_plain(a, w, out_dtype, *, tm=TILE_M, tn=TILE_N, tk=TILE_K, w_cols=None, name):
    m, n = a.shape[0], (w.shape[1] if w_cols is None else w_cols[1])
    tm, tn = min(tm, m), min(tn, n)

    def epilogue(acc, ex, outs):
        outs[0][...] = acc.astype(out_dtype)

    return _matmul(a, w, tm=tm, tn=tn, tk=tk, epilogue=epilogue, w_cols=w_cols,
                   out_shape=[jax.ShapeDtypeStruct((m, n), out_dtype)],
                   out_specs=[pl.BlockSpec((tm, tn), lambda i, j, k: (i, j))], name=name)[0]


def _mm_residual(a, w, res, mods, sub, coef, rows_per_group, *, tm=TILE_M, tn=TILE_N, tk=TILE_K, w_index=(),
                 resident_a=False, name):
    m, n = (a[0] if isinstance(a, (list, tuple)) else a).shape[0], w.shape[-1]
    tm, tn = min(tm, m, rows_per_group), min(tn, n)
    tiles_per_group = rows_per_group // tm
    gate_row = 3 * sub + 2

    def epilogue(acc, ex, outs):
        res_ref, mod_ref = ex
        gate = mod_ref[gate_row:gate_row + 1, :]
        outs[0][...] = res_ref[...] + (coef * gate) * acc

    extras = [(res, pl.BlockSpec((tm, tn), lambda i, j, k: (i, j))),
              (mods, pl.BlockSpec((None, 3 * N_SUB, tn), lambda i, j, k: (i // tiles_per_group, 0, j)))]
    return _matmul(a, w, tm=tm, tn=tn, tk=tk, epilogue=epilogue, extras=extras, w_index=w_index,
                   resident_a=resident_a, out_shape=[jax.ShapeDtypeStruct((m, n), F32)],
                   out_specs=[pl.BlockSpec((tm, tn), lambda i, j, k: (i, j))], name=name)[0]


class RopeLayout(NamedTuple):
    group: int
    lo: int
    n_freq: int
    split: bool


MLA_ROPE_LAYOUT = RopeLayout(group=MLA_HEAD_PAD, lo=NOPE_DIM, n_freq=ROPE_DIM // 4, split=False)
DIFF_ROPE_LAYOUT = RopeLayout(group=DIFF_DIM, lo=0, n_freq=DIFF_DIM // 4, split=True)


def _rope_tables(seq, layout):
    n_freq = layout.n_freq
    pos = jnp.arange(seq, dtype=jnp.int32)
    row_pos = (pos // GRID_W).astype(F32)
    col_pos = (pos % GRID_W).astype(F32)
    inv = ROPE_THETA ** (-jnp.arange(n_freq, dtype=F32) / n_freq)
    ang = jnp.concatenate([row_pos[:, None] * inv, col_pos[:, None] * inv], axis=-1)
    cos, sin = jnp.cos(ang), jnp.sin(ang)
    if layout.split:
        assert 4 * n_freq == LANES
        return jnp.concatenate([cos, cos], axis=-1), jnp.concatenate([-sin, sin], axis=-1)
    cos = jnp.repeat(cos, 2, axis=-1)
    sin = jnp.repeat(sin, 2, axis=-1) * jnp.tile(jnp.array([-1.0, 1.0], F32), 2 * n_freq)
    rest = LANES - 4 * n_freq
    return (jnp.concatenate([cos, jnp.ones((seq, rest), F32)], axis=-1),
            jnp.concatenate([sin, jnp.zeros((seq, rest), F32)], axis=-1))


def _rope_apply(x, cos_ref, sin_ref, layout):
    assert layout.lo % LANES == 0 and layout.group % LANES == 0 and x.shape[-1] % layout.group == 0
    cos, sin = cos_ref[...], sin_ref[...]
    slabs = []
    for c0 in range(0, x.shape[-1], LANES):
        slab = x[:, c0:c0 + LANES]
        if c0 % layout.group == layout.lo:
            if layout.split:
                partner = pltpu.roll(slab, LANES // 2, 1)
            else:
                even = (lax.broadcasted_iota(jnp.int32, slab.shape, 1) & 1) == 0
                partner = jnp.where(even, pltpu.roll(slab, LANES - 1, 1), pltpu.roll(slab, 1, 1))
            slab = slab * cos + partner * sin
        slabs.append(slab)
    return jnp.concatenate(slabs, axis=1) if len(slabs) > 1 else slabs[0]


def _cast_deinterleave_columns(w, n_perm):
    kdim, n = w.shape
    perm = np.concatenate([np.arange(0, LANES, 2), np.arange(1, LANES, 2)])
    p = np.zeros((LANES, LANES), np.float32)
    p[perm, np.arange(LANES)] = 1.0
    tm, tn = min(TILE_M, kdim), min(TILE_N, n)
    assert n_perm % tn == 0
    perm_blocks = n_perm // tn

    def body(w_ref, p_ref, o_ref):
        @pl.when(pl.program_id(1) < perm_blocks)
        def _():
            pm = p_ref[...]
            for c0 in range(0, tn, LANES):
                o_ref[:, c0:c0 + LANES] = jnp.dot(w_ref[:, c0:c0 + LANES].astype(BF16), pm,
                                                  preferred_element_type=F32).astype(BF16)

        @pl.when(pl.program_id(1) >= perm_blocks)
        def _():
            o_ref[...] = w_ref[...].astype(BF16)

    return pl.pallas_call(
        body,
        grid=(kdim // tm, n // tn),
        in_specs=[pl.BlockSpec((tm, tn), lambda i, j: (i, j)), pl.BlockSpec((LANES, LANES), lambda i, j: (0, 0))],
        out_specs=pl.BlockSpec((tm, tn), lambda i, j: (i, j)),
        out_shape=jax.ShapeDtypeStruct((kdim, n), BF16),
        compiler_params=_params(("parallel", "parallel"), _nbytes((tm, tn), w.dtype) + _nbytes((tm, tn), BF16)),
        name="cast_deinterleave_rotary_columns",
    )(w, jnp.asarray(p, BF16))


def _mm_rope(a, w, tables, scale, seq, out_dtype, *, layout, tm=TILE_M, tn=TILE_N, tk=TILE_K, w_cols=None, name):
    m, n = a.shape[0], (w.shape[1] if w_cols is None else w_cols[1])
    tm, tn = min(tm, m, seq), min(tn, n)
    tiles_per_seq = seq // tm

    def epilogue(acc, ex, outs):
        y = acc if tables is None else _rope_apply(acc, ex[0], ex[1], layout)
        outs[0][...] = (y * scale).astype(out_dtype)

    extras = []
    if tables is not None:
        spec = pl.BlockSpec((tm, LANES), lambda i, j, k: (i % tiles_per_seq, 0))
        extras = [(tables[0], spec), (tables[1], spec)]
    return _matmul(a, w, tm=tm, tn=tn, tk=tk, epilogue=epilogue, extras=extras, w_cols=w_cols,
                   out_shape=[jax.ShapeDtypeStruct((m, n), out_dtype)],
                   out_specs=[pl.BlockSpec((tm, tn), lambda i, j, k: (i, j))], name=name)[0]


def _modulation(cvec, mod_w, mod_b, *, tn=512):
    depth, d, n = mod_w.shape
    rows = cvec.shape[0]
    tn = min(tn, n)

    def body(c_ref, w_ref, b_ref, o_ref):
        cv = c_ref[...]
        s = cv * (1.0 / (1.0 + jnp.exp(-cv)))
        o_ref[...] = jnp.dot(s, w_ref[...], preferred_element_type=F32) + b_ref[...]

    block_bytes = _nbytes((d, tn), F32) + _nbytes((rows, d), F32) + 2 * _nbytes((rows, tn), F32)
    return pl.pallas_call(
        body,
        grid=(depth, n // tn),
        in_specs=[pl.BlockSpec((rows, d), lambda l, j: (0, 0)),
                  pl.BlockSpec((None, d, tn), lambda l, j: (l, 0, j)),
                  pl.BlockSpec((None, 1, tn), lambda l, j: (l, 0, j))],
        out_specs=pl.BlockSpec((None, rows, tn), lambda l, j: (l, 0, j)),
        out_shape=jax.ShapeDtypeStruct((depth, rows, n), F32),
        compiler_params=_params(("parallel", "parallel"), block_bytes),
        name="adaln_modulation",
    )(cvec, mod_w, mod_b.reshape(depth, 1, n))


def _norm_mod(x, g, mods, sub, rows_per_group, out_dtype, *, tm=512):
    m, d = x.shape
    tm = min(tm, m, rows_per_group)
    tiles_per_group = rows_per_group // tm

    def body(*refs):
        if mods is None:
            x_ref, g_ref, o_ref = refs
        else:
            x_ref, g_ref, mod_ref, o_ref = refs
        gain = g_ref[...]
        if mods is not None:
            gain = gain * (1.0 + mod_ref[3 * sub + 1:3 * sub + 2, :])
            shift = mod_ref[3 * sub:3 * sub + 1, :]
        rows = min(tm, NORM_ROWS)
        for r0 in range(0, tm, rows):
            xx = x_ref[r0:r0 + rows, :]
            y = xx * lax.rsqrt(jnp.mean(xx * xx, axis=-1, keepdims=True) + EPS) * gain
            if mods is not None:
                y = y + shift
            o_ref[r0:r0 + rows, :] = y.astype(out_dtype)

    in_specs = [pl.BlockSpec((tm, d), lambda i: (i, 0)), pl.BlockSpec((1, d), lambda i: (0, 0))]
    args = [x, g.reshape(1, d)]
    if mods is not None:
        in_specs.append(pl.BlockSpec((None, 3 * N_SUB, d), lambda i: (i // tiles_per_group, 0, 0)))
        args.append(mods)
    block_bytes = _nbytes((tm, d), F32) + _nbytes((tm, d), out_dtype) + _nbytes((16, d), F32)
    return pl.pallas_call(
        body,
        grid=(m // tm,),
        in_specs=in_specs,
        out_specs=pl.BlockSpec((tm, d), lambda i: (i, 0)),
        out_shape=jax.ShapeDtypeStruct((m, d), out_dtype),
        compiler_params=_params(("parallel",), block_bytes, 2 * _nbytes((tm, d), F32)),
        name="rmsnorm_modulate",
    )(*args)


def _gate_up(h, w_gate, w_up, w_index, *, tm=2 * TILE_M, tn=TILE_N, tk=TILE_K):
    m, kdim = h.shape
    n = w_gate.shape[-1]
    tn = tn * jnp.dtype(BF16).itemsize // jnp.dtype(w_gate.dtype).itemsize
    tm, tn, tk = min(tm, m), min(tn, n), min(tk, kdim)
    assert m % tm == 0 and n % tn == 0 and kdim % tk == 0
    nk = kdim // tk

    def finish(g, u, o_ref):
        o_ref[...] = (g * (1.0 / (1.0 + jnp.exp(-g))) * u).astype(o_ref.dtype)

    def body(h_ref, wg_ref, wu_ref, o_ref, *acc):
        hh = h_ref[...]
        pg = jnp.dot(hh, wg_ref[...].astype(BF16), preferred_element_type=F32)
        pu = jnp.dot(hh, wu_ref[...].astype(BF16), preferred_element_type=F32)
        if nk == 1:
            finish(pg, pu, o_ref)
            return
        accg, accu = acc
        k = pl.program_id(2)

        @pl.when(k == 0)
        def _():
            accg[...] = pg
            accu[...] = pu

        if nk > 2:
            @pl.when((k > 0) & (k < nk - 1))
            def _():
                accg[...] += pg
                accu[...] += pu

        @pl.when(k == nk - 1)
        def _():
            finish(accg[...] + pg, accu[...] + pu, o_ref)

    block_bytes = _nbytes((tm, tk), BF16) + 2 * _nbytes((tk, tn), w_gate.dtype) + _nbytes((tm, tn), BF16)
    acc_bytes = _nbytes((tm, tn), F32) + _nbytes((tk, tn), BF16)
    w_spec = pl.BlockSpec((None,) * len(w_index) + (tk, tn), lambda i, j, k: (*w_index, k, j))
    return pl.pallas_call(
        body,
        grid=(m // tm, n // tn, nk),
        in_specs=[pl.BlockSpec((tm, tk), lambda i, j, k: (i, k), pipeline_mode=pl.Buffered(1)), w_spec, w_spec],
        out_specs=pl.BlockSpec((tm, tn), lambda i, j, k: (i, j)),
        out_shape=jax.ShapeDtypeStruct((m, n), BF16),
        scratch_shapes=[pltpu.VMEM((tm, tn), F32)] * 2 if nk > 1 else [],
        compiler_params=_params(("parallel", "parallel", "arbitrary"), block_bytes, 6 * acc_bytes),
        name="ffn_gate_up",
    )(h, w_gate, w_up)


def _half_ffn(x, mods, sub, rows_per_group, g, w_gate, w_up, w_down, w_index):
    h = _norm_mod(x, g, mods, sub, rows_per_group, BF16)
    u = _gate_up(h, w_gate, w_up, w_index)
    tn = TILE_N * jnp.dtype(BF16).itemsize // jnp.dtype(w_down.dtype).itemsize
    return _mm_residual(u, w_down, x, mods, sub, 0.5, rows_per_group, w_index=w_index,
                        tm=TILE_M, tn=tn, tk=TILE_K_LONG, resident_a=True, name="ffn_down_residual")


class ScoreSet(NamedTuple):
    q: object
    k_cols: slice
    write: object
    read: object


def _lane_fold(x, op):
    return functools.reduce(op, [x[:, c0:c0 + LANES] for c0 in range(0, x.shape[-1], LANES)])


def _attention_step(sets, k_refs, v_refs, chunk):
    state = [dict(mx=None, ls=None, acc=None, m_old=s.read[1][...]) for s in sets]
    off = 0
    for k_ref, v_ref in zip(k_refs, v_refs):
        total = k_ref.shape[0]
        for c0 in range(0, total, chunk):
            rows = min(chunk, total - c0)
            cols = slice(off + c0, off + c0 + rows)
            for s, st in zip(sets, state):
                if s.write is not None:
                    new = lax.dot_general(s.q, k_ref[c0:c0 + rows, s.k_cols], (((1,), (1,)), ((), ())),
                                          preferred_element_type=F32)
                    s.write[0][:, cols] = new
                    part = _lane_fold(new, jnp.maximum)
                    st["mx"] = part if st["mx"] is None else jnp.maximum(st["mx"], part)
            for s, st in zip(sets, state):
                m_old = jnp.concatenate([st["m_old"]] * (rows // LANES), axis=1) if rows > LANES else st["m_old"]
                e = jnp.exp(s.read[0][:, cols] - m_old)
                part = _lane_fold(e, jnp.add)
                st["ls"] = part if st["ls"] is None else st["ls"] + part
                pv = jnp.dot(e.astype(BF16), v_ref[c0:c0 + rows, :], preferred_element_type=F32)
                st["acc"] = pv if st["acc"] is None else st["acc"] + pv
        off += total
    out = []
    for s, st in zip(sets, state):
        if s.write is not None:
            s.write[1][...] = jnp.broadcast_to(jnp.max(st["mx"], axis=-1, keepdims=True), st["mx"].shape)
        out.append((st["acc"], jnp.sum(st["ls"], axis=-1, keepdims=True)))
    return out


def _skewed_steps(nq, score_bufs, emit):
    i = pl.program_id(2)
    buf_a, buf_b = score_bufs

    @pl.when(i == 0)
    def _():
        for pair in buf_b:
            for ref in pair:
                ref[...] = jnp.zeros_like(ref)

    even = lax.rem(i, 2) == 0

    @pl.when((i < nq) & even)
    def _():
        emit(buf_a, buf_b)

    @pl.when((i < nq) & jnp.logical_not(even))
    def _():
        emit(buf_b, buf_a)

    @pl.when(i == nq)
    def _():
        emit(None, buf_a if (nq - 1) % 2 == 0 else buf_b)


def _skewed_maps(nq):
    return (lambda bb, h, i: (bb, jnp.minimum(i, nq - 1), h)), (lambda bb, h, i: (bb, jnp.maximum(i - 1, 0), h))


def _kv_specs(arrays, width):
    return [pl.BlockSpec((None, a.shape[1], width), lambda bb, h, i: (bb, 0, h)) for a in arrays]


def _mla_attention(q, ks, vs, heads, *, tq=2 * ATTN_TQ):
    b, sq, _ = q.shape
    sk = sum(k.shape[1] for k in ks)
    tq = min(tq, sq)
    nkv = len(ks)

    nq = sq // tq
    q_map, o_map = _skewed_maps(nq)

    def body(q_ref, *refs):
        k_refs, v_refs = refs[:nkv], refs[nkv:2 * nkv]
        o_ref = refs[2 * nkv]
        s_a, m_a, s_b, m_b = refs[2 * nkv + 1:]

        def emit(write, read):
            sets = [ScoreSet(q_ref[...], slice(None), None if write is None else write[0], read[0])]
            (o, l), = _attention_step(sets, k_refs, v_refs, ATTN_CHUNK)
            o_ref[...] = (o / l).astype(o_ref.dtype)

        _skewed_steps(nq, ([(s_a, m_a)], [(s_b, m_b)]), emit)

    block_bytes = (_nbytes((tq, MLA_HEAD_PAD), BF16) + _nbytes((sk, MLA_HEAD_PAD), BF16)
                   + _nbytes((sk, V_DIM), BF16) + _nbytes((tq, V_DIM), BF16))
    score_bytes = _nbytes((tq, sk), F32)
    return pl.pallas_call(
        body,
        grid=(b, heads, nq + 1),
        in_specs=([pl.BlockSpec((None, tq, MLA_HEAD_PAD), q_map)]
                  + _kv_specs(ks, MLA_HEAD_PAD) + _kv_specs(vs, V_DIM)),
        out_specs=pl.BlockSpec((None, tq, V_DIM), o_map),
        out_shape=jax.ShapeDtypeStruct((b, sq, heads * V_DIM), BF16),
        scratch_shapes=[pltpu.VMEM((tq, sk), F32), pltpu.VMEM((tq, LANES), F32)] * 2,
        compiler_params=_params(("parallel", "parallel", "arbitrary"), block_bytes, 4 * score_bytes),
        name="mla_attention",
    )(q, *ks, *vs)


def _diff_attention(q, ks, vs, lam_p, subln_g, lambda_init, heads, *, tq=ATTN_TQ):
    b, sq, _ = q.shape
    sk = sum(k.shape[1] for k in ks)
    tq = min(tq, sq)
    hw = 2 * DIFF_DIM
    nkv = len(ks)
    nq = sq // tq
    q_map, o_map = _skewed_maps(nq)

    def body(lam_ref, g_ref, q_ref, *refs):
        k_refs, v_refs = refs[:nkv], refs[nkv:2 * nkv]
        o_ref = refs[2 * nkv]
        bufs = refs[2 * nkv + 1:]
        set_a = [(bufs[0], bufs[1]), (bufs[2], bufs[3])]
        set_b = [(bufs[4], bufs[5]), (bufs[6], bufs[7])]

        def emit(write, read):
            halves = [(q_ref[:, :DIFF_DIM], slice(0, DIFF_DIM)), (q_ref[:, DIFF_DIM:], slice(DIFF_DIM, hw))]
            sets = [ScoreSet(qh, cols, None if write is None else write[n], read[n])
                    for n, (qh, cols) in enumerate(halves)]
            (o1, l1), (o2, l2) = _attention_step(sets, k_refs, v_refs, ATTN_CHUNK)
            lp = lam_ref[...]
            lam = (jnp.exp(jnp.sum(lp[0:1] * lp[1:2], axis=-1, keepdims=True))
                   - jnp.exp(jnp.sum(lp[2:3] * lp[3:4], axis=-1, keepdims=True)) + lambda_init)
            o = o1 / l1 - (lam / l2) * o2
            o_ref[...] = (_rms(o, g_ref[...]) * (1.0 - lambda_init)).astype(o_ref.dtype)

        _skewed_steps(nq, (set_a, set_b), emit)

    block_bytes = 2 * _nbytes((tq, hw), BF16) + 2 * _nbytes((sk, hw), BF16)
    score_bytes = _nbytes((tq, sk), F32)
    return pl.pallas_call(
        body,
        grid=(b, heads, nq + 1),
        in_specs=([pl.BlockSpec((4, DIFF_DIM), lambda bb, h, i: (0, 0)),
                   pl.BlockSpec((1, hw), lambda bb, h, i: (0, 0)),
                   pl.BlockSpec((None, tq, hw), q_map)]
                  + _kv_specs(ks, hw) + _kv_specs(vs, hw)),
        out_specs=pl.BlockSpec((None, tq, hw), o_map),
        out_shape=jax.ShapeDtypeStruct((b, sq, heads * hw), BF16),
        scratch_shapes=[pltpu.VMEM((tq, sk), F32), pltpu.VMEM((tq, LANES), F32)] * 4,
        compiler_params=_params(("parallel", "parallel", "arbitrary"), block_bytes, 6 * score_bytes),
        name="diff_attention",
    )(lam_p.astype(F32), subln_g.reshape(1, hw).astype(F32), q, *ks, *vs)


def _hyena_filters(seq, w1, b1, w2, b2, w3, b3, freq, w_out, *, tl=256):
    hid = w1.shape[1]
    width2 = w_out.shape[1]
    width = width2 // 2
    bands = (HY_EMB - 1) // 2
    t = jnp.linspace(0.0, 1.0, seq, dtype=F32)[:, None]
    ang = ((2.0 * math.pi / seq) * jnp.arange(seq, dtype=F32)[:, None]
           * jnp.linspace(1e-4, bands - 1, bands, dtype=F32)[None, :])
    z = jnp.concatenate([t, jnp.cos(ang), -jnp.sin(ang), jnp.zeros((seq, LANES - HY_EMB), F32)], axis=-1)
    z = jnp.concatenate([z[0::2], z[1::2]], axis=0)
    deltas =jnp.abs(jnp.linspace(math.log(HY_TARGET) / HY_LONG_DECAY_PCT,
                                  math.log(HY_TARGET) / HY_SHORT_DECAY_PCT, width, dtype=F32))
    deltas2 = jnp.concatenate([deltas, deltas])[None, :]

    def pad2(w):
        return jnp.pad(w.astype(F32), ((0, LANES - w.shape[0]), (0, LANES - w.shape[1])))

    def pad_row(v):
        return jnp.pad(v.astype(F32), (0, LANES - v.shape[0]))[None, :]

    w_out_p = jnp.pad(w_out.astype(F32), ((0, LANES - hid), (0, 0)))
    tl, tn = min(tl, seq), width2
    exact = lax.Precision.HIGHEST

    def body(z_ref, w1_ref, b1_ref, w2_ref, b2_ref, w3_ref, b3_ref, f_ref, wo_ref, d_ref, o_ref, mid_ref):
        zz, f = z_ref[...], f_ref[...]
        a = jnp.sin(f * (jnp.dot(zz, w1_ref[...], precision=exact, preferred_element_type=F32) + b1_ref[...]))
        a = jnp.sin(f * (jnp.dot(a, w2_ref[...], precision=exact, preferred_element_type=F32) + b2_ref[...]))
        a = jnp.sin(f * (jnp.dot(a, w3_ref[...], precision=exact, preferred_element_type=F32) + b3_ref[...]))
        h = jnp.dot(a, wo_ref[...], precision=exact, preferred_element_type=F32)
        h = h * jnp.exp(-zz[:, 0:1] * d_ref[...])
        row = pl.program_id(0) * tl + lax.broadcasted_iota(jnp.int32, (tl, width), 0)
        h_fwd = h[:, :width]
        h_bwd = jnp.where(row == 0, 0.0, h[:, width:])
        h_sum = h_fwd + h_bwd
        o_ref[:, :width] = h_sum.astype(o_ref.dtype)
        o_ref[:, width:] = (h_fwd - h_bwd).astype(o_ref.dtype)
        signed = jnp.where(row >= seq // 2, 0.0, jnp.where((row & 1) == 0, h_sum, -h_sum))
        alternating = jnp.sum(signed, axis=0, keepdims=True)

        @pl.when(pl.program_id(0) == 0)
        def _():
            mid_ref[...] = alternating

        @pl.when(pl.program_id(0) > 0)
        def _():
            mid_ref[...] += alternating

    sq = pl.BlockSpec((LANES, LANES), lambda i: (0, 0))
    vec = pl.BlockSpec((1, LANES), lambda i: (0, 0))
    block_bytes = _nbytes((tl, LANES), F32) + _nbytes((LANES, tn), F32) + 2 * _nbytes((tl, tn), F32)
    return pl.pallas_call(
        body,
        grid=(seq // tl,),
        in_specs=[pl.BlockSpec((tl, LANES), lambda i: (i, 0)), sq, vec, sq, vec, sq, vec, vec,
                  pl.BlockSpec((LANES, tn), lambda i: (0, 0)),
                  pl.BlockSpec((1, tn), lambda i: (0, 0))],
        out_specs=[pl.BlockSpec((tl, tn), lambda i: (i, 0)), pl.BlockSpec((1, width), lambda i: (0, 0))],
        out_shape=[jax.ShapeDtypeStruct((seq, width2), BF16), jax.ShapeDtypeStruct((1, width), F32)],
        compiler_params=_params(("arbitrary",), block_bytes, 4 * _nbytes((tl, tn), F32)),
        name="hyena_filters",
    )(z, pad2(w1), pad_row(b1), pad2(w2), pad_row(b2), pad2(w3), pad_row(b3), pad_row(freq), w_out_p, deltas2)


def _hyena_prep(zh, conv_w, conv_b, *, tc=128):
    b, seq, w3 = zh.shape
    width = w3 // 3
    half = seq // 2
    tc = min(tc, width)
    nw = width // tc

    def body(z0_ref, z1_ref, z2_ref, w0_ref, w1_ref, w2_ref, b0_ref, b1_ref, b2_ref, x0_ref, vx_ref):
        row = lax.broadcasted_iota(jnp.int32, (half, 1), 0)

        def conv(z_ref, w_ref, b_ref):
            ev = z_ref[pl.ds(0, half, stride=2), :]
            od = z_ref[pl.ds(1, half, stride=2), :]
            w, bias = w_ref[...], b_ref[...]
            before_even = jnp.where(row == 0, 0.0, pltpu.roll(od, 1, 0))
            after_odd = jnp.where(row == half - 1, 0.0, pltpu.roll(ev, half - 1, 0))
            return (before_even * w[0:1] + ev * w[1:2] + od * w[2:3] + bias,
                    ev * w[0:1] + od * w[1:2] + after_odd * w[2:3] + bias)

        x0 = conv(z0_ref, w0_ref, b0_ref)
        x1 = conv(z1_ref, w1_ref, b1_ref)
        v = conv(z2_ref, w2_ref, b2_ref)
        for parity in range(2):
            x0_ref[parity] = x0[parity]
            vx_ref[parity] = (v[parity] * x1[parity]).astype(vx_ref.dtype)

    def zspec(part):
        return pl.BlockSpec((None, seq, tc), lambda bb, j: (bb, 0, part * nw + j))

    def wspec(rows, part):
        return pl.BlockSpec((rows, tc), lambda bb, j: (0, part * nw + j))

    out_spec = pl.BlockSpec((2, half, tc), lambda bb, j: (0, 0, bb * nw + j))
    block_bytes = 4 * _nbytes((seq, tc), F32) + _nbytes((seq, tc), BF16)
    return pl.pallas_call(
        body,
        grid=(b, nw),
        in_specs=[zspec(0), zspec(1), zspec(2),
                  wspec(3, 0), wspec(3, 1), wspec(3, 2), wspec(1, 0), wspec(1, 1), wspec(1, 2)],
        out_specs=[out_spec, out_spec],
        out_shape=[jax.ShapeDtypeStruct((2, half, b * width), F32),
                   jax.ShapeDtypeStruct((2, half, b * width), BF16)],
        compiler_params=_params(("parallel", "parallel"), block_bytes, 8 * _nbytes((seq, tc), F32)),
        name="hyena_short_conv",
    )(zh, zh, zh, conv_w, conv_w, conv_w, conv_b[None, :], conv_b[None, :], conv_b[None, :])


def _dft_matrices(seq):
    n = 2 * seq
    half = seq // 2
    assert seq % 2 == 0 and n % 4 == 0
    idx = jnp.arange(half, dtype=jnp.int32)
    alt = jnp.where((idx & 1) == 0, 1.0, -1.0).astype(F32)

    def cos_sin(k, s):
        phase = _mod_nonneg(k * s, n)
        return _cos_turns(phase, n), _cos_turns(_mod_nonneg(phase + (n - n // 4), n), n)

    def forward(c, s):
        return jnp.concatenate([c, jnp.where(idx[:, None] == 0, alt[None, :], s)], axis=0).astype(BF16)

    def inverse(c, s):
        return jnp.concatenate([c, jnp.where(idx[None, :] == 0, alt[:, None], s)], axis=1).astype(BF16)

    kb = math.gcd(half, 64)
    k_hi, k_lo = jnp.arange(half // kb, dtype=jnp.int32) * kb, jnp.arange(kb, dtype=jnp.int32)

    def by_rows(sample):
        (ch, sh), (cl, sl) = cos_sin(k_hi[:, None], sample[None, :]), cos_sin(k_lo[:, None], sample[None, :])
        return ((ch[:, None, :] * cl[None] - sh[:, None, :] * sl[None]).reshape(half, half),
                (sh[:, None, :] * cl[None] + ch[:, None, :] * sl[None]).reshape(half, half))

    def by_cols(sample):
        (ch, sh), (cl, sl) = cos_sin(k_hi[None, :], sample[:, None]), cos_sin(k_lo[None, :], sample[:, None])
        return ((ch[:, :, None] * cl[:, None, :] - sh[:, :, None] * sl[:, None, :]).reshape(half, half),
                (sh[:, :, None] * cl[:, None, :] + ch[:, :, None] * sl[:, None, :]).reshape(half, half))

    even = by_rows(2 * idx)
    return forward(*even), forward(*by_rows(2 * idx + 1)), inverse(*even), inverse(*by_cols(2 * idx + 1))


def _div_nonneg(x, d):
    return x >> (d.bit_length() - 1) if d & (d - 1) == 0 else x // d


def _mod_nonneg(x, d):
    return x & (d - 1) if d & (d - 1) == 0 else x % d


def _cos_turns(phase, n):
    quarter = n // 4
    quad = _div_nonneg(phase, quarter)
    rem = phase - quad * quarter
    odd = (quad & 1) == 1
    x = jnp.where(odd, quarter - rem, rem).astype(F32) * (2.0 * math.pi / n)
    x2 = x * x
    acc = jnp.full_like(x2, 1.0 / math.factorial(16))
    for order in range(14, -1, -2):
        acc = acc * (-x2) + 1.0 / math.factorial(order)
    return jnp.where((quad == 1) | (quad == 2), -acc, acc)


def _spectrum_multiply(eu, ou, eg, og, mid_g, batch, width, *, tr=256, tc=512):
    seq = eu.shape[0]
    half = seq // 2
    n = 2 * seq
    tr, tc = min(tr, half), min(tc, width)
    nwc = width // tc

    def body(eu_ref, ou_ref, eg_ref, og_ref, mid_ref, pe_ref, po_ref):
        row = pl.program_id(0) * tr + lax.broadcasted_iota(jnp.int32, (tr, tc), 0)
        first = row == 0

        def pair(e_ref, o_ref):
            ec, es, oc, os_ = e_ref[0], e_ref[1], o_ref[0], o_ref[1]
            return (ec + oc, jnp.where(first, 0.0, es + os_)), (ec - oc, jnp.where(first, 0.0, os_ - es))

        def product(u, g):
            return u[0] * g[0] - u[1] * g[1], u[0] * g[1] + u[1] * g[0]

        (u_lo, u_hi), (g_lo, g_hi) = pair(eu_ref, ou_ref), pair(eg_ref, og_ref)
        ya_lo, yb_lo = product(u_lo, g_lo)
        ya_hi, yb_hi = product(u_hi, g_hi)
        ya_mid, yb_mid = product((eu_ref[1], ou_ref[1]), (mid_ref[...], og_ref[1]))
        w_cos = jnp.where(first, 1.0 / n, 2.0 / n)
        w_sin = 2.0 / n
        pe_ref[0] = (w_cos * (ya_lo + ya_hi)).astype(pe_ref.dtype)
        po_ref[0] = (w_cos * (ya_lo - ya_hi)).astype(po_ref.dtype)
        pe_ref[1] = (w_sin * jnp.where(first, ya_mid, yb_lo - yb_hi)).astype(pe_ref.dtype)
        po_ref[1] = (w_sin * jnp.where(first, yb_mid, yb_lo + yb_hi)).astype(po_ref.dtype)

    u_spec = pl.BlockSpec((2, tr, tc), lambda i, j, bb: (0, i, bb * nwc + j))
    g_spec = pl.BlockSpec((2, tr, tc), lambda i, j, bb: (0, i, j))
    block_bytes = 4 * _nbytes((2, tr, tc), F32) + 2 * _nbytes((2, tr, tc), BF16)
    out_sds = jax.ShapeDtypeStruct((2, half, batch * width), BF16)
    pe, po = pl.pallas_call(
        body,
        grid=(half // tr, nwc, batch),
        in_specs=[u_spec, u_spec, g_spec, g_spec, pl.BlockSpec((1, tc), lambda i, j, bb: (0, j))],
        out_specs=[u_spec, u_spec],
        out_shape=[out_sds, out_sds],
        compiler_params=_params(("parallel", "parallel", "arbitrary"), block_bytes, 24 * _nbytes((tr, tc), F32)),
        name="hyena_spectrum_multiply",
    )(eu.reshape(2, half, batch * width), ou.reshape(2, half, batch * width),
      eg.reshape(2, half, width), og.reshape(2, half, width), mid_g)
    return pe.reshape(seq, batch * width), po.reshape(seq, batch * width)


def _hyena_branch(zh, conv_w, conv_b, filt, hy_bias):
    b, seq, w3 = zh.shape
    width = w3 // 3
    half = seq // 2
    x0, vx = _hyena_prep(zh, conv_w.astype(F32), conv_b.astype(F32))
    h_sum_dif, mid_g = _hyena_filters(seq, *filt)
    h_sum_dif = h_sum_dif.reshape(2, half, 2 * width)
    mats = _dft_matrices(seq)
    forward_mats, inverse_mats = mats[:2], mats[2:]
    tm, tn = min(TILE_M, half), min(TILE_N, width)
    nw = width // tn

    def store(acc, ex, outs):
        outs[0][...] = acc

    def forward(mat, samples, parity, cols, col_shift, name):
        return _matmul(mat, samples, tm=tm, tn=tn, tk=TILE_K, epilogue=store, w_index=(parity,),
                       w_cols=(0, cols), w_col_shift=col_shift,
                       out_shape=[jax.ShapeDtypeStruct((seq, cols), F32)],
                       out_specs=[pl.BlockSpec((tm, tn), lambda i, j, k: (i, j))], name=name)[0]

    def filter_cols(i):
        return (i // (half // tm)) * nw

    data = [forward(m, vx, p, b * width, None, "hyena_dft_forward") for p, m in enumerate(forward_mats)]
    filt_spec = [forward(m, h_sum_dif, p, width, filter_cols, "hyena_dft_filters")
                 for p, m in enumerate(forward_mats)]
    spectra = _spectrum_multiply(data[0], data[1], filt_spec[0], filt_spec[1], mid_g, b, width)

    return _hyena_inverse(inverse_mats, spectra, x0, vx, hy_bias.astype(F32)[None, :], b, width)


def _hyena_inverse(mats, spectra, x0, vx, bias, batch, width):
    half, seq = mats[0].shape
    tm, tn = min(TILE_M // 2, half), min(TILE_N, width)
    nw = width // tn

    def body(me_ref, mo_ref, pe_ref, po_ref, x0_ref, vx_ref, bias_ref, o_ref, rows_ref):
        for parity, (m_ref, p_ref) in enumerate(((me_ref, pe_ref), (mo_ref, po_ref))):
            conv = jnp.dot(m_ref[...], p_ref[...], preferred_element_type=F32)
            gated = x0_ref[parity] * (conv + bias_ref[...] * vx_ref[parity].astype(F32))
            for c in range(tn // LANES):
                rows_ref[c, pl.ds(parity, tm, stride=2), :] = gated[:, c * LANES:(c + 1) * LANES]
        for c in range(tn // LANES):
            o_ref[:, c * LANES:(c + 1) * LANES] = rows_ref[c].astype(o_ref.dtype)

    mat_spec = pl.BlockSpec((tm, seq), lambda i, j: (i, 0))
    spec_spec = pl.BlockSpec((seq, tn), lambda i, j: (0, j))
    split_spec = pl.BlockSpec((2, tm, tn), lambda i, j: (0, i, j))
    block_bytes = (2 * _nbytes((tm, seq), BF16) + 2 * _nbytes((seq, tn), BF16) + _nbytes((2, tm, tn), F32)
                   + 2 * _nbytes((2, tm, tn), BF16))
    return pl.pallas_call(
        body,
        grid=(half // tm, batch * width // tn),
        in_specs=[mat_spec, mat_spec, spec_spec, spec_spec, split_spec, split_spec,
                  pl.BlockSpec((1, tn), lambda i, j: (0, j % nw))],
        out_specs=pl.BlockSpec((None, 2 * tm, tn), lambda i, j: (j // nw, i, j % nw)),
        out_shape=jax.ShapeDtypeStruct((batch, 2 * half, width), BF16),
        scratch_shapes=[pltpu.VMEM((tn // LANES, 2 * tm, LANES), F32)],
        compiler_params=_params(("parallel", "parallel"), block_bytes, 8 * _nbytes((2 * tm, tn), F32)),
        name="hyena_dft_inverse",
    )(mats[0], mats[1], spectra[0], spectra[1], x0, vx, bias)


def _ab_latents(h, w1, q_norm_g, kv_norm_g, tables, seq, q_lora, kv_lora, *, tm=512, tk=TILE_K):
    m = h.shape[0]
    n1 = w1.shape[1]
    tm = min(tm, m, seq)
    tiles_per_seq = seq // tm

    def epilogue(acc, ex, outs):
        outs[0][...] = _rms(acc[:, :q_lora], ex[0][...]).astype(BF16)
        outs[1][...] = _rms(acc[:, q_lora:q_lora + kv_lora], ex[1][...]).astype(BF16)
        kr = acc[:, q_lora + kv_lora:]
        if tables is not None:
            kr = _rope_apply(kr, ex[2], ex[3], MLA_ROPE_LAYOUT)
        outs[2][...] = kr

    extras = [(q_norm_g.astype(F32)[None, :], pl.BlockSpec((1, q_lora), lambda i, j, k: (0, 0))),
              (kv_norm_g.astype(F32)[None, :], pl.BlockSpec((1, kv_lora), lambda i, j, k: (0, 0)))]
    if tables is not None:
        spec = pl.BlockSpec((tm, LANES), lambda i, j, k: (i % tiles_per_seq, 0))
        extras += [(tables[0], spec), (tables[1], spec)]
    widths = (q_lora, kv_lora, MLA_HEAD_PAD)
    dtypes = (BF16, BF16, F32)
    return _matmul(h, w1, tm=tm, tn=n1, tk=tk, epilogue=epilogue, extras=extras,
                   out_shape=[jax.ShapeDtypeStruct((m, wd), dt) for wd, dt in zip(widths, dtypes)],
                   out_specs=[pl.BlockSpec((tm, wd), lambda i, j, k: (i, 0)) for wd in widths],
                   name="mla_latent_projection")


def _mla_keys(ckv, w_uk, kr, heads, *, tm=1024, tn=1024):
    m = ckv.shape[0]
    n = heads * MLA_HEAD_PAD
    tm, tn = min(tm, m), min(tn, n)
    reps = tn // MLA_HEAD_PAD

    def epilogue(acc, ex, outs):
        rot = ex[0][...]
        if reps > 1:
            rot = jnp.concatenate([rot] * reps, axis=1)
        outs[0][...] = (acc + rot).astype(BF16)

    extras = [(kr, pl.BlockSpec((tm, MLA_HEAD_PAD), lambda i, j, k: (i, 0)))]
    return _matmul(ckv, w_uk, tm=tm, tn=tn, tk=ckv.shape[1], epilogue=epilogue, extras=extras,
                   out_shape=[jax.ShapeDtypeStruct((m, n), BF16)],
                   out_specs=[pl.BlockSpec((tm, tn), lambda i, j, k: (i, j))], name="mla_key_up")[0]


def _ab_mixer(h, hc, batch, seq, seq_c, need_ctx, w_in, q_norm_g, kv_norm_g, w_uq, w_ukv,
              conv_w, conv_b, filt, hy_bias):
    d = w_in.shape[0]
    heads = d // (2 * V_DIM)
    q_lora, kv_lora = w_uq.shape[0], w_ukv.shape[0]
    kv_end = q_lora + kv_lora + ROPE_DIM
    mla_scale = (NOPE_DIM + ROPE_DIM) ** -0.5
    zero_pad = MLA_HEAD_PAD - NOPE_DIM - ROPE_DIM

    w1 = jnp.concatenate([w_in[:, :q_lora + kv_lora], jnp.zeros((d, NOPE_DIM), w_in.dtype),
                          w_in[:, q_lora + kv_lora:kv_end], jnp.zeros((d, zero_pad), w_in.dtype)],
                         axis=1).astype(BF16)
    w_hy = w_in[:, kv_end:].astype(BF16)
    uq = w_uq.reshape(q_lora, heads, NOPE_DIM + ROPE_DIM)
    w_uq_p = jnp.pad(uq, ((0, 0), (0, 0), (0, zero_pad))).reshape(q_lora, heads * MLA_HEAD_PAD).astype(BF16)
    ukv = w_ukv.reshape(kv_lora, heads, NOPE_DIM + V_DIM)
    w_uk_p = jnp.pad(ukv[..., :NOPE_DIM], ((0, 0), (0, 0), (0, MLA_HEAD_PAD - NOPE_DIM))
                     ).reshape(kv_lora, heads * MLA_HEAD_PAD).astype(BF16)
    w_uv = ukv[..., NOPE_DIM:].reshape(kv_lora, heads * V_DIM).astype(BF16)

    tables = _rope_tables(seq, MLA_ROPE_LAYOUT)
    rope_kw = dict(layout=MLA_ROPE_LAYOUT)

    cq, ckv, kr = _ab_latents(h, w1, q_norm_g, kv_norm_g, tables, seq, q_lora, kv_lora)
    cqc, ckvc, krc = _ab_latents(hc, w1, q_norm_g, kv_norm_g, None, seq_c, q_lora, kv_lora)
    q = _mm_rope(cq, w_uq_p, tables, mla_scale, seq, BF16, name="mla_query_up", **rope_kw)
    k = _mla_keys(ckv, w_uk_p, kr, heads)
    v = _mm_plain(ckv, w_uv, BF16, name="mla_value_up")
    kc = _mla_keys(ckvc, w_uk_p, krc, heads)
    vc = _mm_plain(ckvc, w_uv, BF16, name="mla_value_up")

    def split(t, s):
        return t.reshape(batch, s, t.shape[-1])

    att = _mla_attention(split(q, seq), [split(kc, seq_c), split(k, seq)], [split(vc, seq_c), split(v, seq)], heads)
    zh = _mm_plain(h, w_hy, F32, tn=512, name="hyena_in_projection")
    hy = _hyena_branch(split(zh, seq), conv_w, conv_b, filt, hy_bias)
    def flat(t):
        return t.reshape(t.shape[0] * t.shape[1], t.shape[2])

    mix = [flat(att), flat(hy)]
    if not need_ctx:
        return mix, None
    qc = _mm_rope(cqc, w_uq_p, None, mla_scale, seq_c, BF16, name="mla_query_up", **rope_kw)
    attc = _mla_attention(split(qc, seq_c), [split(kc, seq_c)], [split(vc, seq_c)], heads)
    zhc = _mm_plain(hc, w_hy, F32, tn=512, name="hyena_in_projection")
    hyc = _hyena_branch(split(zhc, seq_c), conv_w, conv_b, filt, hy_bias)
    return mix, [flat(attc), flat(hyc)]


def _diff_mixer(h, hc, batch, seq, seq_c, need_ctx, w_in, lam_p, subln_g, lambda_init):
    d = w_in.shape[0]
    hd = w_in.shape[1] // 3
    heads = hd // (2 * DIFF_DIM)
    scale = DIFF_DIM ** -0.5
    w = _cast_deinterleave_columns(w_in, 2 * hd)
    q_cols, k_cols, v_cols = (0, hd), (hd, hd), (2 * hd, hd)
    tables = _rope_tables(seq, DIFF_ROPE_LAYOUT)
    rope_kw = dict(layout=DIFF_ROPE_LAYOUT)

    q = _mm_rope(h, w, tables, scale, seq, BF16, w_cols=q_cols, name="diff_query_projection", **rope_kw)
    k = _mm_rope(h, w, tables, 1.0, seq, BF16, w_cols=k_cols, name="diff_key_projection", **rope_kw)
    v = _mm_plain(h, w, BF16, w_cols=v_cols, name="diff_value_projection")
    kc = _mm_rope(hc, w, None, 1.0, seq_c, BF16, w_cols=k_cols, name="diff_key_projection", **rope_kw)
    vc = _mm_plain(hc, w, BF16, w_cols=v_cols, name="diff_value_projection")

    def split(t, s):
        return t.reshape(batch, s, t.shape[-1])

    o = _diff_attention(split(q, seq), [split(kc, seq_c), split(k, seq)], [split(vc, seq_c), split(v, seq)],
                        lam_p, subln_g, lambda_init, heads)
    mix = o.reshape(batch * seq, hd)
    if not need_ctx:
        return mix, None
    qc = _mm_rope(hc, w, None, scale, seq_c, BF16, w_cols=q_cols, name="diff_query_projection", **rope_kw)
    oc = _diff_attention(split(qc, seq_c), [split(kc, seq_c)], [split(vc, seq_c)], lam_p, subln_g, lambda_init, heads)
    return mix, oc.reshape(batch * seq_c, hd)


def kernel(x, c, ctx, c_ctx, mod_w, mod_b, norm_g, ffn_w_gate, ffn_w_up, ffn_w_down, ab_w_in, mla_q_norm_g, mla_kv_norm_g, mla_w_uq, mla_w_ukv, hy_conv_w, hy_conv_b, hy_w1, hy_b1, hy_w2, hy_b2, hy_w3, hy_b3, hy_freq, hy_w_out, hy_bias, ab_w_out, c_w_in, c_lambda, c_subln_g, c_w_out, final_norm_g):
    batch, seq, d = x.shape
    seq_c = ctx.shape[1]
    depth = mod_w.shape[0]
    rows_c = batch * seq_c
    xs = x.reshape(batch * seq, d).astype(F32)
    xc = ctx.reshape(rows_c, d).astype(F32)

    cvec = jnp.concatenate([c.astype(F32), c_ctx.astype(F32)[None, :]], axis=0)
    cvec = jnp.pad(cvec, ((0, -cvec.shape[0] % 8), (0, 0)))
    mods_all = _modulation(cvec, mod_w.astype(F32), mod_b.astype(F32))

    for layer in range(depth):
        need_ctx = layer < depth - 1
        mods = mods_all[layer, :batch].reshape(batch, 3 * N_SUB, d)
        modc = mods_all[layer, batch:batch + 1].reshape(1, 3 * N_SUB, d)
        g = norm_g[layer].astype(F32)

        def ffn(t, m, rows, sub, which):
            return _half_ffn(t, m, sub, rows, g[sub], ffn_w_gate, ffn_w_up, ffn_w_down, (layer, which))

        xs = ffn(xs, mods, seq, 0, 0)
        xc = ffn(xc, modc, rows_c, 0, 0)
        h = _norm_mod(xs, g[1], mods, 1, seq, BF16)
        hc = _norm_mod(xc, g[1], modc, 1, rows_c, BF16)
        i = layer // 2
        if layer % 2 == 0:
            filt = (hy_w1[i], hy_b1[i], hy_w2[i], hy_b2[i], hy_w3[i], hy_b3[i], hy_freq[i], hy_w_out[i])
            mix, mixc = _ab_mixer(h, hc, batch, seq, seq_c, need_ctx, ab_w_in[i], mla_q_norm_g[i],
                                  mla_kv_norm_g[i], mla_w_uq[i], mla_w_ukv[i], hy_conv_w[i], hy_conv_b[i],
                                  filt, hy_bias[i])
            w_out = ab_w_out[i].astype(BF16)
        else:
            lambda_init = 0.8 - 0.6 * math.exp(-0.3 * layer)
            mix, mixc = _diff_mixer(h, hc, batch, seq, seq_c, need_ctx, c_w_in[i], c_lambda[i], c_subln_g[i],
                                    lambda_init)
            w_out = c_w_out[i].astype(BF16)
        xs = _mm_residual(mix, w_out, xs, mods, 1, 1.0, seq, name="mixer_out_residual")
        xs = ffn(xs, mods, seq, 2, 1)
        if need_ctx:
            xc = _mm_residual(mixc, w_out, xc, modc, 1, 1.0, rows_c, name="mixer_out_residual")
            xc = ffn(xc, modc, rows_c, 2, 1)

    out = _norm_mod(xs, final_norm_g.astype(F32), None, 0, seq, x.dtype)
    return out.reshape(batch, seq, d)
```
